```python
import math
import jax
import jax.numpy as jnp
from jax import lax
import numpy as np

D_MODEL = 1024
BATCH = 2
SEQ = 16384
DEPTH = 2

GRID_W = 64
CTX_LEN = 256
F32 = jnp.float32

D_FF = 2816
BR_WIDTH = D_MODEL // 2
A_DK = 128
A_WIDTH = BR_WIDTH
A_HEADS = A_WIDTH // A_DK
A_DV = A_WIDTH // A_HEADS
A_CHUNK = 64
B_WIDTH = BR_WIDTH
B_ORDER = 2
B_EMB = 33
B_BANDS = (B_EMB - 1) // 2
B_FFN = 64
B_FAST_DECAY = 0.3
B_SLOW_DECAY = 1.5
B_TARGET = 1e-2
C_HEAD_DIM = 64
C_HEADS = BR_WIDTH // C_HEAD_DIM
C_KV_HEADS = 2
C_GROUP = C_HEADS // C_KV_HEADS
C_WIDTH = C_HEADS * C_HEAD_DIM
C_KV_WIDTH = C_KV_HEADS * C_HEAD_DIM
C_WINDOW = 128
C_BLOCK = C_WINDOW
ROPE_BASE = 10000.0
N_BRANCH = 3
N_MOD = 9
OFF_B = 5 * A_WIDTH
OFF_Q = OFF_B + (B_ORDER + 1) * B_WIDTH
OFF_K = OFF_Q + C_WIDTH
OFF_V = OFF_K + C_KV_WIDTH
OFF_G = OFF_V + C_KV_WIDTH
IN_WIDTH = OFF_G + N_BRANCH * D_MODEL
IN_SPLITS = (OFF_B, OFF_Q, OFF_K, OFF_V, OFF_G)
DN_ALPHA = (2.0 * DEPTH) ** 0.25
DN_BETA = (8.0 * DEPTH) ** -0.25
LN_EPS = 1e-5
RMS_EPS = 1e-6

kernel_name = 'hybrid_hgrn2_hyena_swa_flow_block'


def layer_norm(x, g, b):
    xf = x.astype(F32)
    mu = jnp.mean(xf, axis=-1, keepdims=True)
    var = jnp.mean(jnp.square(xf - mu), axis=-1, keepdims=True)
    return ((xf - mu) * lax.rsqrt(var + LN_EPS) * g.astype(F32) + b.astype(F32)).astype(x.dtype)


def modulate(x, shift, scale):
    return x * (1.0 + scale) + shift


def swiglu(u, w_in, w_out):
    a, b = jnp.split(u @ w_in, 2, axis=-1)
    return (jax.nn.silu(a) * b) @ w_out


def macaron_ffn(h, shift, scale, gate, w_in, w_out, g, b):
    y = swiglu(modulate(h, shift, scale), w_in, w_out)
    return layer_norm(DN_ALPHA * h + 0.5 * gate * y, g, b)


def _heads(a):
    n, b, l, _ = a.shape
    return a.reshape(n, b, l, A_HEADS, -1).transpose(0, 1, 3, 2, 4)


def hgrn_streams(pa, lb):
    q, i, g, zf, zb = jnp.split(pa.astype(F32), 5, axis=-1)
    lbx = lb.astype(F32)[:, None, None, :]
    z = jnp.stack([zf, zb[:, ::-1]], 0)
    logf = jnp.log(lbx + (1.0 - lbx) * jax.nn.sigmoid(z))
    k = (1.0 - lbx) * jax.nn.sigmoid(-z)
    qd = jnp.stack([q, q[:, ::-1]], 0)
    vd = jnp.stack([i, i[:, ::-1]], 0)
    return _heads(qd), _heads(k), _heads(vd), _heads(logf), g


def hgrn_chunk_scan(q, k, v, logf, s0):
    n, b, h, L, _ = q.shape
    nc = L // A_CHUNK

    def to_chunks(a):
        return jnp.moveaxis(a.reshape(n, b, h, nc, A_CHUNK, a.shape[-1]), 3, 0)

    mask = jnp.tril(jnp.ones((A_CHUNK, A_CHUNK), bool))[:, :, None]

    def step(S, inp):
        qc, kc, vc, gc = inp
        bcum = jnp.cumsum(gc, axis=-2)
        inter = jnp.einsum('nbhtk,nbhkv->nbhtv', qc * jnp.exp(bcum), S)
        rel = jnp.exp(jnp.where(mask, bcum[..., :, None, :] - bcum[..., None, :, :], -jnp.inf))
        att = jnp.einsum('nbhtk,nbhsk,nbhtsk->nbhts', qc, kc, rel)
        out = inter + jnp.einsum('nbhts,nbhsv->nbhtv', att, vc)
        blast = bcum[..., -1:, :]
        S = jnp.exp(blast[..., 0, :])[..., None] * S + jnp.einsum(
            'nbhsk,nbhsv->nbhkv', kc * jnp.exp(blast - bcum), vc)
        return S, out

    s_fin, out = lax.scan(step, s0, (to_chunks(q), to_chunks(k), to_chunks(v), to_chunks(logf)))
    out = jnp.moveaxis(out, 0, 3).reshape(n, b, h, L, -1)
    return out, s_fin


def hgrn_readout(out, g, norm_w):
    o = out[0] + out[1][:, :, ::-1]
    o = o.transpose(0, 2, 1, 3)
    o = o * lax.rsqrt(jnp.mean(jnp.square(o), axis=-1, keepdims=True) + RMS_EPS)
    o = o * norm_w.astype(F32).reshape(A_HEADS, A_DV)
    bsz, L = o.shape[:2]
    return o.reshape(bsz, L, A_WIDTH) * jax.nn.silu(g)


def short_conv3(x, w, b):
    xp = jnp.pad(x, ((0, 0), (1, 1), (0, 0)))
    return xp[:, :-2] * w[0] + xp[:, 1:-1] * w[1] + xp[:, 2:] * w[2] + b


def hyena_decay_rates():
    max_decay = math.log(B_TARGET) / B_FAST_DECAY
    min_decay = math.log(B_TARGET) / B_SLOW_DECAY
    return jnp.abs(jnp.linspace(min_decay, max_decay, B_WIDTH, dtype=F32))


def hyena_filters(L, w1, b1, f1, w2, b2, f2, w3):
    pos = jnp.arange(L, dtype=F32)
    t = pos / (L - 1)
    w = 2.0 * math.pi * pos / L
    bands = jnp.linspace(1e-4, B_BANDS - 1, B_BANDS, dtype=F32)
    feats = jnp.concatenate([t[:, None], jnp.cos(w[:, None] * bands), -jnp.sin(w[:, None] * bands)], axis=-1)
    h = jnp.sin(f1.astype(F32) * (feats @ w1.astype(F32) + b1.astype(F32)))
    h = jnp.sin(f2.astype(F32) * (h @ w2.astype(F32) + b2.astype(F32)))
    h = (h @ w3.astype(F32)).reshape(L, 2, B_ORDER, B_WIDTH)
    h = h * jnp.exp(-t[:, None] * hyena_decay_rates())[:, None, None, :]
    h_fwd = h[:, 0]
    h_bwd = h[:0:-1, 1]
    l1 = jnp.sum(jnp.abs(h_fwd), axis=0) + jnp.sum(jnp.abs(h_bwd), axis=0)
    filt = jnp.concatenate([h_fwd, jnp.zeros((1, B_ORDER, B_WIDTH), F32), h_bwd], axis=0)
    return filt / l1


def long_conv(z, filt):
    L = z.shape[1]
    zf = jnp.fft.rfft(z, n=2 * L, axis=1)
    ff = jnp.fft.rfft(filt, n=2 * L, axis=0)
    return jnp.fft.irfft(zf * ff[None], n=2 * L, axis=1)[:, :L]


def hyena_mixer(p, conv_w, conv_b, w1, b1, f1, w2, b2, f2, w3, bias):
    L = p.shape[1]
    parts = jnp.split(short_conv3(p, conv_w, conv_b).astype(F32), B_ORDER + 1, axis=-1)
    filt = hyena_filters(L, w1, b1, f1, w2, b2, f2, w3)
    z = parts[0]
    for n in range(B_ORDER):
        z = parts[n + 1] * (long_conv(z, filt[:, n]) + bias[n].astype(F32) * z)
    return z


def axial_rope_tables(L):
    n_rows = L // GRID_W
    row = jnp.repeat(jnp.arange(n_rows), GRID_W).astype(F32)
    col = jnp.tile(jnp.arange(GRID_W), n_rows).astype(F32)
    nf = C_HEAD_DIM // 4
    inv = ROPE_BASE ** (-jnp.arange(nf, dtype=F32) * 2.0 / (C_HEAD_DIM // 2))
    ang = jnp.concatenate([row[:, None] * inv, col[:, None] * inv], axis=-1)
    return jnp.cos(ang), jnp.sin(ang)


def apply_axial_rope(x, cos, sin):
    bsz, L, hh, _ = x.shape
    nf = C_HEAD_DIM // 4
    xr = x.astype(F32).reshape(bsz, L, hh, 2, 2, nf)
    x1, x2 = xr[..., 0, :], xr[..., 1, :]
    cs = cos.reshape(1, L, 1, 2, nf)
    sn = sin.reshape(1, L, 1, 2, nf)
    out = jnp.stack([x1 * cs - x2 * sn, x2 * cs + x1 * sn], axis=-2)
    return out.reshape(x.shape).astype(x.dtype)


def _sink_column(sink, shape):
    s = sink.astype(F32).reshape(C_KV_HEADS, C_GROUP)[:, :, None, None]
    return jnp.broadcast_to(s, shape)


def windowed_attention(q, k, v, k_ctx, v_ctx, sink):
    bsz, S = q.shape[:2]
    nb = S // C_BLOCK
    scale = C_HEAD_DIM ** -0.5
    qb = q.reshape(bsz, nb, C_BLOCK, C_KV_HEADS, C_GROUP, C_HEAD_DIM)

    def band(a):
        ap = jnp.pad(a, ((0, 0), (C_BLOCK, C_BLOCK), (0, 0), (0, 0)))
        ap = ap.reshape(bsz, nb + 2, C_BLOCK, C_KV_HEADS, C_HEAD_DIM)
        return jnp.concatenate([ap[:, :-2], ap[:, 1:-1], ap[:, 2:]], axis=2)

    kw, vw = band(k), band(v)
    s_loc = jnp.einsum('bnqhgd,bnkhd->bnhgqk', qb, kw).astype(F32) * scale
    s_ctx = jnp.einsum('bnqhgd,bchd->bnhgqc', qb, k_ctx).astype(F32) * scale
    blk = jnp.arange(nb)[:, None, None] * C_BLOCK
    qpos = blk + jnp.arange(C_BLOCK)[None, :, None]
    kpos = blk - C_BLOCK + jnp.arange(3 * C_BLOCK)[None, None, :]
    valid = (jnp.abs(qpos - kpos) <= C_WINDOW) & (kpos >= 0) & (kpos < S)
    s_loc = jnp.where(valid[None, :, None, None], s_loc, -jnp.inf)
    sink_col = _sink_column(sink, s_loc.shape[:-1] + (1,))
    p = jax.nn.softmax(jnp.concatenate([s_loc, s_ctx, sink_col], axis=-1), axis=-1).astype(v.dtype)
    nw = 3 * C_BLOCK
    lc = k_ctx.shape[1]
    o = jnp.einsum('bnhgqk,bnkhd->bnqhgd', p[..., :nw], vw) + jnp.einsum(
        'bnhgqc,bchd->bnqhgd', p[..., nw:nw + lc], v_ctx)
    return o.reshape(bsz, S, C_WIDTH)


def context_attention(q_ctx, k_ctx, v_ctx, sink):
    bsz, lc = q_ctx.shape[:2]
    qg = q_ctx.reshape(bsz, lc, C_KV_HEADS, C_GROUP, C_HEAD_DIM)
    s = jnp.einsum('bqhgd,bkhd->bhgqk', qg, k_ctx).astype(F32) * (C_HEAD_DIM ** -0.5)
    sink_col = _sink_column(sink, s.shape[:-1] + (1,))
    p = jax.nn.softmax(jnp.concatenate([s, sink_col], axis=-1), axis=-1).astype(v_ctx.dtype)
    o = jnp.einsum('bhgqk,bkhd->bqhgd', p[..., :lc], v_ctx)
    return o.reshape(bsz, lc, C_WIDTH)


def merge_branches(ys, pg, w_branch, w_out):
    gates = jnp.split(pg, N_BRANCH, axis=-1)
    m = jax.nn.sigmoid(gates[0]) * (ys[0] @ w_branch[0])
    for n in range(1, N_BRANCH):
        m = m + jax.nn.sigmoid(gates[n]) * (ys[n] @ w_branch[n])
    return m @ w_out


def token_mixer(u, uc, w_in, lb, a_norm_w, b_conv_w, b_conv_b, b_w1, b_b1, b_f1, b_w2, b_b2, b_f2, b_w3,
                b_bias, sink, w_branch, w_out, with_ctx_out):
    bsz, S, _ = u.shape
    lc = uc.shape[1]
    pa, pb, pq, pk, pv, pg = jnp.split(u @ w_in, IN_SPLITS, axis=-1)
    ca, cb, cq, ck, cv, cg = jnp.split(uc @ w_in, IN_SPLITS, axis=-1)
    filt = (b_w1, b_b1, b_f1, b_w2, b_b2, b_f2, b_w3)

    qa_c, ka_c, va_c, lf_c, ga_c = hgrn_streams(ca, lb)
    s0 = jnp.zeros((2, bsz, A_HEADS, A_DK, A_DV), F32)
    oa_c, s_ctx = hgrn_chunk_scan(qa_c, ka_c, va_c, lf_c, s0)
    qa, ka, va, lf, ga = hgrn_streams(pa, lb)
    oa, _ = hgrn_chunk_scan(qa, ka, va, lf, s_ctx)
    y_a = hgrn_readout(oa, ga, a_norm_w).astype(u.dtype)

    y_b = hyena_mixer(pb, b_conv_w, b_conv_b, *filt, b_bias).astype(u.dtype)

    cos, sin = axial_rope_tables(S)
    q = apply_axial_rope(pq.reshape(bsz, S, C_HEADS, C_HEAD_DIM), cos, sin)
    k = apply_axial_rope(pk.reshape(bsz, S, C_KV_HEADS, C_HEAD_DIM), cos, sin)
    v = pv.reshape(bsz, S, C_KV_HEADS, C_HEAD_DIM)
    k_ctx = ck.reshape(bsz, lc, C_KV_HEADS, C_HEAD_DIM)
    v_ctx = cv.reshape(bsz, lc, C_KV_HEADS, C_HEAD_DIM)
    y_c = windowed_attention(q, k, v, k_ctx, v_ctx, sink).astype(u.dtype)

    y = merge_branches((y_a, y_b, y_c), pg, w_branch, w_out)
    if not with_ctx_out:
        return y, None
    yc_a = hgrn_readout(oa_c, ga_c, a_norm_w).astype(uc.dtype)
    yc_b = hyena_mixer(cb, b_conv_w, b_conv_b, *filt, b_bias).astype(uc.dtype)
    yc_c = context_attention(cq.reshape(bsz, lc, C_HEADS, C_HEAD_DIM), k_ctx, v_ctx, sink).astype(uc.dtype)
    yc = merge_branches((yc_a, yc_b, yc_c), cg, w_branch, w_out)
    return y, yc


def setup_inputs(seed: int = 0) -> dict:
    key = jax.random.key(seed)
    ks = jax.random.split(key, 26)

    def nrm(k, shape, scale):
        return jax.random.normal(k, shape, F32) * scale

    gate_offset = jnp.repeat(jnp.array([0.0, 0.0, 1.0] * 3, F32), D_MODEL)
    return {
        'x': nrm(ks[0], (BATCH, SEQ, D_MODEL), 1.0),
        'c': nrm(ks[1], (BATCH, D_MODEL), 1.0),
        'ctx': nrm(ks[2], (BATCH, CTX_LEN, D_MODEL), 1.0),
        'c_ctx': nrm(ks[3], (D_MODEL,), 1.0),
        'ada_w': nrm(ks[4], (DEPTH, D_MODEL, N_MOD * D_MODEL), 0.5 * D_MODEL ** -0.5),
        'ada_b': nrm(ks[5], (DEPTH, N_MOD * D_MODEL), 0.02) + gate_offset,
        'ln_g': 1.0 + nrm(ks[6], (DEPTH, 3, D_MODEL), 0.02),
        'ln_b': nrm(ks[7], (DEPTH, 3, D_MODEL), 0.02),
        'ffn_w_in': nrm(ks[8], (DEPTH, 2, D_MODEL, 2 * D_FF), D_MODEL ** -0.5),
        'ffn_w_out': nrm(ks[9], (DEPTH, 2, D_FF, D_MODEL), DN_BETA * D_FF ** -0.5),
        'mix_w_in': nrm(ks[10], (DEPTH, D_MODEL, IN_WIDTH), D_MODEL ** -0.5),
        'hgrn_lb': nrm(ks[11], (DEPTH, 2, A_WIDTH), 0.5),
        'hgrn_norm_w': 1.0 + nrm(ks[12], (DEPTH, A_WIDTH), 0.02),
        'hyena_conv_w': nrm(ks[13], (DEPTH, 3, (B_ORDER + 1) * B_WIDTH), 3 ** -0.5),
        'hyena_conv_b': nrm(ks[14], (DEPTH, (B_ORDER + 1) * B_WIDTH), 0.02),
        'hyena_w1': nrm(ks[15], (DEPTH, B_EMB, B_FFN), B_EMB ** -0.5),
        'hyena_b1': nrm(ks[16], (DEPTH, B_FFN), 0.1),
        'hyena_f1': 1.0 + nrm(ks[17], (DEPTH, B_FFN), 0.1),
        'hyena_w2': nrm(ks[18], (DEPTH, B_FFN, B_FFN), B_FFN ** -0.5),
        'hyena_b2': nrm(ks[19], (DEPTH, B_FFN), 0.1),
        'hyena_f2': 1.0 + nrm(ks[20], (DEPTH, B_FFN), 0.1),
        'hyena_w3': nrm(ks[21], (DEPTH, B_FFN, 2 * B_ORDER * B_WIDTH), B_FFN ** -0.5),
        'hyena_bias': nrm(ks[22], (DEPTH, B_ORDER, B_WIDTH), 0.5),
        'attn_sink': nrm(ks[23], (DEPTH, C_HEADS), 0.5),
        'branch_w': nrm(ks[24], (DEPTH, N_BRANCH, BR_WIDTH, D_MODEL), BR_WIDTH ** -0.5),
        'out_w': nrm(ks[25], (DEPTH, D_MODEL, D_MODEL), DN_BETA * D_MODEL ** -0.5),
    }


def reference(x, c, ctx, c_ctx, ada_w, ada_b, ln_g, ln_b, ffn_w_in, ffn_w_out, mix_w_in, hgrn_lb, hgrn_norm_w,
              hyena_conv_w, hyena_conv_b, hyena_w1, hyena_b1, hyena_f1, hyena_w2, hyena_b2, hyena_f2, hyena_w3,
              hyena_bias, attn_sink, branch_w, out_w):
    s = jax.nn.softmax(hgrn_lb.astype(F32), axis=0)
    lower_bounds = jnp.cumsum(s, axis=0) - s[0:1]
    h, hc = x, ctx
    for l in range(DEPTH):
        last = l == DEPTH - 1
        mod = jnp.split((jax.nn.silu(c) @ ada_w[l] + ada_b[l])[:, None, :], N_MOD, axis=-1)
        modc = jnp.split(jax.nn.silu(c_ctx) @ ada_w[l] + ada_b[l], N_MOD, axis=-1)
        h = macaron_ffn(h, mod[0], mod[1], mod[2], ffn_w_in[l, 0], ffn_w_out[l, 0], ln_g[l, 0], ln_b[l, 0])
        hc = macaron_ffn(hc, modc[0], modc[1], modc[2], ffn_w_in[l, 0], ffn_w_out[l, 0], ln_g[l, 0], ln_b[l, 0])
        y, yc = token_mixer(modulate(h, mod[3], mod[4]), modulate(hc, modc[3], modc[4]), mix_w_in[l],
                            lower_bounds[l], hgrn_norm_w[l], hyena_conv_w[l], hyena_conv_b[l], hyena_w1[l],
                            hyena_b1[l], hyena_f1[l], hyena_w2[l], hyena_b2[l], hyena_f2[l], hyena_w3[l],
                            hyena_bias[l], attn_sink[l], branch_w[l], out_w[l], not last)
        h = layer_norm(DN_ALPHA * h + mod[5] * y, ln_g[l, 1], ln_b[l, 1])
        h = macaron_ffn(h, mod[6], mod[7], mod[8], ffn_w_in[l, 1], ffn_w_out[l, 1], ln_g[l, 2], ln_b[l, 2])
        if not last:
            hc = layer_norm(DN_ALPHA * hc + modc[5] * yc, ln_g[l, 1], ln_b[l, 1])
            hc = macaron_ffn(hc, modc[6], modc[7], modc[8], ffn_w_in[l, 1], ffn_w_out[l, 1],
                             ln_g[l, 2], ln_b[l, 2])
    return h
```

```python
import functools
import math

import jax
import jax.numpy as jnp
from jax import lax
from jax.experimental import pallas as pl
from jax.experimental.pallas import tpu as pltpu

F32 = jnp.float32
BF16 = jnp.bfloat16
HI = lax.Precision.HIGHEST

LANES = 128
SUBLANES = 8
VMEM_LIMIT = 56 * 1024 * 1024

N_MOD = 9
A_HEADS = 4
A_DK = 128
B_ORDER = 2
B_EMB = 33
B_BANDS = 16
B_FFN = 64
B_FAST_DECAY = 0.3
B_SLOW_DECAY = 1.5
B_TARGET = 1e-2
C_HEAD_DIM = 64
C_HEADS = 8
C_KV_HEADS = 2
C_GROUP = 4
C_BLOCK = 128
GRID_W = 64
ROPE_BASE = 10000.0
LN_EPS = 1e-5
RMS_EPS = 1e-6
DFT_N2 = 256
NEG_BIG = -1e30


def _cparams(sem, vmem=VMEM_LIMIT):
    return pltpu.CompilerParams(dimension_semantics=sem, vmem_limit_bytes=vmem)


def _const_spec(shape):
    nd = len(shape)
    return pl.BlockSpec(shape, lambda *_: (0,) * nd, pipeline_mode=pl.Buffered(1))


def _bdot(a, b):
    return jnp.dot(a.astype(BF16), b.astype(BF16), preferred_element_type=F32)


def _bdot_nt(a, b):
    return lax.dot_general(a.astype(BF16), b.astype(BF16), (((1,), (1,)), ((), ())),
                           preferred_element_type=F32)


def _hdot(a, b):
    return jnp.dot(a, b, precision=HI, preferred_element_type=F32)


def _layer_norm(x, g, b):
    mu = jnp.mean(x, axis=-1, keepdims=True)
    xc = x - mu
    var = jnp.mean(xc * xc, axis=-1, keepdims=True)
    return xc * lax.rsqrt(var + LN_EPS) * g + b


def _silu(x):
    return x * jax.nn.sigmoid(x)


def _ada_body(c_ref, w_ref, b_ref, o_ref):
    o_ref[0] = _hdot(_silu(c_ref[...]), w_ref[0]) + b_ref[0]


def _ada_mod(c8, ada_w, ada_b):
    depth, d, nw = ada_w.shape
    tn = nw // 8
    return pl.pallas_call(
        _ada_body,
        out_shape=jax.ShapeDtypeStruct((depth, SUBLANES, nw), F32),
        grid=(depth, nw // tn),
        in_specs=[pl.BlockSpec((SUBLANES, d), lambda l, j: (0, 0)),
                  pl.BlockSpec((1, d, tn), lambda l, j: (l, 0, j)),
                  pl.BlockSpec((1, 1, tn), lambda l, j: (l, 0, j))],
        out_specs=pl.BlockSpec((1, SUBLANES, tn), lambda l, j: (l, 0, j)),
        compiler_params=_cparams(("parallel", "parallel")),
    )(c8, ada_w, ada_b.reshape(depth, 1, nw))


def _ffn_body(h_ref, mod_ref, win_ref, wout_ref, g_ref, b_ref, o_ref, *, m0, d_ff, fc, alpha):
    h = h_ref[...]
    u = (h * (1.0 + mod_ref[0, m0 + 1:m0 + 2, :]) + mod_ref[0, m0:m0 + 1, :]).astype(BF16)
    acc = jnp.zeros_like(h)
    for j in range(d_ff // fc):
        a = jnp.dot(u, win_ref[:, j * fc:(j + 1) * fc], preferred_element_type=F32)
        b = jnp.dot(u, win_ref[:, d_ff + j * fc:d_ff + (j + 1) * fc], preferred_element_type=F32)
        act = (_silu(a) * b).astype(BF16)
        acc = acc + jnp.dot(act, wout_ref[j * fc:(j + 1) * fc, :], preferred_element_type=F32)
    r = alpha * h + 0.5 * mod_ref[0, m0 + 2:m0 + 3, :] * acc
    o_ref[...] = _layer_norm(r, g_ref[...], b_ref[...])


def _ffn(h2, mod, m0, w_in, w_out, g, b, rows_per_mod, alpha, tm):
    r, d = h2.shape
    d_ff = w_out.shape[0]
    tpm = rows_per_mod // tm
    body = functools.partial(_ffn_body, m0=m0, d_ff=d_ff, fc=d_ff // 2, alpha=alpha)
    return pl.pallas_call(
        body,
        out_shape=jax.ShapeDtypeStruct((r, d), F32),
        grid=(r // tm,),
        in_specs=[pl.BlockSpec((tm, d), lambda i: (i, 0)),
                  pl.BlockSpec((1, N_MOD, d), lambda i: (i // tpm, 0, 0)),
                  _const_spec(w_in.shape), _const_spec(w_out.shape),
                  _const_spec((1, d)), _const_spec((1, d))],
        out_specs=pl.BlockSpec((tm, d), lambda i: (i, 0)),
        compiler_params=_cparams(("parallel",)),
    )(h2, mod, w_in, w_out, g.reshape(1, d), b.reshape(1, d))


def _swap16(x):
    lane = lax.broadcasted_iota(jnp.int32, x.shape, 1)
    return jnp.where((lane % 32) < 16, pltpu.roll(x, LANES - 16, 1), pltpu.roll(x, 16, 1))


def _inproj_body(h_ref, mod_ref, w_ref, cos_ref, sin_ref, pa_ref, pb_ref, pq_ref, pk_ref, pv_ref,
                 *, wa, wb, wq, wk, rope):
    h = h_ref[...]
    u = (h * (1.0 + mod_ref[0, 4:5, :]) + mod_ref[0, 3:4, :]).astype(BF16)
    pa_ref[...] = jnp.dot(u, w_ref[:, 0:wa], preferred_element_type=F32)
    pb_ref[...] = jnp.dot(u, w_ref[:, wa:wa + wb], preferred_element_type=F32)
    oq = wa + wb
    q = jnp.dot(u, w_ref[:, oq:oq + wq], preferred_element_type=F32)
    k = jnp.dot(u, w_ref[:, oq + wq:oq + wq + wk], preferred_element_type=F32)
    pv_ref[...] = jnp.dot(u, w_ref[:, oq + wq + wk:oq + wq + 2 * wk], preferred_element_type=F32)
    if rope:
        cs = cos_ref[...]
        sn = sin_ref[...]
        for j in range(wq // LANES):
            xq = q[:, j * LANES:(j + 1) * LANES]
            pq_ref[:, j * LANES:(j + 1) * LANES] = xq * cs + _swap16(xq) * sn
        for j in range(wk // LANES):
            xk = k[:, j * LANES:(j + 1) * LANES]
            pk_ref[:, j * LANES:(j + 1) * LANES] = xk * cs + _swap16(xk) * sn
    else:
        pq_ref[...] = q
        pk_ref[...] = k


def _inproj(h2, mod, w, cos_t, sin_t, rows_per_mod, rope, widths, tm):
    r, d = h2.shape
    wa, wb, wq, wk = widths
    tpm = rows_per_mod // tm
    body = functools.partial(_inproj_body, wa=wa, wb=wb, wq=wq, wk=wk, rope=rope)
    outs = [jax.ShapeDtypeStruct((r, n), F32) for n in (wa, wb, wq, wk, wk)]
    return pl.pallas_call(
        body,
        out_shape=outs,
        grid=(r // tm,),
        in_specs=[pl.BlockSpec((tm, d), lambda i: (i, 0)),
                  pl.BlockSpec((1, N_MOD, d), lambda i: (i // tpm, 0, 0)),
                  _const_spec(w.shape),
                  pl.BlockSpec((tm, LANES), lambda i: (i % tpm, 0)),
                  pl.BlockSpec((tm, LANES), lambda i: (i % tpm, 0))],
        out_specs=[pl.BlockSpec((tm, n), lambda i: (i, 0)) for n in (wa, wb, wq, wk, wk)],
        compiler_params=_cparams(("parallel",)),
    )(h2, mod, w, cos_t, sin_t)


def _pivot_rows(b, c, r0):
    t, w = b.shape
    g = 2 * c
    if g >= SUBLANES:
        p = b.reshape(t // g, g, w)[:, r0:r0 + 1, :]
        return jnp.broadcast_to(p, (t // g, g, w)).reshape(t, w)
    row = lax.broadcasted_iota(jnp.int32, (t, 1), 0) % g
    out = b
    for m in range(g):
        if m != r0:
            out = jnp.where(row == m, pltpu.roll(b, (m - r0) % t, 0), out)
    return out


def _hgrn_chunk(q, kk, v, b, st, reverse):
    t = q.shape[0]
    row = lax.broadcasted_iota(jnp.int32, (t, 1), 0)
    ri = lax.broadcasted_iota(jnp.int32, (t, t), 0)
    ci = lax.broadcasted_iota(jnp.int32, (t, t), 1)
    vb = v.astype(BF16)
    out = _bdot_nt(q * jnp.exp(b), st)
    diag = jnp.sum(q * kk, axis=-1, keepdims=True)
    att = jnp.where(ri == ci, diag, 0.0)
    same = ri ^ ci
    c = t // 2
    while c >= 1:
        piv = _pivot_rows(b, c, c if reverse else c - 1)
        later = ((row // c) % 2) == 1
        qmask = jnp.logical_not(later) if reverse else later
        kmask = later if reverse else jnp.logical_not(later)
        qs = jnp.where(qmask, q * jnp.exp(jnp.minimum(b - piv, 0.0)), 0.0)
        ks = jnp.where(kmask, kk * jnp.exp(jnp.minimum(piv - b, 0.0)), 0.0)
        att = att + jnp.where(same < 2 * c, _bdot_nt(qs, ks), 0.0)
        c //= 2
    out = out + jnp.dot(att.astype(BF16), vb, preferred_element_type=F32)
    blast = b[0:1, :] if reverse else b[t - 1:t, :]
    ke = (kk * jnp.exp(blast - b)).astype(BF16)
    upd = jnp.dot(vb.T, ke, preferred_element_type=F32)
    return out, st * jnp.exp(blast) + upd


def _hgrn_body(qf_ref, vf_ref, zf_ref, qb_ref, vb_ref, zb_ref, lb_ref, s0_ref, tril_ref, triu_ref,
               of_ref, ob_ref, sfin_ref, st_ref):
    i = pl.program_id(2)

    @pl.when(i == 0)
    def _():
        st_ref[...] = s0_ref[0, 0]

    streams = ((qf_ref, vf_ref, zf_ref, of_ref, tril_ref), (qb_ref, vb_ref, zb_ref, ob_ref, triu_ref))
    for d, (q_r, v_r, z_r, o_r, tri_r) in enumerate(streams):
        z = z_r[0]
        lb = lb_ref[d:d + 1, :]
        logf = jnp.log(lb + (1.0 - lb) * jax.nn.sigmoid(z))
        kk = (1.0 - lb) * jax.nn.sigmoid(-z)
        b = _hdot(tri_r[...], logf)
        out, st_new = _hgrn_chunk(q_r[0], kk, v_r[0], b, st_ref[d], reverse=(d == 1))
        o_r[0] = out
        st_ref[d] = st_new

    @pl.when(i == pl.num_programs(2) - 1)
    def _():
        sfin_ref[0, 0] = st_ref[...]


def _hgrn(pa3, lb, s0, t):
    bsz, length, _ = pa3.shape
    w = lb.shape[1]
    nh = w // A_DK
    n = length // t
    ii = jnp.arange(t)
    tril = (ii[:, None] >= ii[None, :]).astype(F32)
    triu = (ii[:, None] <= ii[None, :]).astype(F32)

    def col(off, rev):
        if rev:
            return pl.BlockSpec((1, t, A_DK), lambda b, h, i: (b, n - 1 - i, off + h))
        return pl.BlockSpec((1, t, A_DK), lambda b, h, i: (b, i, off + h))

    st_spec = pl.BlockSpec((1, 1, 2, A_DK, A_DK), lambda b, h, i: (b, h, 0, 0, 0))
    return pl.pallas_call(
        _hgrn_body,
        out_shape=[jax.ShapeDtypeStruct((bsz, length, w), F32),
                   jax.ShapeDtypeStruct((bsz, length, w), F32),
                   jax.ShapeDtypeStruct((bsz, nh, 2, A_DK, A_DK), F32)],
        grid=(bsz, nh, n),
        in_specs=[col(0, False), col(nh, False), col(3 * nh, False),
                  col(0, True), col(nh, True), col(4 * nh, True),
                  pl.BlockSpec((2, A_DK), lambda b, h, i: (0, h)),
                  st_spec,
                  pl.BlockSpec((t, t), lambda b, h, i: (0, 0)),
                  pl.BlockSpec((t, t), lambda b, h, i: (0, 0))],
        out_specs=[col(0, False), col(0, True), st_spec],
        scratch_shapes=[pltpu.VMEM((2, A_DK, A_DK), F32)],
        compiler_params=_cparams(("parallel", "parallel", "arbitrary")),
    )(pa3, pa3, pa3, pa3, pa3, pa3, lb, s0, tril, triu)


def _shortconv_body(x_ref, prev_ref, next_ref, w_ref, b_ref, o0_ref, o1_ref, o2_ref, *, tl, cw):
    i = pl.program_id(1)
    x = x_ref[0]
    row = lax.broadcasted_iota(jnp.int32, (tl, 1), 0)
    prev_row = jnp.where(i > 0, prev_ref[0, SUBLANES - 1:SUBLANES, :], 0.0)
    next_row = jnp.where(i < pl.num_programs(1) - 1, next_ref[0, 0:1, :], 0.0)
    xm = jnp.where(row == 0, prev_row, pltpu.roll(x, 1, 0))
    xp = jnp.where(row == tl - 1, next_row, pltpu.roll(x, tl - 1, 0))
    y = xm * w_ref[0:1, :] + x * w_ref[1:2, :] + xp * w_ref[2:3, :] + b_ref[...]
    o0_ref[0] = y[:, 0:cw]
    o1_ref[0] = y[:, cw:2 * cw]
    o2_ref[0] = y[:, 2 * cw:3 * cw]


def _shortconv(pb3, w, b, tl):
    bsz, length, wd = pb3.shape
    cw = wd // 3
    n8 = length // SUBLANES
    r8 = tl // SUBLANES
    body = functools.partial(_shortconv_body, tl=tl, cw=cw)
    out = jax.ShapeDtypeStruct((bsz, length, cw), F32)
    return pl.pallas_call(
        body,
        out_shape=[out, out, out],
        grid=(bsz, length // tl),
        in_specs=[pl.BlockSpec((1, tl, wd), lambda bb, i: (bb, i, 0)),
                  pl.BlockSpec((1, SUBLANES, wd), lambda bb, i: (bb, jnp.maximum(i * r8 - 1, 0), 0)),
                  pl.BlockSpec((1, SUBLANES, wd), lambda bb, i: (bb, jnp.minimum((i + 1) * r8, n8 - 1), 0)),
                  pl.BlockSpec((3, wd), lambda bb, i: (0, 0)),
                  pl.BlockSpec((1, wd), lambda bb, i: (0, 0))],
        out_specs=[pl.BlockSpec((1, tl, cw), lambda bb, i: (bb, i, 0))] * 3,
        compiler_params=_cparams(("parallel", "parallel")),
    )(pb3, pb3, pb3, w, b.reshape(1, wd))


def _hyena_feats(length):
    n = jnp.arange(2 * length)
    pos = jnp.where(n < length, n, 2 * length - n).astype(F32)
    t = pos / (length - 1)
    w = 2.0 * math.pi * pos / length
    bands = jnp.linspace(1e-4, B_BANDS - 1, B_BANDS, dtype=F32)
    feats = jnp.concatenate([t[:, None], jnp.cos(w[:, None] * bands), -jnp.sin(w[:, None] * bands)], axis=-1)
    return jnp.pad(feats, ((0, 0), (0, B_FFN - B_EMB)))


def _decay_rates(width):
    max_decay = math.log(B_TARGET) / B_FAST_DECAY
    min_decay = math.log(B_TARGET) / B_SLOW_DECAY
    return jnp.abs(jnp.linspace(min_decay, max_decay, width, dtype=F32)).reshape(1, width)


def _filter_body(f_ref, w1_ref, b1_ref, f1_ref, w2_ref, b2_ref, f2_ref, w3_ref, rate_ref, o_ref, l1_ref,
                 *, tr, length):
    i = pl.program_id(0)
    f = f_ref[...]
    h = jnp.sin(f1_ref[...] * (_hdot(f, w1_ref[...]) + b1_ref[...]))
    h = jnp.sin(f2_ref[...] * (_hdot(h, w2_ref[...]) + b2_ref[...]))
    h = _hdot(h, w3_ref[...])
    dec = jnp.exp(-(f[:, 0:1] * rate_ref[...]))
    h = h * jnp.concatenate([dec] * B_ORDER, axis=1)
    row = i * tr + lax.broadcasted_iota(jnp.int32, (tr, 1), 0)
    h = jnp.where(row == length, 0.0, h)
    o_ref[...] = h

    @pl.when(i == 0)
    def _():
        l1_ref[...] = jnp.zeros_like(l1_ref)

    l1_ref[...] += jnp.sum(jnp.abs(h), axis=0, keepdims=True)


def _hyena_filter(feats, w1, b1, f1, w2, b2, f2, w3, rates, length):
    width = rates.shape[1]
    ow = B_ORDER * width
    tr = min(512, length)
    nblk = 2 * length // tr
    w1p = jnp.pad(w1, ((0, B_FFN - B_EMB), (0, 0)))
    body = functools.partial(_filter_body, tr=tr, length=length)
    small = lambda shape: pl.BlockSpec(shape, lambda i: (0, 0))
    return pl.pallas_call(
        body,
        out_shape=[jax.ShapeDtypeStruct((2 * length, ow), F32), jax.ShapeDtypeStruct((1, ow), F32)],
        grid=(nblk,),
        in_specs=[pl.BlockSpec((tr, B_FFN), lambda i: (i, 0)),
                  small((B_FFN, B_FFN)), small((1, B_FFN)), small((1, B_FFN)),
                  small((B_FFN, B_FFN)), small((1, B_FFN)), small((1, B_FFN)),
                  pl.BlockSpec((B_FFN, ow), lambda i: (0, (2 * i) // nblk)),
                  small((1, width))],
        out_specs=[pl.BlockSpec((tr, ow), lambda i: (i, 0)), pl.BlockSpec((1, ow), lambda i: (0, 0))],
        compiler_params=_cparams(("arbitrary",)),
    )(feats, w1p, b1.reshape(1, -1), f1.reshape(1, -1), w2, b2.reshape(1, -1), f2.reshape(1, -1), w3, rates)


def _dft_tables(length):
    n = 2 * length
    n2 = DFT_N2
    n1 = n // n2
    n1h = n1 // 2
    k1 = jnp.arange(n1, dtype=jnp.int32)[None, :, None]
    j1 = jnp.arange(n1, dtype=jnp.int32)[None, None, :]
    j2 = jnp.arange(n2, dtype=jnp.int32)[:, None, None]
    ang = ((k1 * (n2 * j1 + j2)) % n).astype(F32) * (-2.0 * math.pi / n)
    er, ei = jnp.cos(ang), jnp.sin(ang)
    a_f = jnp.concatenate([er, ei], axis=1)
    erh, eih = er[:, :, :n1h], ei[:, :, :n1h]
    a_z = jnp.concatenate([jnp.concatenate([erh, -eih], axis=2),
                           jnp.concatenate([eih, erh], axis=2)], axis=1)
    a_inv = jnp.swapaxes(a_z, 1, 2) / n
    kk = jnp.arange(n2, dtype=jnp.int32)
    ang2 = ((kk[:, None] * kk[None, :]) % n2).astype(F32) * (-2.0 * math.pi / n2)
    fr, fi = jnp.cos(ang2), jnp.sin(ang2)
    f2 = jnp.concatenate([jnp.concatenate([fr, -fi], axis=1), jnp.concatenate([fi, fr], axis=1)], axis=0)
    return a_f.astype(BF16), a_z.astype(BF16), a_inv.astype(BF16), f2.astype(BF16), f2.T.astype(BF16)


def _dfta_fwd_body(x_ref, a_ref, s_ref, o_ref, *, nb, cw):
    for j in range(nb):
        x = x_ref[:, j * cw:(j + 1) * cw] * s_ref[...]
        o_ref[:, j * cw:(j + 1) * cw] = jnp.dot(a_ref[j], x.astype(BF16), preferred_element_type=F32)


def _dfta_fwd(x2, a, scale, cw, nb):
    rows = x2.shape[0]
    n2, m, _ = a.shape
    body = functools.partial(_dfta_fwd_body, nb=nb, cw=cw)
    return pl.pallas_call(
        body,
        out_shape=jax.ShapeDtypeStruct((m, n2 * cw), F32),
        grid=(n2 // nb,),
        in_specs=[pl.BlockSpec((rows, nb * cw), lambda i: (0, i)),
                  pl.BlockSpec((nb, m, rows), lambda i: (i, 0, 0)),
                  pl.BlockSpec((1, cw), lambda i: (0, 0))],
        out_specs=pl.BlockSpec((m, nb * cw), lambda i: (0, i)),
        compiler_params=_cparams(("parallel",)),
    )(x2, a, scale)


def _dfta_inv_body(u_ref, a_ref, xn_ref, z_ref, bias_ref, o_ref, *, nb, cw):
    for j in range(nb):
        sl = slice(j * cw, (j + 1) * cw)
        y = jnp.dot(a_ref[j], u_ref[:, sl].astype(BF16), preferred_element_type=F32)
        o_ref[:, sl] = xn_ref[:, sl] * (y + bias_ref[...] * z_ref[:, sl])


def _dfta_inv(u2, a_inv, xn2, z2, bias, cw, nb):
    n2, m, k = a_inv.shape
    body = functools.partial(_dfta_inv_body, nb=nb, cw=cw)
    return pl.pallas_call(
        body,
        out_shape=jax.ShapeDtypeStruct((m, n2 * cw), F32),
        grid=(n2 // nb,),
        in_specs=[pl.BlockSpec((k, nb * cw), lambda i: (0, i)),
                  pl.BlockSpec((nb, m, k), lambda i: (i, 0, 0)),
                  pl.BlockSpec((m, nb * cw), lambda i: (0, i)),
                  pl.BlockSpec((m, nb * cw), lambda i: (0, i)),
                  pl.BlockSpec((1, cw), lambda i: (0, 0))],
        out_specs=pl.BlockSpec((m, nb * cw), lambda i: (0, i)),
        compiler_params=_cparams(("parallel",)),
    )(u2, a_inv, xn2, z2, bias)


def _dftc_filter_body(t_ref, f_ref, o_ref, *, kb):
    for j in range(kb):
        t = t_ref[:, j]
        t2 = t.reshape(2 * DFT_N2, t.shape[-1])
        y = jnp.dot(f_ref[...], t2.astype(BF16), preferred_element_type=F32)
        o_ref[j] = y.reshape(2, DFT_N2, t.shape[-1])


def _dftc_filter(t4, f2, kb):
    _, n1, n2, ow = t4.shape
    body = functools.partial(_dftc_filter_body, kb=kb)
    return pl.pallas_call(
        body,
        out_shape=jax.ShapeDtypeStruct((n1, 2, n2, ow), F32),
        grid=(n1 // kb,),
        in_specs=[pl.BlockSpec((2, kb, n2, ow), lambda i: (0, i, 0, 0)),
                  pl.BlockSpec((2 * n2, 2 * n2), lambda i: (0, 0))],
        out_specs=pl.BlockSpec((kb, 2, n2, ow), lambda i: (i, 0, 0, 0)),
        compiler_params=_cparams(("parallel",)),
    )(t4, f2)


def _dftc_mid_body(t_ref, h_ref, f_ref, fi_ref, o_ref, *, kb):
    cw = t_ref.shape[-1]
    for j in range(kb):
        t2 = t_ref[:, j].reshape(2 * DFT_N2, cw)
        y = jnp.dot(f_ref[...], t2.astype(BF16), preferred_element_type=F32)
        yr, yi = y[:DFT_N2], y[DFT_N2:]
        hr, hi = h_ref[j, 0], h_ref[j, 1]
        p = jnp.concatenate([yr * hr - yi * hi, yr * hi + yi * hr], axis=0)
        u = jnp.dot(fi_ref[...], p.astype(BF16), preferred_element_type=F32)
        o_ref[0, j] = u[:DFT_N2]
        o_ref[1, j] = u[DFT_N2:]


def _dftc_mid(t4, hspec, order, f2, f2inv, kb):
    _, n1, n2, cw = t4.shape
    body = functools.partial(_dftc_mid_body, kb=kb)
    return pl.pallas_call(
        body,
        out_shape=jax.ShapeDtypeStruct((2, n1, n2, cw), F32),
        grid=(n1 // kb,),
        in_specs=[pl.BlockSpec((2, kb, n2, cw), lambda i: (0, i, 0, 0)),
                  pl.BlockSpec((kb, 2, n2, cw), lambda i: (i, 0, 0, order)),
                  pl.BlockSpec((2 * n2, 2 * n2), lambda i: (0, 0)),
                  pl.BlockSpec((2 * n2, 2 * n2), lambda i: (0, 0))],
        out_specs=pl.BlockSpec((2, kb, n2, cw), lambda i: (0, i, 0, 0)),
        compiler_params=_cparams(("parallel",)),
    )(t4, hspec, f2, f2inv)


def _hyena_long(parts, filt, l1, bias, tables):
    a_f, a_z, a_inv, f2, f2inv = tables
    bsz, length, cw = parts[0].shape
    assert bsz == 2, "the two batch rows ride as real and imaginary parts of one transform"
    n = 2 * length
    n1 = n // DFT_N2
    ow = filt.shape[1]
    inv_l1 = 1.0 / l1
    tf = _dfta_fwd(filt.reshape(n1, DFT_N2 * ow), a_f, inv_l1, ow, 4)
    hspec = _dftc_filter(tf.reshape(2, n1, DFT_N2, ow), f2, 2)
    ones = jnp.ones((1, cw), F32)
    z = parts[0].reshape(bsz * n1 // 2, DFT_N2 * cw)
    for o in range(B_ORDER):
        t = _dfta_fwd(z, a_z, ones, cw, 8)
        u = _dftc_mid(t.reshape(2, n1, DFT_N2, cw), hspec, o, f2, f2inv, 4)
        z = _dfta_inv(u.reshape(2 * n1, DFT_N2 * cw), a_inv, parts[o + 1].reshape(z.shape), z,
                      bias[o:o + 1], cw, 8)
    return z.reshape(bsz, length, cw)


def _ctx_conv_body(v_ref, x1_ref, x2_ref, filt_ref, il1_ref, bias_ref, ff_ref, fz_ref, fi_ref, o_ref, *, n, cw):
    hf = _hdot(ff_ref[...], filt_ref[...] * il1_ref[...])
    z = jnp.concatenate([v_ref[0], v_ref[1]], axis=0)
    for o, x_ref in enumerate((x1_ref, x2_ref)):
        zz = _hdot(fz_ref[...], z)
        zr, zi = zz[:n], zz[n:]
        hr, hi = hf[:n, o * cw:(o + 1) * cw], hf[n:, o * cw:(o + 1) * cw]
        y = _hdot(fi_ref[...], jnp.concatenate([zr * hr - zi * hi, zr * hi + zi * hr], axis=0))
        xn = jnp.concatenate([x_ref[0], x_ref[1]], axis=0)
        z = xn * (y + bias_ref[o:o + 1, :] * z)
    half = n // 2
    o_ref[0] = z[:half]
    o_ref[1] = z[half:]


def _hyena_ctx(parts, filt, l1, bias):
    bsz, length, cw = parts[0].shape
    assert bsz == 2
    n = 2 * length
    k = jnp.arange(n, dtype=jnp.int32)
    ang = ((k[:, None] * k[None, :]) % n).astype(F32) * (-2.0 * math.pi / n)
    cr, ci = jnp.cos(ang), jnp.sin(ang)
    ff = jnp.concatenate([cr, ci], axis=0)
    crh, cih = cr[:, :length], ci[:, :length]
    fz = jnp.concatenate([jnp.concatenate([crh, -cih], axis=1), jnp.concatenate([cih, crh], axis=1)], axis=0)
    fi = fz.T / n
    body = functools.partial(_ctx_conv_body, n=n, cw=cw)
    full = lambda a: pl.BlockSpec(a.shape, lambda i: (0,) * a.ndim)
    args = (parts[0], parts[1], parts[2], filt, 1.0 / l1, bias, ff, fz, fi)
    return pl.pallas_call(
        body,
        out_shape=jax.ShapeDtypeStruct((bsz, length, cw), F32),
        grid=(1,),
        in_specs=[full(a) for a in args],
        out_specs=pl.BlockSpec((bsz, length, cw), lambda i: (0, 0, 0)),
        compiler_params=_cparams(("arbitrary",)),
    )(*args)


def _rope_tables(length):
    n_rows = length // GRID_W
    row = jnp.repeat(jnp.arange(n_rows), GRID_W).astype(F32)
    col = jnp.tile(jnp.arange(GRID_W), n_rows).astype(F32)
    nf = C_HEAD_DIM // 4
    inv = ROPE_BASE ** (-jnp.arange(nf, dtype=F32) * 2.0 / (C_HEAD_DIM // 2))
    ar, ac = row[:, None] * inv, col[:, None] * inv
    cos_h = jnp.concatenate([jnp.cos(ar), jnp.cos(ar), jnp.cos(ac), jnp.cos(ac)], axis=-1)
    sin_h = jnp.concatenate([-jnp.sin(ar), jnp.sin(ar), -jnp.sin(ac), jnp.sin(ac)], axis=-1)
    reps = LANES // C_HEAD_DIM
    return jnp.tile(cos_h, (1, reps)), jnp.tile(sin_h, (1, reps))


def _attn_body(sink_ref, q_ref, kp_ref, kc_ref, kn_ref, vp_ref, vc_ref, vn_ref, kx_ref, vx_ref, o_ref):
    n = pl.program_id(1)
    blk = q_ref.shape[1]
    ri = lax.broadcasted_iota(jnp.int32, (blk, blk), 0)
    ci = lax.broadcasted_iota(jnp.int32, (blk, blk), 1)
    mask_p = jnp.logical_and(ci >= ri, n > 0)
    mask_n = jnp.logical_and(ci <= ri, n < pl.num_programs(1) - 1)
    scale = C_HEAD_DIM ** -0.5
    for h in range(C_HEADS):
        ks = slice((h // C_GROUP) * C_HEAD_DIM, (h // C_GROUP + 1) * C_HEAD_DIM)
        qh = q_ref[0, :, h * C_HEAD_DIM:(h + 1) * C_HEAD_DIM] * scale
        sp = jnp.where(mask_p, _bdot_nt(qh, kp_ref[0, :, ks]), NEG_BIG)
        sc = _bdot_nt(qh, kc_ref[0, :, ks])
        sn = jnp.where(mask_n, _bdot_nt(qh, kn_ref[0, :, ks]), NEG_BIG)
        sx = _bdot_nt(qh, kx_ref[0, :, ks])
        sk = sink_ref[h]
        rmax = lambda s: jnp.max(s, axis=-1, keepdims=True)
        m = jnp.maximum(jnp.maximum(jnp.maximum(rmax(sp), rmax(sc)), jnp.maximum(rmax(sn), rmax(sx))), sk)
        pp, pc, pn, px = jnp.exp(sp - m), jnp.exp(sc - m), jnp.exp(sn - m), jnp.exp(sx - m)
        rsum = lambda p: jnp.sum(p, axis=-1, keepdims=True)
        den = rsum(pp) + rsum(pc) + rsum(pn) + rsum(px) + jnp.exp(sk - m)
        o = (_bdot(pp, vp_ref[0, :, ks]) + _bdot(pc, vc_ref[0, :, ks])
             + _bdot(pn, vn_ref[0, :, ks]) + _bdot(px, vx_ref[0, :, ks]))
        o_ref[0, :, h * C_HEAD_DIM:(h + 1) * C_HEAD_DIM] = o / den


def _attention(q3, k3, v3, kx3, vx3, sink):
    bsz, length, wq = q3.shape
    wk = k3.shape[2]
    lc = kx3.shape[1]
    nb = length // C_BLOCK
    cur = lambda w: pl.BlockSpec((1, C_BLOCK, w), lambda b, i: (b, i, 0))
    prv = lambda w: pl.BlockSpec((1, C_BLOCK, w), lambda b, i: (b, jnp.maximum(i - 1, 0), 0))
    nxt = lambda w: pl.BlockSpec((1, C_BLOCK, w), lambda b, i: (b, jnp.minimum(i + 1, nb - 1), 0))
    ctx = pl.BlockSpec((1, lc, wk), lambda b, i: (b, 0, 0))
    return pl.pallas_call(
        _attn_body,
        out_shape=jax.ShapeDtypeStruct((bsz, length, wq), F32),
        grid=(bsz, nb),
        in_specs=[pl.BlockSpec(memory_space=pltpu.SMEM), cur(wq), prv(wk), cur(wk), nxt(wk),
                  prv(wk), cur(wk), nxt(wk), ctx, ctx],
        out_specs=cur(wq),
        compiler_params=_cparams(("parallel", "parallel")),
    )(sink, q3, k3, k3, k3, v3, v3, v3, kx3, vx3)


def _ctx_attn_body(sink_ref, q_ref, k_ref, v_ref, o_ref):
    scale = C_HEAD_DIM ** -0.5
    for h in range(C_HEADS):
        ks = slice((h // C_GROUP) * C_HEAD_DIM, (h // C_GROUP + 1) * C_HEAD_DIM)
        qh = q_ref[0, :, h * C_HEAD_DIM:(h + 1) * C_HEAD_DIM] * scale
        s = _bdot_nt(qh, k_ref[0, :, ks])
        sk = sink_ref[h]
        m = jnp.maximum(jnp.max(s, axis=-1, keepdims=True), sk)
        p = jnp.exp(s - m)
        den = jnp.sum(p, axis=-1, keepdims=True) + jnp.exp(sk - m)
        o_ref[0, :, h * C_HEAD_DIM:(h + 1) * C_HEAD_DIM] = _bdot(p, v_ref[0, :, ks]) / den


def _ctx_attention(q3, k3, v3, sink):
    bsz, lc, wq = q3.shape
    wk = k3.shape[2]
    spec = lambda w: pl.BlockSpec((1, lc, w), lambda b: (b, 0, 0))
    return pl.pallas_call(
        _ctx_attn_body,
        out_shape=jax.ShapeDtypeStruct((bsz, lc, wq), F32),
        grid=(bsz,),
        in_specs=[pl.BlockSpec(memory_space=pltpu.SMEM), spec(wq), spec(wk), spec(wk)],
        out_specs=spec(wq),
        compiler_params=_cparams(("parallel",)),
    )(sink, q3, k3, v3)


def _merge_body(h_ref, mod_ref, of_ref, ob_ref, ga_ref, yb_ref, yc_ref, wg_ref, wbr_ref, wo_ref, nw_ref,
                g_ref, b_ref, o_ref, *, alpha):
    h = h_ref[...]
    d = h.shape[1]
    u = (h * (1.0 + mod_ref[0, 4:5, :]) + mod_ref[0, 3:4, :]).astype(BF16)
    o = of_ref[...] + ob_ref[...]
    heads = []
    for hh in range(o.shape[1] // A_DK):
        oh = o[:, hh * A_DK:(hh + 1) * A_DK]
        ms = jnp.mean(oh * oh, axis=-1, keepdims=True)
        heads.append(oh * lax.rsqrt(ms + RMS_EPS) * nw_ref[:, hh * A_DK:(hh + 1) * A_DK])
    ya = jnp.concatenate(heads, axis=1) * _silu(ga_ref[...])
    m = jnp.zeros_like(h)
    for n, y in enumerate((ya, yb_ref[...], yc_ref[...])):
        gate = jnp.dot(u, wg_ref[:, n * d:(n + 1) * d], preferred_element_type=F32)
        m = m + jax.nn.sigmoid(gate) * jnp.dot(y.astype(BF16), wbr_ref[n], preferred_element_type=F32)
    y = jnp.dot(m.astype(BF16), wo_ref[...], preferred_element_type=F32)
    r = alpha * h + mod_ref[0, 5:6, :] * y
    o_ref[...] = _layer_norm(r, g_ref[...], b_ref[...])


def _merge(h2, mod, of2, ob2, pa2, yb2, yc2, wg, wbr, wo, norm_w, g, b, rows_per_mod, alpha, tm):
    r, d = h2.shape
    bw = of2.shape[1]
    tpm = rows_per_mod // tm
    body = functools.partial(_merge_body, alpha=alpha)
    row = lambda w: pl.BlockSpec((tm, w), lambda i: (i, 0))
    return pl.pallas_call(
        body,
        out_shape=jax.ShapeDtypeStruct((r, d), F32),
        grid=(r // tm,),
        in_specs=[row(d), pl.BlockSpec((1, N_MOD, d), lambda i: (i // tpm, 0, 0)),
                  row(bw), row(bw), pl.BlockSpec((tm, bw), lambda i: (i, 2)), row(bw), row(bw),
                  _const_spec(wg.shape), _const_spec(wbr.shape), _const_spec(wo.shape),
                  _const_spec((1, bw)), _const_spec((1, d)), _const_spec((1, d))],
        out_specs=row(d),
        compiler_params=_cparams(("parallel",)),
    )(h2, mod, of2, ob2, pa2, yb2, yc2, wg, wbr, wo, norm_w.reshape(1, bw), g.reshape(1, d), b.reshape(1, d))


def kernel(x, c, ctx, c_ctx, ada_w, ada_b, ln_g, ln_b, ffn_w_in, ffn_w_out, mix_w_in, hgrn_lb, hgrn_norm_w,
           hyena_conv_w, hyena_conv_b, hyena_w1, hyena_b1, hyena_f1, hyena_w2, hyena_b2, hyena_f2, hyena_w3,
           hyena_bias, attn_sink, branch_w, out_w):
    bsz, seq, d = x.shape
    lc = ctx.shape[1]
    depth = ada_w.shape[0]
    alpha = (2.0 * depth) ** 0.25
    bw = hgrn_lb.shape[2]
    wk = C_KV_HEADS * C_HEAD_DIM
    widths = (5 * bw, (B_ORDER + 1) * bw, bw, wk)
    off_g = widths[0] + widths[1] + widths[2] + 2 * wk
    tm = 512
    tmc = min(256, bsz * lc)

    s = jax.nn.softmax(hgrn_lb.astype(F32), axis=0)
    lower_bounds = jnp.cumsum(s, axis=0) - s[0:1]

    c8 = jnp.zeros((SUBLANES, d), F32).at[:bsz].set(c).at[bsz].set(c_ctx)
    mods = _ada_mod(c8, ada_w, ada_b).reshape(depth, SUBLANES, N_MOD, d)

    cos_t, sin_t = _rope_tables(seq)
    feats = _hyena_feats(seq)
    feats_c = _hyena_feats(lc)
    rates = _decay_rates(bw)
    tables = _dft_tables(seq)

    w_in_bf = ffn_w_in.astype(BF16)
    w_out_bf = ffn_w_out.astype(BF16)
    mix_bf = mix_w_in.astype(BF16)
    br_bf = branch_w.astype(BF16)
    out_bf = out_w.astype(BF16)

    h = x.reshape(bsz * seq, d)
    hc = ctx.reshape(bsz * lc, d)
    for l in range(depth):
        last = l == depth - 1
        mod = mods[l, :bsz]
        modc = mods[l, bsz:bsz + 1]
        w_proj = mix_bf[l, :, :off_g]
        w_gate = mix_bf[l, :, off_g:]
        ffn = lambda t, mm, m0, j, g, rpm, tt: _ffn(t, mm, m0, w_in_bf[l, j], w_out_bf[l, j], ln_g[l, g],
                                                     ln_b[l, g], rpm, alpha, tt)
        h = ffn(h, mod, 0, 0, 0, seq, tm)
        hc = ffn(hc, modc, 0, 0, 0, bsz * lc, tmc)

        pa, pb, pq, pk, pv = _inproj(h, mod, w_proj, cos_t, sin_t, seq, True, widths, 256)
        ca, cb, cq, ck, cv = _inproj(hc, modc, w_proj, cos_t, sin_t, bsz * lc, False, widths, tmc)

        s0 = jnp.zeros((bsz, A_HEADS, 2, A_DK, A_DK), F32)
        ocf, ocb, s_ctx = _hgrn(ca.reshape(bsz, lc, -1), lower_bounds[l], s0, lc)
        of, ob, _ = _hgrn(pa.reshape(bsz, seq, -1), lower_bounds[l], s_ctx, 256)

        parts = _shortconv(pb.reshape(bsz, seq, -1), hyena_conv_w[l], hyena_conv_b[l], 512)
        fargs = (hyena_w1[l], hyena_b1[l], hyena_f1[l], hyena_w2[l], hyena_b2[l], hyena_f2[l], hyena_w3[l], rates)
        filt, l1 = _hyena_filter(feats, *fargs, seq)
        yb = _hyena_long(parts, filt, l1, hyena_bias[l], tables)

        kx, vx = ck.reshape(bsz, lc, wk), cv.reshape(bsz, lc, wk)
        yc = _attention(pq.reshape(bsz, seq, bw), pk.reshape(bsz, seq, wk), pv.reshape(bsz, seq, wk),
                        kx, vx, attn_sink[l])

        merge = lambda t, mm, a1, a2, a3, a4, a5, rpm, tt: _merge(
            t, mm, a1, a2, a3, a4, a5, w_gate, br_bf[l], out_bf[l], hgrn_norm_w[l], ln_g[l, 1], ln_b[l, 1],
            rpm, alpha, tt)
        h = merge(h, mod, of.reshape(-1, bw), ob.reshape(-1, bw), pa, yb.reshape(-1, bw), yc.reshape(-1, bw),
                  seq, 256)
        h = ffn(h, mod, 6, 1, 2, seq, tm)
        if not last:
            cparts = _shortconv(cb.reshape(bsz, lc, -1), hyena_conv_w[l], hyena_conv_b[l], lc)
            cfilt, cl1 = _hyena_filter(feats_c, *fargs, lc)
            ycb = _hyena_ctx(cparts, cfilt, cl1, hyena_bias[l])
            ycc = _ctx_attention(cq.reshape(bsz, lc, bw), kx, vx, attn_sink[l])
            hc = merge(hc, modc, ocf.reshape(-1, bw), ocb.reshape(-1, bw), ca, ycb.reshape(-1, bw),
                       ycc.reshape(-1, bw), bsz * lc, tmc)
            hc = ffn(hc, modc, 6, 1, 2, bsz * lc, tmc)
    return h.reshape(bsz, seq, d)
```

```python
import functools
import math

import jax
import jax.numpy as jnp
from jax import lax
from jax.experimental import pallas as pl
from jax.experimental.pallas import tpu as pltpu

F32 = jnp.float32
BF16 = jnp.bfloat16
HI = lax.Precision.HIGHEST

LANES = 128
SUBLANES = 8
VMEM_LIMIT = 56 * 1024 * 1024

N_MOD = 9
A_HEADS = 4
A_DK = 128
B_ORDER = 2
B_EMB = 33
B_BANDS = 16
B_FFN = 64
B_FAST_DECAY = 0.3
B_SLOW_DECAY = 1.5
B_TARGET = 1e-2
C_HEAD_DIM = 64
C_HEADS = 8
C_KV_HEADS = 2
C_GROUP = 4
C_BLOCK = 128
GRID_W = 64
ROPE_BASE = 10000.0
LN_EPS = 1e-5
RMS_EPS = 1e-6
DFT_N2 = 256
NEG_BIG = -1e30


def _cparams(sem, vmem=VMEM_LIMIT):
    return pltpu.CompilerParams(dimension_semantics=sem, vmem_limit_bytes=vmem)


def _const_spec(shape):
    nd = len(shape)
    return pl.BlockSpec(shape, lambda *_: (0,) * nd, pipeline_mode=pl.Buffered(1))


def _bdot(a, b):
    return jnp.dot(a.astype(BF16), b.astype(BF16), preferred_element_type=F32)


def _bdot_nt(a, b):
    return lax.dot_general(a.astype(BF16), b.astype(BF16), (((1,), (1,)), ((), ())),
                           preferred_element_type=F32)


def _hdot(a, b):
    return jnp.dot(a, b, precision=HI, preferred_element_type=F32)


def _layer_norm(x, g, b):
    mu = jnp.mean(x, axis=-1, keepdims=True)
    xc = x - mu
    var = jnp.mean(xc * xc, axis=-1, keepdims=True)
    return xc * lax.rsqrt(var + LN_EPS) * g + b


def _silu(x):
    return x * jax.nn.sigmoid(x)


def _ada_body(c_ref, w_ref, b_ref, o_ref):
    o_ref[0] = _hdot(_silu(c_ref[...]), w_ref[0]) + b_ref[0]


def _ada_mod(c8, ada_w, ada_b):
    depth, d, nw = ada_w.shape
    tn = nw // 8
    return pl.pallas_call(
        _ada_body,
        out_shape=jax.ShapeDtypeStruct((depth, SUBLANES, nw), F32),
        grid=(depth, nw // tn),
        in_specs=[pl.BlockSpec((SUBLANES, d), lambda l, j: (0, 0)),
                  pl.BlockSpec((1, d, tn), lambda l, j: (l, 0, j)),
                  pl.BlockSpec((1, 1, tn), lambda l, j: (l, 0, j))],
        out_specs=pl.BlockSpec((1, SUBLANES, tn), lambda l, j: (l, 0, j)),
        compiler_params=_cparams(("parallel", "parallel")),
        name="ada_mod",
    )(c8, ada_w, ada_b.reshape(depth, 1, nw))


def _ffn_body(h_ref, mod_ref, win_ref, wout_ref, g_ref, b_ref, o_ref, *, m0, d_ff, fc, alpha):
    h = h_ref[...]
    u = (h * (1.0 + mod_ref[0, m0 + 1:m0 + 2, :]) + mod_ref[0, m0:m0 + 1, :]).astype(BF16)
    acc = jnp.zeros_like(h)
    for j in range(d_ff // fc):
        a = jnp.dot(u, win_ref[:, j * fc:(j + 1) * fc], preferred_element_type=F32)
        b = jnp.dot(u, win_ref[:, d_ff + j * fc:d_ff + (j + 1) * fc], preferred_element_type=F32)
        act = (_silu(a) * b).astype(BF16)
        acc = acc + jnp.dot(act, wout_ref[j * fc:(j + 1) * fc, :], preferred_element_type=F32)
    r = alpha * h + 0.5 * mod_ref[0, m0 + 2:m0 + 3, :] * acc
    o_ref[...] = _layer_norm(r, g_ref[...], b_ref[...])


def _ffn(h2, mod, m0, w_in, w_out, g, b, rows_per_mod, alpha, tm):
    r, d = h2.shape
    d_ff = w_out.shape[0]
    tpm = rows_per_mod // tm
    body = functools.partial(_ffn_body, m0=m0, d_ff=d_ff, fc=d_ff // 2, alpha=alpha)
    return pl.pallas_call(
        body,
        out_shape=jax.ShapeDtypeStruct((r, d), F32),
        grid=(r // tm,),
        in_specs=[pl.BlockSpec((tm, d), lambda i: (i, 0)),
                  pl.BlockSpec((1, N_MOD, d), lambda i: (i // tpm, 0, 0)),
                  _const_spec(w_in.shape), _const_spec(w_out.shape),
                  _const_spec((1, d)), _const_spec((1, d))],
        out_specs=pl.BlockSpec((tm, d), lambda i: (i, 0)),
        compiler_params=_cparams(("parallel",)),
        name="ffn",
    )(h2, mod, w_in, w_out, g.reshape(1, d), b.reshape(1, d))


def _swap16(x):
    lane = lax.broadcasted_iota(jnp.int32, x.shape, 1)
    return jnp.where((lane % 32) < 16, pltpu.roll(x, LANES - 16, 1), pltpu.roll(x, 16, 1))


def _head_pair_variants(x):
    lane = lax.broadcasted_iota(jnp.int32, x.shape, 1)
    low = lane < C_HEAD_DIM
    xs = pltpu.roll(x, C_HEAD_DIM, 1)
    parts = (jnp.where(low, x, 0.0), jnp.where(low, 0.0, xs), jnp.where(low, xs, 0.0), jnp.where(low, 0.0, x))
    return jnp.concatenate(parts, axis=1).astype(BF16)


def _inproj_body(h_ref, mod_ref, w_ref, cos_ref, sin_ref, pa_ref, pb_ref, pq_ref, pk_ref, pv_ref,
                 *, wa, wb, wq, wk, rope):
    assert wk == LANES
    h = h_ref[...]
    u = (h * (1.0 + mod_ref[0, 4:5, :]) + mod_ref[0, 3:4, :]).astype(BF16)
    pa_ref[...] = jnp.dot(u, w_ref[:, 0:wa], preferred_element_type=F32)
    pb_ref[...] = jnp.dot(u, w_ref[:, wa:wa + wb], preferred_element_type=F32)
    oq = wa + wb
    q = jnp.dot(u, w_ref[:, oq:oq + wq], preferred_element_type=F32)
    k = jnp.dot(u, w_ref[:, oq + wq:oq + wq + wk], preferred_element_type=F32)
    v = jnp.dot(u, w_ref[:, oq + wq + wk:oq + wq + 2 * wk], preferred_element_type=F32)
    scale = C_HEAD_DIM ** -0.5
    if rope:
        cs = cos_ref[...]
        sn = sin_ref[...]
        for j in range(wq // LANES):
            xq = q[:, j * LANES:(j + 1) * LANES]
            pq_ref[:, j * LANES:(j + 1) * LANES] = ((xq * cs + _swap16(xq) * sn) * scale).astype(BF16)
        k = k * cs + _swap16(k) * sn
    else:
        pq_ref[...] = (q * scale).astype(BF16)
    pk_ref[...] = _head_pair_variants(k)
    pv_ref[...] = _head_pair_variants(v)


def _inproj(h2, mod, w, cos_t, sin_t, rows_per_mod, rope, widths, tm):
    r, d = h2.shape
    wa, wb, wq, wk = widths
    tpm = rows_per_mod // tm
    body = functools.partial(_inproj_body, wa=wa, wb=wb, wq=wq, wk=wk, rope=rope)
    ow = (wa, wb, wq, 4 * wk, 4 * wk)
    od = (F32, F32, BF16, BF16, BF16)
    return pl.pallas_call(
        body,
        out_shape=[jax.ShapeDtypeStruct((r, n), t) for n, t in zip(ow, od)],
        grid=(r // tm,),
        in_specs=[pl.BlockSpec((tm, d), lambda i: (i, 0)),
                  pl.BlockSpec((1, N_MOD, d), lambda i: (i // tpm, 0, 0)),
                  _const_spec(w.shape),
                  pl.BlockSpec((tm, LANES), lambda i: (i % tpm, 0)),
                  pl.BlockSpec((tm, LANES), lambda i: (i % tpm, 0))],
        out_specs=[pl.BlockSpec((tm, n), lambda i: (i, 0)) for n in ow],
        compiler_params=_cparams(("parallel",)),
        name="inproj",
    )(h2, mod, w, cos_t, sin_t)


HGRN_CHUNK = 128


def _pivot_rows(b_ref, c, r0):
    t, w = b_ref.shape
    g = 2 * c
    if g >= SUBLANES:
        rows = [jnp.broadcast_to(b_ref[s + r0:s + r0 + 1, :], (g, w)) for s in range(0, t, g)]
        return rows[0] if len(rows) == 1 else jnp.concatenate(rows, axis=0)
    b = b_ref[...]
    row = lax.broadcasted_iota(jnp.int32, (t, 1), 0) % g
    out = b
    for m in range(g):
        if m != r0:
            out = jnp.where(row == m, pltpu.roll(b, (m - r0) % t, 0), out)
    return out


def _hgrn_chunk(q, v, z, lb, tri, lvl, st, b_ref, reverse):
    t = q.shape[0]
    e = jnp.exp(-jnp.abs(z))
    r = 1.0 / (1.0 + e)
    pos = z >= 0.0
    sig_p = jnp.where(pos, r, e * r)
    sig_n = jnp.where(pos, e * r, r)
    logf = jnp.log(lb + (1.0 - lb) * sig_p)
    kk = (1.0 - lb) * sig_n
    hi = logf.astype(BF16)
    r1 = logf - hi.astype(F32)
    mid = r1.astype(BF16)
    lo = (r1 - mid.astype(F32)).astype(BF16)
    bb = jnp.dot(tri, jnp.concatenate([hi, mid, lo], axis=1), preferred_element_type=F32)
    w = q.shape[1]
    b = bb[:, 0:w] + bb[:, w:2 * w] + bb[:, 2 * w:3 * w]
    b_ref[...] = b
    vb = v.astype(BF16)
    out = _bdot_nt(q * jnp.exp(b), st)
    ri = lax.broadcasted_iota(jnp.int32, (t, t), 0)
    ci = lax.broadcasted_iota(jnp.int32, (t, t), 1)
    att = jnp.where(ri == ci, jnp.sum(q * kk, axis=-1, keepdims=True), 0.0)
    c = t // 2
    while c >= 1:
        piv = _pivot_rows(b_ref, c, c if reverse else c - 1)
        dec = jnp.exp(-jnp.abs(b - piv))
        att = jnp.where(lvl == c.bit_length() - 1, _bdot_nt(q * dec, kk * dec), att)
        c //= 2
    out = out + jnp.dot(att.astype(BF16), vb, preferred_element_type=F32)
    blast = b[0:1, :] if reverse else b[t - 1:t, :]
    ke = (kk * jnp.exp(blast - b)).astype(BF16)
    return out, st * jnp.exp(blast) + jnp.dot(vb.T, ke, preferred_element_type=F32)


def _hgrn_body(qf_ref, vf_ref, zf_ref, qb_ref, vb_ref, zb_ref, lb_ref, s0_ref, tri_ref, lvl_ref,
               of_ref, ob_ref, sfin_ref, st_ref, b_ref, *, nc, nh):
    i = pl.program_id(1)

    @pl.when(i == 0)
    def _():
        st_ref[...] = s0_ref[0]

    def step(j, carry):
        fwd = pl.ds(pl.multiple_of(j * HGRN_CHUNK, HGRN_CHUNK), HGRN_CHUNK)
        bwd = pl.ds(pl.multiple_of((nc - 1 - j) * HGRN_CHUNK, HGRN_CHUNK), HGRN_CHUNK)
        streams = ((qf_ref, vf_ref, zf_ref, of_ref, fwd), (qb_ref, vb_ref, zb_ref, ob_ref, bwd))
        for h in range(nh):
            cols = slice(h * A_DK, (h + 1) * A_DK)
            for d, (q_r, v_r, z_r, o_r, rows) in enumerate(streams):
                out, st_new = _hgrn_chunk(q_r[0, rows, cols], v_r[0, rows, cols], z_r[0, rows, cols],
                                          lb_ref[d:d + 1, cols], tri_ref[d], lvl_ref[d], st_ref[h, d],
                                          b_ref.at[h, d], reverse=(d == 1))
                o_r[0, rows, cols] = out
                st_ref[h, d] = st_new
        return carry

    lax.fori_loop(0, nc, step, 0)

    @pl.when(i == pl.num_programs(1) - 1)
    def _():
        sfin_ref[0] = st_ref[...]


def _hgrn(pa3, lb, s0, t):
    bsz, length, _ = pa3.shape
    w = lb.shape[1]
    nh = w // A_DK
    n = length // t
    ii = jnp.arange(HGRN_CHUNK, dtype=jnp.int32)
    tri = jnp.stack([ii[:, None] >= ii[None, :], ii[:, None] <= ii[None, :]]).astype(BF16)
    top = 31 - lax.clz(ii[:, None] ^ ii[None, :])
    lvl = jnp.stack([jnp.where(ii[:, None] > ii[None, :], top, -1), jnp.where(ii[:, None] < ii[None, :], top, -1)])

    def col(off, rev):
        if rev:
            return pl.BlockSpec((1, t, w), lambda b, i: (b, n - 1 - i, off))
        return pl.BlockSpec((1, t, w), lambda b, i: (b, i, off))

    st_spec = pl.BlockSpec((1, nh, 2, A_DK, A_DK), lambda b, i: (b, 0, 0, 0, 0))
    sq = pl.BlockSpec((2, HGRN_CHUNK, HGRN_CHUNK), lambda b, i: (0, 0, 0))
    return pl.pallas_call(
        functools.partial(_hgrn_body, nc=t // HGRN_CHUNK, nh=nh),
        out_shape=[jax.ShapeDtypeStruct((bsz, length, w), F32),
                   jax.ShapeDtypeStruct((bsz, length, w), F32),
                   jax.ShapeDtypeStruct((bsz, nh, 2, A_DK, A_DK), F32)],
        grid=(bsz, n),
        in_specs=[col(0, False), col(1, False), col(3, False),
                  col(0, True), col(1, True), col(4, True),
                  pl.BlockSpec((2, w), lambda b, i: (0, 0)),
                  st_spec, sq, sq],
        out_specs=[col(0, False), col(0, True), st_spec],
        scratch_shapes=[pltpu.VMEM((nh, 2, A_DK, A_DK), F32), pltpu.VMEM((nh, 2, HGRN_CHUNK, A_DK), F32)],
        compiler_params=_cparams(("parallel", "arbitrary")),
        name="hgrn",
    )(pa3, pa3, pa3, pa3, pa3, pa3, lb, s0, tri, lvl)


def _shortconv_body(x_ref, prev_ref, next_ref, w_ref, b_ref, o0_ref, o1_ref, o2_ref, *, tl, cw):
    i = pl.program_id(1)
    x = x_ref[0]
    row = lax.broadcasted_iota(jnp.int32, (tl, 1), 0)
    prev_row = jnp.where(i > 0, prev_ref[0, SUBLANES - 1:SUBLANES, :], 0.0)
    next_row = jnp.where(i < pl.num_programs(1) - 1, next_ref[0, 0:1, :], 0.0)
    xm = jnp.where(row == 0, prev_row, pltpu.roll(x, 1, 0))
    xp = jnp.where(row == tl - 1, next_row, pltpu.roll(x, tl - 1, 0))
    y = xm * w_ref[0:1, :] + x * w_ref[1:2, :] + xp * w_ref[2:3, :] + b_ref[...]
    o0_ref[0] = y[:, 0:cw]
    o1_ref[0] = y[:, cw:2 * cw]
    o2_ref[0] = y[:, 2 * cw:3 * cw]


def _shortconv(pb3, w, b, tl):
    bsz, length, wd = pb3.shape
    cw = wd // 3
    n8 = length // SUBLANES
    r8 = tl // SUBLANES
    body = functools.partial(_shortconv_body, tl=tl, cw=cw)
    out = jax.ShapeDtypeStruct((bsz, length, cw), F32)
    return pl.pallas_call(
        body,
        out_shape=[out, out, out],
        grid=(bsz, length // tl),
        in_specs=[pl.BlockSpec((1, tl, wd), lambda bb, i: (bb, i, 0)),
                  pl.BlockSpec((1, SUBLANES, wd), lambda bb, i: (bb, jnp.maximum(i * r8 - 1, 0), 0)),
                  pl.BlockSpec((1, SUBLANES, wd), lambda bb, i: (bb, jnp.minimum((i + 1) * r8, n8 - 1), 0)),
                  pl.BlockSpec((3, wd), lambda bb, i: (0, 0)),
                  pl.BlockSpec((1, wd), lambda bb, i: (0, 0))],
        out_specs=[pl.BlockSpec((1, tl, cw), lambda bb, i: (bb, i, 0))] * 3,
        compiler_params=_cparams(("parallel", "parallel")),
        name="short_conv",
    )(pb3, pb3, pb3, w, b.reshape(1, wd))


def _hyena_feats(length):
    n = jnp.arange(2 * length)
    pos = jnp.where(n < length, n, 2 * length - n).astype(F32)
    t = pos / (length - 1)
    w = 2.0 * math.pi * pos / length
    bands = jnp.linspace(1e-4, B_BANDS - 1, B_BANDS, dtype=F32)
    feats = jnp.concatenate([t[:, None], jnp.cos(w[:, None] * bands), -jnp.sin(w[:, None] * bands)], axis=-1)
    return jnp.pad(feats, ((0, 0), (0, B_FFN - B_EMB)))


def _decay_rates(width):
    max_decay = math.log(B_TARGET) / B_FAST_DECAY
    min_decay = math.log(B_TARGET) / B_SLOW_DECAY
    return jnp.abs(jnp.linspace(min_decay, max_decay, width, dtype=F32)).reshape(1, width)


def _filter_body(f_ref, w1_ref, b1_ref, f1_ref, w2_ref, b2_ref, f2_ref, w3_ref, rate_ref, o_ref, l1_ref,
                 *, tr, length):
    i = pl.program_id(0)
    f = f_ref[...]
    h = jnp.sin(f1_ref[...] * (_hdot(f, w1_ref[...]) + b1_ref[...]))
    h = jnp.sin(f2_ref[...] * (_hdot(h, w2_ref[...]) + b2_ref[...]))
    h = _hdot(h, w3_ref[...])
    dec = jnp.exp(-(f[:, 0:1] * rate_ref[...]))
    h = h * jnp.concatenate([dec] * B_ORDER, axis=1)
    row = i * tr + lax.broadcasted_iota(jnp.int32, (tr, 1), 0)
    h = jnp.where(row == length, 0.0, h)
    o_ref[...] = h

    @pl.when(i == 0)
    def _():
        l1_ref[...] = jnp.zeros_like(l1_ref)

    l1_ref[...] += jnp.sum(jnp.abs(h), axis=0, keepdims=True)


def _hyena_filter(feats, w1, b1, f1, w2, b2, f2, w3, rates, length):
    width = rates.shape[1]
    ow = B_ORDER * width
    tr = min(512, length)
    nblk = 2 * length // tr
    w1p = jnp.pad(w1, ((0, B_FFN - B_EMB), (0, 0)))
    body = functools.partial(_filter_body, tr=tr, length=length)
    small = lambda shape: pl.BlockSpec(shape, lambda i: (0, 0))
    return pl.pallas_call(
        body,
        out_shape=[jax.ShapeDtypeStruct((2 * length, ow), F32), jax.ShapeDtypeStruct((1, ow), F32)],
        grid=(nblk,),
        in_specs=[pl.BlockSpec((tr, B_FFN), lambda i: (i, 0)),
                  small((B_FFN, B_FFN)), small((1, B_FFN)), small((1, B_FFN)),
                  small((B_FFN, B_FFN)), small((1, B_FFN)), small((1, B_FFN)),
                  pl.BlockSpec((B_FFN, ow), lambda i: (0, (2 * i) // nblk)),
                  small((1, width))],
        out_specs=[pl.BlockSpec((tr, ow), lambda i: (i, 0)), pl.BlockSpec((1, ow), lambda i: (0, 0))],
        compiler_params=_cparams(("arbitrary",)),
        name="hyena_filter",
    )(feats, w1p, b1.reshape(1, -1), f1.reshape(1, -1), w2, b2.reshape(1, -1), f2.reshape(1, -1), w3, rates)


def _dft_tables(length):
    n = 2 * length
    n2 = DFT_N2
    n1 = n // n2
    n1h = n1 // 2
    k1 = jnp.arange(n1, dtype=jnp.int32)[None, :, None]
    j1 = jnp.arange(n1, dtype=jnp.int32)[None, None, :]
    j2 = jnp.arange(n2, dtype=jnp.int32)[:, None, None]
    ang = ((k1 * (n2 * j1 + j2)) % n).astype(F32) * (-2.0 * math.pi / n)
    er, ei = jnp.cos(ang), jnp.sin(ang)
    a_f = jnp.concatenate([er, ei], axis=1)
    erh, eih = er[:, :, :n1h], ei[:, :, :n1h]
    a_z = jnp.concatenate([jnp.concatenate([erh, -eih], axis=2),
                           jnp.concatenate([eih, erh], axis=2)], axis=1)
    a_inv = jnp.swapaxes(a_z, 1, 2) / n
    kk = jnp.arange(n2, dtype=jnp.int32)
    ang2 = ((kk[:, None] * kk[None, :]) % n2).astype(F32) * (-2.0 * math.pi / n2)
    fr, fi = jnp.cos(ang2), jnp.sin(ang2)
    f2 = jnp.concatenate([jnp.concatenate([fr, -fi], axis=1), jnp.concatenate([fi, fr], axis=1)], axis=0)
    return a_f.astype(BF16), a_z.astype(BF16), a_inv.astype(BF16), f2.astype(BF16), f2.T.astype(BF16)


def _dfta_fwd_body(x_ref, a_ref, s_ref, o_ref, *, nb):
    for j in range(nb):
        x = x_ref[:, j, :] * s_ref[...]
        o_ref[:, j, :] = jnp.dot(a_ref[j], x.astype(BF16), preferred_element_type=F32)


def _dfta_fwd(x3, a, scale, nb):
    rows, n2, cw = x3.shape
    m = a.shape[1]
    return pl.pallas_call(
        functools.partial(_dfta_fwd_body, nb=nb),
        out_shape=jax.ShapeDtypeStruct((m, n2, cw), F32),
        grid=(n2 // nb,),
        in_specs=[pl.BlockSpec((rows, nb, cw), lambda i: (0, i, 0)),
                  pl.BlockSpec((nb, m, rows), lambda i: (i, 0, 0)),
                  pl.BlockSpec((1, cw), lambda i: (0, 0))],
        out_specs=pl.BlockSpec((m, nb, cw), lambda i: (0, i, 0)),
        compiler_params=_cparams(("parallel",)),
        name="dft_stage_a",
    )(x3, a, scale)


def _dfta_inv_body(u_ref, a_ref, xn_ref, z_ref, bias_ref, o_ref, *, nb):
    for j in range(nb):
        y = jnp.dot(a_ref[j], u_ref[:, j, :].astype(BF16), preferred_element_type=F32)
        o_ref[:, j, :] = xn_ref[:, j, :] * (y + bias_ref[...] * z_ref[:, j, :])


def _dfta_inv(u3, a_inv, xn3, z3, bias, nb):
    n2, m, k = a_inv.shape
    cw = u3.shape[2]
    return pl.pallas_call(
        functools.partial(_dfta_inv_body, nb=nb),
        out_shape=jax.ShapeDtypeStruct((m, n2, cw), F32),
        grid=(n2 // nb,),
        in_specs=[pl.BlockSpec((k, nb, cw), lambda i: (0, i, 0)),
                  pl.BlockSpec((nb, m, k), lambda i: (i, 0, 0)),
                  pl.BlockSpec((m, nb, cw), lambda i: (0, i, 0)),
                  pl.BlockSpec((m, nb, cw), lambda i: (0, i, 0)),
                  pl.BlockSpec((1, cw), lambda i: (0, 0))],
        out_specs=pl.BlockSpec((m, nb, cw), lambda i: (0, i, 0)),
        compiler_params=_cparams(("parallel",)),
        name="dft_stage_a_inv",
    )(u3, a_inv, xn3, z3, bias)


def _dftc_filter_body(t_ref, f_ref, o_ref, *, kb):
    for j in range(kb):
        t = t_ref[:, j]
        t2 = t.reshape(2 * DFT_N2, t.shape[-1])
        y = jnp.dot(f_ref[...], t2.astype(BF16), preferred_element_type=F32)
        o_ref[j] = y.reshape(2, DFT_N2, t.shape[-1])


def _dftc_filter(t4, f2, kb):
    _, n1, n2, ow = t4.shape
    body = functools.partial(_dftc_filter_body, kb=kb)
    return pl.pallas_call(
        body,
        out_shape=jax.ShapeDtypeStruct((n1, 2, n2, ow), F32),
        grid=(n1 // kb,),
        in_specs=[pl.BlockSpec((2, kb, n2, ow), lambda i: (0, i, 0, 0)),
                  pl.BlockSpec((2 * n2, 2 * n2), lambda i: (0, 0))],
        out_specs=pl.BlockSpec((kb, 2, n2, ow), lambda i: (i, 0, 0, 0)),
        compiler_params=_cparams(("parallel",)),
        name="dft_stage_c_filter",
    )(t4, f2)


def _dftc_mid_body(t_ref, h_ref, f_ref, fi_ref, o_ref, *, kb):
    cw = t_ref.shape[-1]
    for j in range(kb):
        t2 = t_ref[:, j].reshape(2 * DFT_N2, cw)
        y = jnp.dot(f_ref[...], t2.astype(BF16), preferred_element_type=F32)
        yr, yi = y[:DFT_N2], y[DFT_N2:]
        hr, hi = h_ref[j, 0], h_ref[j, 1]
        p = jnp.concatenate([yr * hr - yi * hi, yr * hi + yi * hr], axis=0)
        u = jnp.dot(fi_ref[...], p.astype(BF16), preferred_element_type=F32)
        o_ref[0, j] = u[:DFT_N2]
        o_ref[1, j] = u[DFT_N2:]


def _dftc_mid(t4, hspec, order, f2, f2inv, kb):
    _, n1, n2, cw = t4.shape
    body = functools.partial(_dftc_mid_body, kb=kb)
    return pl.pallas_call(
        body,
        out_shape=jax.ShapeDtypeStruct((2, n1, n2, cw), F32),
        grid=(n1 // kb,),
        in_specs=[pl.BlockSpec((2, kb, n2, cw), lambda i: (0, i, 0, 0)),
                  pl.BlockSpec((kb, 2, n2, cw), lambda i: (i, 0, 0, order)),
                  pl.BlockSpec((2 * n2, 2 * n2), lambda i: (0, 0)),
                  pl.BlockSpec((2 * n2, 2 * n2), lambda i: (0, 0))],
        out_specs=pl.BlockSpec((2, kb, n2, cw), lambda i: (0, i, 0, 0)),
        compiler_params=_cparams(("parallel",)),
        name="dft_stage_c_mid",
    )(t4, hspec, f2, f2inv)


def _hyena_long(parts, filt, l1, bias, tables):
    a_f, a_z, a_inv, f2, f2inv = tables
    bsz, length, cw = parts[0].shape
    assert bsz == 2, "the two batch rows ride as real and imaginary parts of one transform"
    n1 = 2 * length // DFT_N2
    ow = filt.shape[1]
    tf = _dfta_fwd(filt.reshape(n1, DFT_N2, ow), a_f, 1.0 / l1, SUBLANES)
    hspec = _dftc_filter(tf.reshape(2, n1, DFT_N2, ow), f2, 2)
    ones = jnp.ones((1, cw), F32)
    z = parts[0].reshape(bsz * n1 // 2, DFT_N2, cw)
    for o in range(B_ORDER):
        t = _dfta_fwd(z, a_z, ones, SUBLANES)
        u = _dftc_mid(t.reshape(2, n1, DFT_N2, cw), hspec, o, f2, f2inv, 4)
        z = _dfta_inv(u.reshape(2 * n1, DFT_N2, cw), a_inv, parts[o + 1].reshape(z.shape), z,
                      bias[o:o + 1], SUBLANES)
    return z.reshape(bsz, length, cw)


def _ctx_conv_body(v_ref, x1_ref, x2_ref, filt_ref, il1_ref, bias_ref, ff_ref, fz_ref, fi_ref, o_ref, *, n, cw):
    hf = _hdot(ff_ref[...], filt_ref[...] * il1_ref[...])
    z = jnp.concatenate([v_ref[0], v_ref[1]], axis=0)
    for o, x_ref in enumerate((x1_ref, x2_ref)):
        zz = _hdot(fz_ref[...], z)
        zr, zi = zz[:n], zz[n:]
        hr, hi = hf[:n, o * cw:(o + 1) * cw], hf[n:, o * cw:(o + 1) * cw]
        y = _hdot(fi_ref[...], jnp.concatenate([zr * hr - zi * hi, zr * hi + zi * hr], axis=0))
        xn = jnp.concatenate([x_ref[0], x_ref[1]], axis=0)
        z = xn * (y + bias_ref[o:o + 1, :] * z)
    half = n // 2
    o_ref[0] = z[:half]
    o_ref[1] = z[half:]


def _hyena_ctx(parts, filt, l1, bias):
    bsz, length, cw = parts[0].shape
    assert bsz == 2
    n = 2 * length
    k = jnp.arange(n, dtype=jnp.int32)
    ang = ((k[:, None] * k[None, :]) % n).astype(F32) * (-2.0 * math.pi / n)
    cr, ci = jnp.cos(ang), jnp.sin(ang)
    ff = jnp.concatenate([cr, ci], axis=0)
    crh, cih = cr[:, :length], ci[:, :length]
    fz = jnp.concatenate([jnp.concatenate([crh, -cih], axis=1), jnp.concatenate([cih, crh], axis=1)], axis=0)
    fi = fz.T / n
    body = functools.partial(_ctx_conv_body, n=n, cw=cw)
    full = lambda a: pl.BlockSpec(a.shape, lambda i: (0,) * a.ndim)
    args = (parts[0], parts[1], parts[2], filt, 1.0 / l1, bias, ff, fz, fi)
    return pl.pallas_call(
        body,
        out_shape=jax.ShapeDtypeStruct((bsz, length, cw), F32),
        grid=(1,),
        in_specs=[full(a) for a in args],
        out_specs=pl.BlockSpec((bsz, length, cw), lambda i: (0, 0, 0)),
        compiler_params=_cparams(("arbitrary",)),
        name="hyena_ctx",
    )(*args)


def _rope_tables(length):
    n_rows = length // GRID_W
    row = jnp.repeat(jnp.arange(n_rows), GRID_W).astype(F32)
    col = jnp.tile(jnp.arange(GRID_W), n_rows).astype(F32)
    nf = C_HEAD_DIM // 4
    inv = ROPE_BASE ** (-jnp.arange(nf, dtype=F32) * 2.0 / (C_HEAD_DIM // 2))
    ar, ac = row[:, None] * inv, col[:, None] * inv
    cos_h = jnp.concatenate([jnp.cos(ar), jnp.cos(ar), jnp.cos(ac), jnp.cos(ac)], axis=-1)
    sin_h = jnp.concatenate([-jnp.sin(ar), jnp.sin(ar), -jnp.sin(ac), jnp.sin(ac)], axis=-1)
    reps = LANES // C_HEAD_DIM
    return jnp.tile(cos_h, (1, reps)), jnp.tile(sin_h, (1, reps))


def _softmax_heads(q_ref, keys, vals, sink_ref, masks, o_ref):
    rows = q_ref.shape[1]
    top = lax.broadcasted_iota(jnp.int32, (2 * rows, 1), 0) < rows
    for g in range(C_KV_HEADS):
        q = jnp.concatenate([q_ref[0, :, (2 * g) * LANES:(2 * g + 1) * LANES],
                             q_ref[0, :, (2 * g + 1) * LANES:(2 * g + 2) * LANES]], axis=0)
        acc = None
        for half in range(2):
            off = (2 * g + half) * LANES
            ss = []
            for k_ref, msk in zip(keys, masks):
                s = lax.dot_general(q, k_ref[0, :, off:off + LANES], (((1,), (1,)), ((), ())),
                                    preferred_element_type=F32)
                ss.append(s if msk is None else jnp.where(jnp.concatenate([msk, msk], axis=0), s, NEG_BIG))
            blocks = [s[:, j * LANES:(j + 1) * LANES] for s in ss for j in range(s.shape[1] // LANES)]
            mx = functools.reduce(jnp.maximum, blocks)
            sk = jnp.where(top, sink_ref[C_GROUP * g + half], sink_ref[C_GROUP * g + 2 + half])
            m = jnp.maximum(jnp.max(mx, axis=-1, keepdims=True), sk)
            ps = [jnp.exp(s - m) for s in ss]
            pblocks = [p[:, j * LANES:(j + 1) * LANES] for p in ps for j in range(p.shape[1] // LANES)]
            den = jnp.sum(functools.reduce(jnp.add, pblocks), axis=-1, keepdims=True) + jnp.exp(sk - m)
            o = None
            for p, v_ref in zip(ps, vals):
                t = jnp.dot(p.astype(BF16), v_ref[0, :, off:off + LANES], preferred_element_type=F32)
                o = t if o is None else o + t
            o = o * (1.0 / den)
            acc = o if acc is None else acc + o
        o_ref[0, :, (2 * g) * LANES:(2 * g + 1) * LANES] = acc[:rows]
        o_ref[0, :, (2 * g + 1) * LANES:(2 * g + 2) * LANES] = acc[rows:]


def _attn_body(sink_ref, q_ref, kp_ref, kc_ref, kn_ref, kx_ref, vp_ref, vc_ref, vn_ref, vx_ref, o_ref):
    n = pl.program_id(1)
    blk = q_ref.shape[1]
    ri = lax.broadcasted_iota(jnp.int32, (blk, blk), 0)
    ci = lax.broadcasted_iota(jnp.int32, (blk, blk), 1)
    mask_p = jnp.logical_and(ci >= ri, n > 0)
    mask_n = jnp.logical_and(ci <= ri, n < pl.num_programs(1) - 1)
    _softmax_heads(q_ref, (kp_ref, kc_ref, kn_ref, kx_ref), (vp_ref, vc_ref, vn_ref, vx_ref), sink_ref,
                   (mask_p, None, mask_n, None), o_ref)


def _attention(q3, k3, v3, kx3, vx3, sink):
    bsz, length, wq = q3.shape
    wk = k3.shape[2]
    lc = kx3.shape[1]
    nb = length // C_BLOCK
    cur = lambda w: pl.BlockSpec((1, C_BLOCK, w), lambda b, i: (b, i, 0))
    prv = lambda w: pl.BlockSpec((1, C_BLOCK, w), lambda b, i: (b, jnp.maximum(i - 1, 0), 0))
    nxt = lambda w: pl.BlockSpec((1, C_BLOCK, w), lambda b, i: (b, jnp.minimum(i + 1, nb - 1), 0))
    ctx = pl.BlockSpec((1, lc, wk), lambda b, i: (b, 0, 0))
    return pl.pallas_call(
        _attn_body,
        out_shape=jax.ShapeDtypeStruct((bsz, length, wq), F32),
        grid=(bsz, nb),
        in_specs=[pl.BlockSpec(memory_space=pltpu.SMEM), cur(wq), prv(wk), cur(wk), nxt(wk), ctx,
                  prv(wk), cur(wk), nxt(wk), ctx],
        out_specs=cur(wq),
        compiler_params=_cparams(("parallel", "parallel")),
        name="window_attn",
    )(sink, q3, k3, k3, k3, kx3, v3, v3, v3, vx3)


def _ctx_attn_body(sink_ref, q_ref, k_ref, v_ref, o_ref):
    _softmax_heads(q_ref, (k_ref,), (v_ref,), sink_ref, (None,), o_ref)


def _ctx_attention(q3, k3, v3, sink):
    bsz, lc, wq = q3.shape
    wk = k3.shape[2]
    spec = lambda w: pl.BlockSpec((1, lc, w), lambda b: (b, 0, 0))
    return pl.pallas_call(
        _ctx_attn_body,
        out_shape=jax.ShapeDtypeStruct((bsz, lc, wq), F32),
        grid=(bsz,),
        in_specs=[pl.BlockSpec(memory_space=pltpu.SMEM), spec(wq), spec(wk), spec(wk)],
        out_specs=spec(wq),
        compiler_params=_cparams(("parallel",)),
        name="ctx_attn",
    )(sink, q3, k3, v3)


def _merge_body(h_ref, mod_ref, of_ref, ob_ref, ga_ref, yb_ref, yc_ref, wg_ref, wbr_ref, wo_ref, nw_ref,
                g_ref, b_ref, o_ref, *, alpha):
    h = h_ref[...]
    d = h.shape[1]
    u = (h * (1.0 + mod_ref[0, 4:5, :]) + mod_ref[0, 3:4, :]).astype(BF16)
    o = of_ref[...] + ob_ref[...]
    heads = []
    for hh in range(o.shape[1] // A_DK):
        oh = o[:, hh * A_DK:(hh + 1) * A_DK]
        ms = jnp.mean(oh * oh, axis=-1, keepdims=True)
        heads.append(oh * lax.rsqrt(ms + RMS_EPS) * nw_ref[:, hh * A_DK:(hh + 1) * A_DK])
    ya = jnp.concatenate(heads, axis=1) * _silu(ga_ref[...])
    m = jnp.zeros_like(h)
    for n, y in enumerate((ya, yb_ref[...], yc_ref[...])):
        gate = jnp.dot(u, wg_ref[:, n * d:(n + 1) * d], preferred_element_type=F32)
        m = m + jax.nn.sigmoid(gate) * jnp.dot(y.astype(BF16), wbr_ref[n], preferred_element_type=F32)
    y = jnp.dot(m.astype(BF16), wo_ref[...], preferred_element_type=F32)
    r = alpha * h + mod_ref[0, 5:6, :] * y
    o_ref[...] = _layer_norm(r, g_ref[...], b_ref[...])


def _merge(h2, mod, of2, ob2, pa2, yb2, yc2, wg, wbr, wo, norm_w, g, b, rows_per_mod, alpha, tm):
    r, d = h2.shape
    bw = of2.shape[1]
    tpm = rows_per_mod // tm
    body = functools.partial(_merge_body, alpha=alpha)
    row = lambda w: pl.BlockSpec((tm, w), lambda i: (i, 0))
    return pl.pallas_call(
        body,
        out_shape=jax.ShapeDtypeStruct((r, d), F32),
        grid=(r // tm,),
        in_specs=[row(d), pl.BlockSpec((1, N_MOD, d), lambda i: (i // tpm, 0, 0)),
                  row(bw), row(bw), pl.BlockSpec((tm, bw), lambda i: (i, 2)), row(bw), row(bw),
                  _const_spec(wg.shape), _const_spec(wbr.shape), _const_spec(wo.shape),
                  _const_spec((1, bw)), _const_spec((1, d)), _const_spec((1, d))],
        out_specs=row(d),
        compiler_params=_cparams(("parallel",)),
        name="merge",
    )(h2, mod, of2, ob2, pa2, yb2, yc2, wg, wbr, wo, norm_w.reshape(1, bw), g.reshape(1, d), b.reshape(1, d))


def kernel(x, c, ctx, c_ctx, ada_w, ada_b, ln_g, ln_b, ffn_w_in, ffn_w_out, mix_w_in, hgrn_lb, hgrn_norm_w,
           hyena_conv_w, hyena_conv_b, hyena_w1, hyena_b1, hyena_f1, hyena_w2, hyena_b2, hyena_f2, hyena_w3,
           hyena_bias, attn_sink, branch_w, out_w):
    bsz, seq, d = x.shape
    lc = ctx.shape[1]
    depth = ada_w.shape[0]
    alpha = (2.0 * depth) ** 0.25
    bw = hgrn_lb.shape[2]
    wk = C_KV_HEADS * C_HEAD_DIM
    widths = (5 * bw, (B_ORDER + 1) * bw, bw, wk)
    off_g = widths[0] + widths[1] + widths[2] + 2 * wk
    tm = 512
    tmc = min(256, bsz * lc)

    s = jax.nn.softmax(hgrn_lb.astype(F32), axis=0)
    lower_bounds = jnp.cumsum(s, axis=0) - s[0:1]

    c8 = jnp.zeros((SUBLANES, d), F32).at[:bsz].set(c).at[bsz].set(c_ctx)
    mods = _ada_mod(c8, ada_w, ada_b).reshape(depth, SUBLANES, N_MOD, d)

    cos_t, sin_t = _rope_tables(seq)
    feats = _hyena_feats(seq)
    feats_c = _hyena_feats(lc)
    rates = _decay_rates(bw)
    tables = _dft_tables(seq)

    w_in_bf = ffn_w_in.astype(BF16)
    w_out_bf = ffn_w_out.astype(BF16)
    mix_bf = mix_w_in.astype(BF16)
    br_bf = branch_w.astype(BF16)
    out_bf = out_w.astype(BF16)

    h = x.reshape(bsz * seq, d)
    hc = ctx.reshape(bsz * lc, d)
    for l in range(depth):
        last = l == depth - 1
        mod = mods[l, :bsz]
        modc = mods[l, bsz:bsz + 1]
        w_proj = mix_bf[l, :, :off_g]
        w_gate = mix_bf[l, :, off_g:]
        ffn = lambda t, mm, m0, j, g, rpm, tt: _ffn(t, mm, m0, w_in_bf[l, j], w_out_bf[l, j], ln_g[l, g],
                                                     ln_b[l, g], rpm, alpha, tt)
        h = ffn(h, mod, 0, 0, 0, seq, tm)
        hc = ffn(hc, modc, 0, 0, 0, bsz * lc, tmc)

        pa, pb, pq, pk, pv = _inproj(h, mod, w_proj, cos_t, sin_t, seq, True, widths, 256)
        ca, cb, cq, ck, cv = _inproj(hc, modc, w_proj, cos_t, sin_t, bsz * lc, False, widths, tmc)

        s0 = jnp.zeros((bsz, A_HEADS, 2, A_DK, A_DK), F32)
        ocf, ocb, s_ctx = _hgrn(ca.reshape(bsz, lc, -1), lower_bounds[l], s0, lc)
        of, ob, _ = _hgrn(pa.reshape(bsz, seq, -1), lower_bounds[l], s_ctx, 512)

        parts = _shortconv(pb.reshape(bsz, seq, -1), hyena_conv_w[l], hyena_conv_b[l], 512)
        fargs = (hyena_w1[l], hyena_b1[l], hyena_f1[l], hyena_w2[l], hyena_b2[l], hyena_f2[l], hyena_w3[l], rates)
        filt, l1 = _hyena_filter(feats, *fargs, seq)
        yb = _hyena_long(parts, filt, l1, hyena_bias[l], tables)

        kx, vx = ck.reshape(bsz, lc, 4 * wk), cv.reshape(bsz, lc, 4 * wk)
        yc = _attention(pq.reshape(bsz, seq, bw), pk.reshape(bsz, seq, 4 * wk), pv.reshape(bsz, seq, 4 * wk),
                        kx, vx, attn_sink[l])

        merge = lambda t, mm, a1, a2, a3, a4, a5, rpm, tt: _merge(
            t, mm, a1, a2, a3, a4, a5, w_gate, br_bf[l], out_bf[l], hgrn_norm_w[l], ln_g[l, 1], ln_b[l, 1],
            rpm, alpha, tt)
        h = merge(h, mod, of.reshape(-1, bw), ob.reshape(-1, bw), pa, yb.reshape(-1, bw), yc.reshape(-1, bw),
                  seq, 256)
        h = ffn(h, mod, 6, 1, 2, seq, tm)
        if not last:
            cparts = _shortconv(cb.reshape(bsz, lc, -1), hyena_conv_w[l], hyena_conv_b[l], lc)
            cfilt, cl1 = _hyena_filter(feats_c, *fargs, lc)
            ycb = _hyena_ctx(cparts, cfilt, cl1, hyena_bias[l])
            ycc = _ctx_attention(cq.reshape(bsz, lc, bw), kx, vx, attn_sink[l])
            hc = merge(hc, modc, ocf.reshape(-1, bw), ocb.reshape(-1, bw), ca, ycb.reshape(-1, bw),
                       ycc.reshape(-1, bw), bsz * lc, tmc)
            hc = ffn(hc, modc, 6, 1, 2, bsz * lc, tmc)
    return h.reshape(bsz, seq, d)
```

```python
import functools
import math

import jax
import jax.numpy as jnp
from jax import lax
from jax.experimental import pallas as pl
from jax.experimental.pallas import tpu as pltpu

F32 = jnp.float32
BF16 = jnp.bfloat16
HI = lax.Precision.HIGHEST

LANES = 128
SUBLANES = 8
VMEM_LIMIT = 56 * 1024 * 1024

N_MOD = 9
A_HEADS = 4
A_DK = 128
B_ORDER = 2
B_EMB = 33
B_BANDS = 16
B_FFN = 64
B_FAST_DECAY = 0.3
B_SLOW_DECAY = 1.5
B_TARGET = 1e-2
C_HEAD_DIM = 64
C_HEADS = 8
C_KV_HEADS = 2
C_GROUP = 4
C_BLOCK = 128
GRID_W = 64
ROPE_BASE = 10000.0
LN_EPS = 1e-5
RMS_EPS = 1e-6
DFT_N2 = 256
NEG_BIG = -1e30


def _cparams(sem, vmem=VMEM_LIMIT):
    return pltpu.CompilerParams(dimension_semantics=sem, vmem_limit_bytes=vmem)


def _const_spec(shape):
    nd = len(shape)
    return pl.BlockSpec(shape, lambda *_: (0,) * nd, pipeline_mode=pl.Buffered(1))


def _bdot(a, b):
    return jnp.dot(a.astype(BF16), b.astype(BF16), preferred_element_type=F32)


def _bdot_nt(a, b):
    return lax.dot_general(a.astype(BF16), b.astype(BF16), (((1,), (1,)), ((), ())),
                           preferred_element_type=F32)


def _hdot(a, b):
    return jnp.dot(a, b, precision=HI, preferred_element_type=F32)


def _layer_norm(x, g, b):
    mu = jnp.mean(x, axis=-1, keepdims=True)
    xc = x - mu
    var = jnp.mean(xc * xc, axis=-1, keepdims=True)
    return xc * lax.rsqrt(var + LN_EPS) * g + b


def _silu(x):
    return x * jax.nn.sigmoid(x)


def _ada_body(c_ref, w_ref, b_ref, o_ref):
    o_ref[0] = _hdot(_silu(c_ref[...]), w_ref[0]) + b_ref[0]


def _ada_mod(c8, ada_w, ada_b):
    depth, d, nw = ada_w.shape
    tn = nw // 8
    return pl.pallas_call(
        _ada_body,
        out_shape=jax.ShapeDtypeStruct((depth, SUBLANES, nw), F32),
        grid=(depth, nw // tn),
        in_specs=[pl.BlockSpec((SUBLANES, d), lambda l, j: (0, 0)),
                  pl.BlockSpec((1, d, tn), lambda l, j: (l, 0, j)),
                  pl.BlockSpec((1, 1, tn), lambda l, j: (l, 0, j))],
        out_specs=pl.BlockSpec((1, SUBLANES, tn), lambda l, j: (l, 0, j)),
        compiler_params=_cparams(("parallel", "parallel")),
        name="ada_mod",
    )(c8, ada_w, ada_b.reshape(depth, 1, nw))


def _ffn_body(h_ref, mod_ref, win_ref, wout_ref, g_ref, b_ref, o_ref, *, m0, d_ff, fc, alpha):
    h = h_ref[...]
    u = (h * (1.0 + mod_ref[0, m0 + 1:m0 + 2, :]) + mod_ref[0, m0:m0 + 1, :]).astype(BF16)
    acc = jnp.zeros_like(h)
    for j in range(d_ff // fc):
        a = jnp.dot(u, win_ref[:, j * fc:(j + 1) * fc], preferred_element_type=F32)
        b = jnp.dot(u, win_ref[:, d_ff + j * fc:d_ff + (j + 1) * fc], preferred_element_type=F32)
        act = (_silu(a) * b).astype(BF16)
        acc = acc + jnp.dot(act, wout_ref[j * fc:(j + 1) * fc, :], preferred_element_type=F32)
    r = alpha * h + 0.5 * mod_ref[0, m0 + 2:m0 + 3, :] * acc
    o_ref[...] = _layer_norm(r, g_ref[...], b_ref[...])


def _ffn(h2, mod, m0, w_in, w_out, g, b, rows_per_mod, alpha, tm):
    r, d = h2.shape
    d_ff = w_out.shape[0]
    tpm = rows_per_mod // tm
    body = functools.partial(_ffn_body, m0=m0, d_ff=d_ff, fc=d_ff // 2, alpha=alpha)
    return pl.pallas_call(
        body,
        out_shape=jax.ShapeDtypeStruct((r, d), F32),
        grid=(r // tm,),
        in_specs=[pl.BlockSpec((tm, d), lambda i: (i, 0)),
                  pl.BlockSpec((1, N_MOD, d), lambda i: (i // tpm, 0, 0)),
                  _const_spec(w_in.shape), _const_spec(w_out.shape),
                  _const_spec((1, d)), _const_spec((1, d))],
        out_specs=pl.BlockSpec((tm, d), lambda i: (i, 0)),
        compiler_params=_cparams(("parallel",)),
        name="ffn",
    )(h2, mod, w_in, w_out, g.reshape(1, d), b.reshape(1, d))


def _swap16(x):
    lane = lax.broadcasted_iota(jnp.int32, x.shape, 1)
    return jnp.where((lane % 32) < 16, pltpu.roll(x, LANES - 16, 1), pltpu.roll(x, 16, 1))


def _head_pair_variants(x):
    lane = lax.broadcasted_iota(jnp.int32, x.shape, 1)
    low = lane < C_HEAD_DIM
    xs = pltpu.roll(x, C_HEAD_DIM, 1)
    parts = (jnp.where(low, x, 0.0), jnp.where(low, 0.0, xs), jnp.where(low, xs, 0.0), jnp.where(low, 0.0, x))
    return jnp.concatenate(parts, axis=1).astype(BF16)


def _inproj_body(h_ref, mod_ref, w_ref, cos_ref, sin_ref, pa_ref, pb_ref, pq_ref, pk_ref, pv_ref,
                 *, wa, wb, wq, wk, rope):
    assert wk == LANES
    h = h_ref[...]
    u = (h * (1.0 + mod_ref[0, 4:5, :]) + mod_ref[0, 3:4, :]).astype(BF16)
    pa_ref[...] = jnp.dot(u, w_ref[:, 0:wa], preferred_element_type=F32)
    pb_ref[...] = jnp.dot(u, w_ref[:, wa:wa + wb], preferred_element_type=F32)
    oq = wa + wb
    q = jnp.dot(u, w_ref[:, oq:oq + wq], preferred_element_type=F32)
    k = jnp.dot(u, w_ref[:, oq + wq:oq + wq + wk], preferred_element_type=F32)
    v = jnp.dot(u, w_ref[:, oq + wq + wk:oq + wq + 2 * wk], preferred_element_type=F32)
    scale = C_HEAD_DIM ** -0.5
    if rope:
        cs = cos_ref[...]
        sn = sin_ref[...]
        for j in range(wq // LANES):
            xq = q[:, j * LANES:(j + 1) * LANES]
            pq_ref[:, j * LANES:(j + 1) * LANES] = ((xq * cs + _swap16(xq) * sn) * scale).astype(BF16)
        k = k * cs + _swap16(k) * sn
    else:
        pq_ref[...] = (q * scale).astype(BF16)
    pk_ref[...] = _head_pair_variants(k)
    pv_ref[...] = _head_pair_variants(v)


def _inproj(h2, mod, w, cos_t, sin_t, rows_per_mod, rope, widths, tm):
    r, d = h2.shape
    wa, wb, wq, wk = widths
    tpm = rows_per_mod // tm
    body = functools.partial(_inproj_body, wa=wa, wb=wb, wq=wq, wk=wk, rope=rope)
    ow = (wa, wb, wq, 4 * wk, 4 * wk)
    od = (F32, F32, BF16, BF16, BF16)
    return pl.pallas_call(
        body,
        out_shape=[jax.ShapeDtypeStruct((r, n), t) for n, t in zip(ow, od)],
        grid=(r // tm,),
        in_specs=[pl.BlockSpec((tm, d), lambda i: (i, 0)),
                  pl.BlockSpec((1, N_MOD, d), lambda i: (i // tpm, 0, 0)),
                  _const_spec(w.shape),
                  pl.BlockSpec((tm, LANES), lambda i: (i % tpm, 0)),
                  pl.BlockSpec((tm, LANES), lambda i: (i % tpm, 0))],
        out_specs=[pl.BlockSpec((tm, n), lambda i: (i, 0)) for n in ow],
        compiler_params=_cparams(("parallel",)),
        name="inproj",
    )(h2, mod, w, cos_t, sin_t)


HGRN_CHUNK = 128


def _pivot_rows(b_ref, c, r0):
    t, w = b_ref.shape
    g = 2 * c
    if g >= SUBLANES:
        rows = [jnp.broadcast_to(b_ref[s + r0:s + r0 + 1, :], (g, w)) for s in range(0, t, g)]
        return rows[0] if len(rows) == 1 else jnp.concatenate(rows, axis=0)
    b = b_ref[...]
    row = lax.broadcasted_iota(jnp.int32, (t, 1), 0) % g
    out = b
    for m in range(g):
        if m != r0:
            out = jnp.where(row == m, pltpu.roll(b, (m - r0) % t, 0), out)
    return out


def _split3_bf16(x):
    c = 65537.0
    t = c * x
    hi = t - (t - x)
    rest = x - hi
    t = c * rest
    mid = t - (t - rest)
    lo = rest - mid
    return hi.astype(BF16), mid.astype(BF16), lo.astype(BF16)


def _hgrn_chunk(q, v, z, lb, tri, lvl, sgn_ref, st, b_ref, reverse):
    t = q.shape[0]
    e = jnp.exp(-jnp.abs(z))
    r = 1.0 / (1.0 + e)
    pos = z >= 0.0
    sig_p = jnp.where(pos, r, e * r)
    sig_n = jnp.where(pos, e * r, r)
    logf = jnp.log2(lb + (1.0 - lb) * sig_p)
    kk = (1.0 - lb) * sig_n
    bb = jnp.dot(tri, jnp.concatenate(_split3_bf16(logf), axis=1), preferred_element_type=F32)
    w = q.shape[1]
    b = bb[:, 0:w] + bb[:, w:2 * w] + bb[:, 2 * w:3 * w]
    b_ref[...] = b
    vb = v.astype(BF16)
    out = _bdot_nt(q * jnp.exp2(b), st)
    ri = lax.broadcasted_iota(jnp.int32, (t, t), 0)
    ci = lax.broadcasted_iota(jnp.int32, (t, t), 1)
    att = jnp.where(ri == ci, jnp.sum(q * kk, axis=-1, keepdims=True), 0.0)
    c = t // 2
    while c >= 1:
        level = c.bit_length() - 1
        piv = _pivot_rows(b_ref, c, c if reverse else c - 1)
        dec = jnp.exp2((b - piv) * sgn_ref[level])
        att = jnp.where(lvl == level, _bdot_nt(q * dec, kk * dec), att)
        c //= 2
    out = out + jnp.dot(att.astype(BF16), vb, preferred_element_type=F32)
    blast = b[0:1, :] if reverse else b[t - 1:t, :]
    ke = (kk * jnp.exp2(blast - b)).astype(BF16)
    return out, st * jnp.exp2(blast) + jnp.dot(vb.T, ke, preferred_element_type=F32)


def _hgrn_body(qf_ref, vf_ref, zf_ref, qb_ref, vb_ref, zb_ref, lb_ref, s0_ref, tri_ref, lvl_ref, sgn_ref,
               of_ref, ob_ref, sfin_ref, st_ref, b_ref, *, nc, nh):
    i = pl.program_id(1)

    @pl.when(i == 0)
    def _():
        st_ref[...] = s0_ref[0]

    def step(j, carry):
        fwd = pl.ds(pl.multiple_of(j * HGRN_CHUNK, HGRN_CHUNK), HGRN_CHUNK)
        bwd = pl.ds(pl.multiple_of((nc - 1 - j) * HGRN_CHUNK, HGRN_CHUNK), HGRN_CHUNK)
        streams = ((qf_ref, vf_ref, zf_ref, of_ref, fwd), (qb_ref, vb_ref, zb_ref, ob_ref, bwd))
        for h in range(nh):
            cols = slice(h * A_DK, (h + 1) * A_DK)
            for d, (q_r, v_r, z_r, o_r, rows) in enumerate(streams):
                out, st_new = _hgrn_chunk(q_r[0, rows, cols], v_r[0, rows, cols], z_r[0, rows, cols],
                                          lb_ref[d:d + 1, cols], tri_ref[d], lvl_ref[d], sgn_ref.at[d], st_ref[h, d],
                                          b_ref.at[h, d], reverse=(d == 1))
                o_r[0, rows, cols] = out
                st_ref[h, d] = st_new
        return carry

    lax.fori_loop(0, nc, step, 0)

    @pl.when(i == pl.num_programs(1) - 1)
    def _():
        sfin_ref[0] = st_ref[...]


def _hgrn(pa3, lb, s0, t):
    bsz, length, _ = pa3.shape
    w = lb.shape[1]
    nh = w // A_DK
    n = length // t
    ii = jnp.arange(HGRN_CHUNK, dtype=jnp.int32)
    tri = jnp.stack([ii[:, None] >= ii[None, :], ii[:, None] <= ii[None, :]]).astype(BF16)
    top = 31 - lax.clz(ii[:, None] ^ ii[None, :])
    lvl = jnp.stack([jnp.where(ii[:, None] > ii[None, :], top, -1), jnp.where(ii[:, None] < ii[None, :], top, -1)])
    nlev = HGRN_CHUNK.bit_length() - 1
    later = ((ii[None, :, None] >> jnp.arange(nlev, dtype=jnp.int32)[:, None, None]) & 1) == 1
    sgn_f = jnp.broadcast_to(jnp.where(later, 1.0, -1.0).astype(F32), (nlev, HGRN_CHUNK, A_DK))
    sgn = jnp.stack([sgn_f, -sgn_f])

    def col(off, rev):
        if rev:
            return pl.BlockSpec((1, t, w), lambda b, i: (b, n - 1 - i, off))
        return pl.BlockSpec((1, t, w), lambda b, i: (b, i, off))

    st_spec = pl.BlockSpec((1, nh, 2, A_DK, A_DK), lambda b, i: (b, 0, 0, 0, 0))
    sq = pl.BlockSpec((2, HGRN_CHUNK, HGRN_CHUNK), lambda b, i: (0, 0, 0))
    return pl.pallas_call(
        functools.partial(_hgrn_body, nc=t // HGRN_CHUNK, nh=nh),
        out_shape=[jax.ShapeDtypeStruct((bsz, length, w), F32),
                   jax.ShapeDtypeStruct((bsz, length, w), F32),
                   jax.ShapeDtypeStruct((bsz, nh, 2, A_DK, A_DK), F32)],
        grid=(bsz, n),
        in_specs=[col(0, False), col(1, False), col(3, False),
                  col(0, True), col(1, True), col(4, True),
                  pl.BlockSpec((2, w), lambda b, i: (0, 0)),
                  st_spec, sq, sq,
                  pl.BlockSpec((2, nlev, HGRN_CHUNK, A_DK), lambda b, i: (0, 0, 0, 0))],
        out_specs=[col(0, False), col(0, True), st_spec],
        scratch_shapes=[pltpu.VMEM((nh, 2, A_DK, A_DK), F32), pltpu.VMEM((nh, 2, HGRN_CHUNK, A_DK), F32)],
        compiler_params=_cparams(("parallel", "arbitrary")),
        name="hgrn",
    )(pa3, pa3, pa3, pa3, pa3, pa3, lb, s0, tri, lvl, sgn)


def _shortconv_body(x_ref, prev_ref, next_ref, w_ref, b_ref, o0_ref, o1_ref, o2_ref, *, tl, cw):
    i = pl.program_id(1)
    x = x_ref[0]
    row = lax.broadcasted_iota(jnp.int32, (tl, 1), 0)
    prev_row = jnp.where(i > 0, prev_ref[0, SUBLANES - 1:SUBLANES, :], 0.0)
    next_row = jnp.where(i < pl.num_programs(1) - 1, next_ref[0, 0:1, :], 0.0)
    xm = jnp.where(row == 0, prev_row, pltpu.roll(x, 1, 0))
    xp = jnp.where(row == tl - 1, next_row, pltpu.roll(x, tl - 1, 0))
    y = xm * w_ref[0:1, :] + x * w_ref[1:2, :] + xp * w_ref[2:3, :] + b_ref[...]
    o0_ref[0] = y[:, 0:cw]
    o1_ref[0] = y[:, cw:2 * cw]
    o2_ref[0] = y[:, 2 * cw:3 * cw]


def _shortconv(pb3, w, b, tl):
    bsz, length, wd = pb3.shape
    cw = wd // 3
    n8 = length // SUBLANES
    r8 = tl // SUBLANES
    body = functools.partial(_shortconv_body, tl=tl, cw=cw)
    out = jax.ShapeDtypeStruct((bsz, length, cw), F32)
    return pl.pallas_call(
        body,
        out_shape=[out, out, out],
        grid=(bsz, length // tl),
        in_specs=[pl.BlockSpec((1, tl, wd), lambda bb, i: (bb, i, 0)),
                  pl.BlockSpec((1, SUBLANES, wd), lambda bb, i: (bb, jnp.maximum(i * r8 - 1, 0), 0)),
                  pl.BlockSpec((1, SUBLANES, wd), lambda bb, i: (bb, jnp.minimum((i + 1) * r8, n8 - 1), 0)),
                  pl.BlockSpec((3, wd), lambda bb, i: (0, 0)),
                  pl.BlockSpec((1, wd), lambda bb, i: (0, 0))],
        out_specs=[pl.BlockSpec((1, tl, cw), lambda bb, i: (bb, i, 0))] * 3,
        compiler_params=_cparams(("parallel", "parallel")),
        name="short_conv",
    )(pb3, pb3, pb3, w, b.reshape(1, wd))


def _hyena_feats(length):
    n = jnp.arange(2 * length)
    pos = jnp.where(n < length, n, 2 * length - n).astype(F32)
    t = pos / (length - 1)
    w = 2.0 * math.pi * pos / length
    bands = jnp.linspace(1e-4, B_BANDS - 1, B_BANDS, dtype=F32)
    feats = jnp.concatenate([t[None, :], jnp.cos(bands[:, None] * w[None, :]), -jnp.sin(bands[:, None] * w[None, :])],
                            axis=0)
    return jnp.pad(feats, ((0, B_FFN - B_EMB), (0, 0))), t[:, None]


def _decay_rates(width):
    max_decay = math.log(B_TARGET) / B_FAST_DECAY
    min_decay = math.log(B_TARGET) / B_SLOW_DECAY
    return jnp.abs(jnp.linspace(min_decay, max_decay, width, dtype=F32)).reshape(1, width)


def _filter_body(f_ref, t_ref, w1_ref, b1_ref, f1_ref, w2_ref, b2_ref, f2_ref, w3_ref, rate_ref, o_ref, l1_ref,
                 *, tr, length):
    i = pl.program_id(0)
    h = jnp.sin(f1_ref[...] * (_hdot(w1_ref[...], f_ref[...]) + b1_ref[...]))
    h = jnp.sin(f2_ref[...] * (_hdot(w2_ref[...], h) + b2_ref[...]))
    h = jnp.dot(h.T.astype(BF16), w3_ref[...], preferred_element_type=F32)
    dec = jnp.exp(-(t_ref[...] * rate_ref[...]))
    h = h * jnp.concatenate([dec] * B_ORDER, axis=1)
    row = i * tr + lax.broadcasted_iota(jnp.int32, (tr, 1), 0)
    h = jnp.where(row == length, 0.0, h)
    o_ref[...] = h

    @pl.when(i == 0)
    def _():
        l1_ref[...] = jnp.zeros_like(l1_ref)

    l1_ref[...] += jnp.sum(jnp.abs(h), axis=0, keepdims=True)


def _hyena_filter(feats_t, tcol, w1, b1, f1, w2, b2, f2, w3, rates, length):
    width = rates.shape[1]
    ow = B_ORDER * width
    tr = min(512, length)
    nblk = 2 * length // tr
    w1t = jnp.pad(w1, ((0, B_FFN - B_EMB), (0, 0))).T
    body = functools.partial(_filter_body, tr=tr, length=length)
    small = lambda shape: pl.BlockSpec(shape, lambda i: (0, 0))
    col = lambda v: v.reshape(-1, 1)
    return pl.pallas_call(
        body,
        out_shape=[jax.ShapeDtypeStruct((2 * length, ow), F32), jax.ShapeDtypeStruct((1, ow), F32)],
        grid=(nblk,),
        in_specs=[pl.BlockSpec((B_FFN, tr), lambda i: (0, i)),
                  pl.BlockSpec((tr, 1), lambda i: (i, 0)),
                  small((B_FFN, B_FFN)), small((B_FFN, 1)), small((B_FFN, 1)),
                  small((B_FFN, B_FFN)), small((B_FFN, 1)), small((B_FFN, 1)),
                  pl.BlockSpec((B_FFN, ow), lambda i: (0, (2 * i) // nblk)),
                  small((1, width))],
        out_specs=[pl.BlockSpec((tr, ow), lambda i: (i, 0)), pl.BlockSpec((1, ow), lambda i: (0, 0))],
        compiler_params=_cparams(("arbitrary",)),
        name="hyena_filter",
    )(feats_t, tcol, w1t, col(b1), col(f1), w2.T, col(b2), col(f2), w3.astype(BF16), rates)


def _dft_tables(length):
    n = 2 * length
    n2 = DFT_N2
    n1 = n // n2
    n1h = n1 // 2
    k1 = jnp.arange(n1, dtype=jnp.int32)[None, :, None]
    j1 = jnp.arange(n1, dtype=jnp.int32)[None, None, :]
    j2 = jnp.arange(n2, dtype=jnp.int32)[:, None, None]
    ang = ((k1 * (n2 * j1 + j2)) % n).astype(F32) * (-2.0 * math.pi / n)
    er, ei = jnp.cos(ang), jnp.sin(ang)
    a_f = jnp.concatenate([er, ei], axis=1)
    erh, eih = er[:, :, :n1h], ei[:, :, :n1h]
    a_z = jnp.concatenate([jnp.concatenate([erh, -eih], axis=2),
                           jnp.concatenate([eih, erh], axis=2)], axis=1)
    a_inv = jnp.swapaxes(a_z, 1, 2) / n
    kk = jnp.arange(n2, dtype=jnp.int32)
    ang2 = ((kk[:, None] * kk[None, :]) % n2).astype(F32) * (-2.0 * math.pi / n2)
    fr, fi = jnp.cos(ang2), jnp.sin(ang2)
    f2 = jnp.concatenate([jnp.concatenate([fr, -fi], axis=1), jnp.concatenate([fi, fr], axis=1)], axis=0)
    return a_f.astype(BF16), a_z.astype(BF16), a_inv.astype(BF16), f2.astype(BF16), f2.T.astype(BF16)


DFT_BW = 512


def _to_slabs(x_ref, slab_ref, scale=None):
    rows, nb, w = x_ref.shape
    for lt in range(w // LANES):
        x = x_ref[:, :, lt * LANES:(lt + 1) * LANES]
        if scale is not None:
            x = x * scale[:, lt * LANES:(lt + 1) * LANES]
        slab_ref[lt] = x.reshape(rows * nb, LANES)


def _slab_rows(slab_ref, j, rows, nb):
    return jnp.concatenate([slab_ref[lt, pl.ds(j, rows, stride=nb), :] for lt in range(slab_ref.shape[0])], axis=1)


def _dfta_fwd_body(x_ref, a_ref, s_ref, o_ref, xs_ref, ys_ref):
    rows, nb, w = x_ref.shape
    m = o_ref.shape[0]
    _to_slabs(x_ref, xs_ref, s_ref[...])
    for j in range(nb):
        y = jnp.dot(a_ref[j], _slab_rows(xs_ref, j, rows, nb).astype(BF16), preferred_element_type=F32)
        for lt in range(w // LANES):
            ys_ref[lt, pl.ds(j, m, stride=nb), :] = y[:, lt * LANES:(lt + 1) * LANES]
    for lt in range(w // LANES):
        o_ref[:, :, lt * LANES:(lt + 1) * LANES] = ys_ref[lt].reshape(m, nb, LANES)


def _dfta_fwd(x3, a, scale):
    rows, n2, cw = x3.shape
    m = a.shape[1]
    nb, bwid = SUBLANES, DFT_BW
    return pl.pallas_call(
        _dfta_fwd_body,
        out_shape=jax.ShapeDtypeStruct((m, n2, cw), F32),
        grid=(n2 // nb, cw // bwid),
        in_specs=[pl.BlockSpec((rows, nb, bwid), lambda i, c: (0, i, c)),
                  pl.BlockSpec((nb, m, rows), lambda i, c: (i, 0, 0)),
                  pl.BlockSpec((1, bwid), lambda i, c: (0, c))],
        out_specs=pl.BlockSpec((m, nb, bwid), lambda i, c: (0, i, c)),
        scratch_shapes=[pltpu.VMEM((bwid // LANES, rows * nb, LANES), F32),
                        pltpu.VMEM((bwid // LANES, m * nb, LANES), F32)],
        compiler_params=_cparams(("parallel", "parallel")),
        name="dft_stage_a",
    )(x3, a, scale)


def _dfta_inv_body(u_ref, a_ref, xn_ref, z_ref, bias_ref, o_ref, us_ref, ys_ref):
    k, nb, w = u_ref.shape
    m = o_ref.shape[0]
    _to_slabs(u_ref, us_ref)
    for j in range(nb):
        y = jnp.dot(a_ref[j], _slab_rows(us_ref, j, k, nb).astype(BF16), preferred_element_type=F32)
        for lt in range(w // LANES):
            ys_ref[lt, pl.ds(j, m, stride=nb), :] = y[:, lt * LANES:(lt + 1) * LANES]
    for lt in range(w // LANES):
        sl = slice(lt * LANES, (lt + 1) * LANES)
        y = ys_ref[lt].reshape(m, nb, LANES)
        o_ref[:, :, sl] = xn_ref[:, :, sl] * (y + bias_ref[:, sl] * z_ref[:, :, sl])


def _dfta_inv(u3, a_inv, xn3, z3, bias):
    n2, m, k = a_inv.shape
    cw = u3.shape[2]
    nb, bwid = SUBLANES, DFT_BW
    return pl.pallas_call(
        _dfta_inv_body,
        out_shape=jax.ShapeDtypeStruct((m, n2, cw), F32),
        grid=(n2 // nb, cw // bwid),
        in_specs=[pl.BlockSpec((k, nb, bwid), lambda i, c: (0, i, c)),
                  pl.BlockSpec((nb, m, k), lambda i, c: (i, 0, 0)),
                  pl.BlockSpec((m, nb, bwid), lambda i, c: (0, i, c)),
                  pl.BlockSpec((m, nb, bwid), lambda i, c: (0, i, c)),
                  pl.BlockSpec((1, bwid), lambda i, c: (0, c))],
        out_specs=pl.BlockSpec((m, nb, bwid), lambda i, c: (0, i, c)),
        scratch_shapes=[pltpu.VMEM((bwid // LANES, k * nb, LANES), F32),
                        pltpu.VMEM((bwid // LANES, m * nb, LANES), F32)],
        compiler_params=_cparams(("parallel", "parallel")),
        name="dft_stage_a_inv",
    )(u3, a_inv, xn3, z3, bias)


def _dftc_filter_body(t_ref, f_ref, o_ref, *, kb):
    for j in range(kb):
        t = t_ref[:, j]
        t2 = t.reshape(2 * DFT_N2, t.shape[-1])
        y = jnp.dot(f_ref[...], t2.astype(BF16), preferred_element_type=F32)
        o_ref[j] = y.reshape(2, DFT_N2, t.shape[-1])


def _dftc_filter(t4, f2, kb):
    _, n1, n2, ow = t4.shape
    body = functools.partial(_dftc_filter_body, kb=kb)
    return pl.pallas_call(
        body,
        out_shape=jax.ShapeDtypeStruct((n1, 2, n2, ow), F32),
        grid=(n1 // kb,),
        in_specs=[pl.BlockSpec((2, kb, n2, ow), lambda i: (0, i, 0, 0)),
                  pl.BlockSpec((2 * n2, 2 * n2), lambda i: (0, 0))],
        out_specs=pl.BlockSpec((kb, 2, n2, ow), lambda i: (i, 0, 0, 0)),
        compiler_params=_cparams(("parallel",)),
        name="dft_stage_c_filter",
    )(t4, f2)


def _dftc_mid_body(t_ref, h_ref, f_ref, fi_ref, o_ref, *, kb):
    cw = t_ref.shape[-1]
    for j in range(kb):
        t2 = t_ref[:, j].reshape(2 * DFT_N2, cw)
        y = jnp.dot(f_ref[...], t2.astype(BF16), preferred_element_type=F32)
        yr, yi = y[:DFT_N2], y[DFT_N2:]
        hr, hi = h_ref[j, 0], h_ref[j, 1]
        p = jnp.concatenate([yr * hr - yi * hi, yr * hi + yi * hr], axis=0)
        u = jnp.dot(fi_ref[...], p.astype(BF16), preferred_element_type=F32)
        o_ref[0, j] = u[:DFT_N2]
        o_ref[1, j] = u[DFT_N2:]


def _dftc_mid(t4, hspec, order, f2, f2inv, kb):
    _, n1, n2, cw = t4.shape
    body = functools.partial(_dftc_mid_body, kb=kb)
    return pl.pallas_call(
        body,
        out_shape=jax.ShapeDtypeStruct((2, n1, n2, cw), F32),
        grid=(n1 // kb,),
        in_specs=[pl.BlockSpec((2, kb, n2, cw), lambda i: (0, i, 0, 0)),
                  pl.BlockSpec((kb, 2, n2, cw), lambda i: (i, 0, 0, order)),
                  pl.BlockSpec((2 * n2, 2 * n2), lambda i: (0, 0)),
                  pl.BlockSpec((2 * n2, 2 * n2), lambda i: (0, 0))],
        out_specs=pl.BlockSpec((2, kb, n2, cw), lambda i: (0, i, 0, 0)),
        compiler_params=_cparams(("parallel",)),
        name="dft_stage_c_mid",
    )(t4, hspec, f2, f2inv)


def _hyena_long(parts, filt, l1, bias, tables):
    a_f, a_z, a_inv, f2, f2inv = tables
    bsz, length, cw = parts[0].shape
    assert bsz == 2, "the two batch rows ride as real and imaginary parts of one transform"
    n1 = 2 * length // DFT_N2
    ow = filt.shape[1]
    tf = _dfta_fwd(filt.reshape(n1, DFT_N2, ow), a_f, 1.0 / l1)
    hspec = _dftc_filter(tf.reshape(2, n1, DFT_N2, ow), f2, 2)
    ones = jnp.ones((1, cw), F32)
    z = parts[0].reshape(bsz * n1 // 2, DFT_N2, cw)
    for o in range(B_ORDER):
        t = _dfta_fwd(z, a_z, ones)
        u = _dftc_mid(t.reshape(2, n1, DFT_N2, cw), hspec, o, f2, f2inv, 4)
        z = _dfta_inv(u.reshape(2 * n1, DFT_N2, cw), a_inv, parts[o + 1].reshape(z.shape), z,
                      bias[o:o + 1])
    return z.reshape(bsz, length, cw)


def _ctx_conv_body(v_ref, x1_ref, x2_ref, filt_ref, il1_ref, bias_ref, ff_ref, fz_ref, fi_ref, o_ref, *, n, cw):
    hf = _hdot(ff_ref[...], filt_ref[...] * il1_ref[...])
    z = jnp.concatenate([v_ref[0], v_ref[1]], axis=0)
    for o, x_ref in enumerate((x1_ref, x2_ref)):
        zz = _hdot(fz_ref[...], z)
        zr, zi = zz[:n], zz[n:]
        hr, hi = hf[:n, o * cw:(o + 1) * cw], hf[n:, o * cw:(o + 1) * cw]
        y = _hdot(fi_ref[...], jnp.concatenate([zr * hr - zi * hi, zr * hi + zi * hr], axis=0))
        xn = jnp.concatenate([x_ref[0], x_ref[1]], axis=0)
        z = xn * (y + bias_ref[o:o + 1, :] * z)
    half = n // 2
    o_ref[0] = z[:half]
    o_ref[1] = z[half:]


def _hyena_ctx(parts, filt, l1, bias):
    bsz, length, cw = parts[0].shape
    assert bsz == 2
    n = 2 * length
    k = jnp.arange(n, dtype=jnp.int32)
    ang = ((k[:, None] * k[None, :]) % n).astype(F32) * (-2.0 * math.pi / n)
    cr, ci = jnp.cos(ang), jnp.sin(ang)
    ff = jnp.concatenate([cr, ci], axis=0)
    crh, cih = cr[:, :length], ci[:, :length]
    fz = jnp.concatenate([jnp.concatenate([crh, -cih], axis=1), jnp.concatenate([cih, crh], axis=1)], axis=0)
    fi = fz.T / n
    body = functools.partial(_ctx_conv_body, n=n, cw=cw)
    full = lambda a: pl.BlockSpec(a.shape, lambda i: (0,) * a.ndim)
    args = (parts[0], parts[1], parts[2], filt, 1.0 / l1, bias, ff, fz, fi)
    return pl.pallas_call(
        body,
        out_shape=jax.ShapeDtypeStruct((bsz, length, cw), F32),
        grid=(1,),
        in_specs=[full(a) for a in args],
        out_specs=pl.BlockSpec((bsz, length, cw), lambda i: (0, 0, 0)),
        compiler_params=_cparams(("arbitrary",)),
        name="hyena_ctx",
    )(*args)


def _rope_tables(length):
    n_rows = length // GRID_W
    row = jnp.repeat(jnp.arange(n_rows), GRID_W).astype(F32)
    col = jnp.tile(jnp.arange(GRID_W), n_rows).astype(F32)
    nf = C_HEAD_DIM // 4
    inv = ROPE_BASE ** (-jnp.arange(nf, dtype=F32) * 2.0 / (C_HEAD_DIM // 2))
    ar, ac = row[:, None] * inv, col[:, None] * inv
    cos_h = jnp.concatenate([jnp.cos(ar), jnp.cos(ar), jnp.cos(ac), jnp.cos(ac)], axis=-1)
    sin_h = jnp.concatenate([-jnp.sin(ar), jnp.sin(ar), -jnp.sin(ac), jnp.sin(ac)], axis=-1)
    reps = LANES // C_HEAD_DIM
    return jnp.tile(cos_h, (1, reps)), jnp.tile(sin_h, (1, reps))


def _softmax_heads(q_ref, keys, vals, sink_ref, masks, o_ref):
    rows = q_ref.shape[1]
    top = lax.broadcasted_iota(jnp.int32, (2 * rows, 1), 0) < rows
    for g in range(C_KV_HEADS):
        q = jnp.concatenate([q_ref[0, :, (2 * g) * LANES:(2 * g + 1) * LANES],
                             q_ref[0, :, (2 * g + 1) * LANES:(2 * g + 2) * LANES]], axis=0)
        acc = None
        for half in range(2):
            off = (2 * g + half) * LANES
            ss = []
            for k_ref, msk in zip(keys, masks):
                s = lax.dot_general(q, k_ref[0, :, off:off + LANES], (((1,), (1,)), ((), ())),
                                    preferred_element_type=F32)
                ss.append(s if msk is None else jnp.where(jnp.concatenate([msk, msk], axis=0), s, NEG_BIG))
            blocks = [s[:, j * LANES:(j + 1) * LANES] for s in ss for j in range(s.shape[1] // LANES)]
            mx = functools.reduce(jnp.maximum, blocks)
            sk = jnp.where(top, sink_ref[C_GROUP * g + half], sink_ref[C_GROUP * g + 2 + half])
            m = jnp.maximum(jnp.max(mx, axis=-1, keepdims=True), sk)
            ps = [jnp.exp(s - m) for s in ss]
            pblocks = [p[:, j * LANES:(j + 1) * LANES] for p in ps for j in range(p.shape[1] // LANES)]
            den = jnp.sum(functools.reduce(jnp.add, pblocks), axis=-1, keepdims=True) + jnp.exp(sk - m)
            o = None
            for p, v_ref in zip(ps, vals):
                t = jnp.dot(p.astype(BF16), v_ref[0, :, off:off + LANES], preferred_element_type=F32)
                o = t if o is None else o + t
            o = o * (1.0 / den)
            acc = o if acc is None else acc + o
        o_ref[0, :, (2 * g) * LANES:(2 * g + 1) * LANES] = acc[:rows]
        o_ref[0, :, (2 * g + 1) * LANES:(2 * g + 2) * LANES] = acc[rows:]


def _attn_body(sink_ref, q_ref, kp_ref, kc_ref, kn_ref, kx_ref, vp_ref, vc_ref, vn_ref, vx_ref, o_ref):
    n = pl.program_id(1)
    blk = q_ref.shape[1]
    ri = lax.broadcasted_iota(jnp.int32, (blk, blk), 0)
    ci = lax.broadcasted_iota(jnp.int32, (blk, blk), 1)
    mask_p = jnp.logical_and(ci >= ri, n > 0)
    mask_n = jnp.logical_and(ci <= ri, n < pl.num_programs(1) - 1)
    _softmax_heads(q_ref, (kp_ref, kc_ref, kn_ref, kx_ref), (vp_ref, vc_ref, vn_ref, vx_ref), sink_ref,
                   (mask_p, None, mask_n, None), o_ref)


def _attention(q3, k3, v3, kx3, vx3, sink):
    bsz, length, wq = q3.shape
    wk = k3.shape[2]
    lc = kx3.shape[1]
    nb = length // C_BLOCK
    cur = lambda w: pl.BlockSpec((1, C_BLOCK, w), lambda b, i: (b, i, 0))
    prv = lambda w: pl.BlockSpec((1, C_BLOCK, w), lambda b, i: (b, jnp.maximum(i - 1, 0), 0))
    nxt = lambda w: pl.BlockSpec((1, C_BLOCK, w), lambda b, i: (b, jnp.minimum(i + 1, nb - 1), 0))
    ctx = pl.BlockSpec((1, lc, wk), lambda b, i: (b, 0, 0))
    return pl.pallas_call(
        _attn_body,
        out_shape=jax.ShapeDtypeStruct((bsz, length, wq), F32),
        grid=(bsz, nb),
        in_specs=[pl.BlockSpec(memory_space=pltpu.SMEM), cur(wq), prv(wk), cur(wk), nxt(wk), ctx,
                  prv(wk), cur(wk), nxt(wk), ctx],
        out_specs=cur(wq),
        compiler_params=_cparams(("parallel", "parallel")),
        name="window_attn",
    )(sink, q3, k3, k3, k3, kx3, v3, v3, v3, vx3)


def _ctx_attn_body(sink_ref, q_ref, k_ref, v_ref, o_ref):
    _softmax_heads(q_ref, (k_ref,), (v_ref,), sink_ref, (None,), o_ref)


def _ctx_attention(q3, k3, v3, sink):
    bsz, lc, wq = q3.shape
    wk = k3.shape[2]
    spec = lambda w: pl.BlockSpec((1, lc, w), lambda b: (b, 0, 0))
    return pl.pallas_call(
        _ctx_attn_body,
        out_shape=jax.ShapeDtypeStruct((bsz, lc, wq), F32),
        grid=(bsz,),
        in_specs=[pl.BlockSpec(memory_space=pltpu.SMEM), spec(wq), spec(wk), spec(wk)],
        out_specs=spec(wq),
        compiler_params=_cparams(("parallel",)),
        name="ctx_attn",
    )(sink, q3, k3, v3)


def _merge_body(h_ref, mod_ref, of_ref, ob_ref, ga_ref, yb_ref, yc_ref, wg_ref, wbr_ref, wo_ref, nw_ref,
                g_ref, b_ref, o_ref, *, alpha):
    h = h_ref[...]
    d = h.shape[1]
    u = (h * (1.0 + mod_ref[0, 4:5, :]) + mod_ref[0, 3:4, :]).astype(BF16)
    o = of_ref[...] + ob_ref[...]
    heads = []
    for hh in range(o.shape[1] // A_DK):
        oh = o[:, hh * A_DK:(hh + 1) * A_DK]
        ms = jnp.mean(oh * oh, axis=-1, keepdims=True)
        heads.append(oh * lax.rsqrt(ms + RMS_EPS) * nw_ref[:, hh * A_DK:(hh + 1) * A_DK])
    ya = jnp.concatenate(heads, axis=1) * _silu(ga_ref[...])
    m = jnp.zeros_like(h)
    for n, y in enumerate((ya, yb_ref[...], yc_ref[...])):
        gate = jnp.dot(u, wg_ref[:, n * d:(n + 1) * d], preferred_element_type=F32)
        m = m + jax.nn.sigmoid(gate) * jnp.dot(y.astype(BF16), wbr_ref[n], preferred_element_type=F32)
    y = jnp.dot(m.astype(BF16), wo_ref[...], preferred_element_type=F32)
    r = alpha * h + mod_ref[0, 5:6, :] * y
    o_ref[...] = _layer_norm(r, g_ref[...], b_ref[...])


def _merge(h2, mod, of2, ob2, pa2, yb2, yc2, wg, wbr, wo, norm_w, g, b, rows_per_mod, alpha, tm):
    r, d = h2.shape
    bw = of2.shape[1]
    tpm = rows_per_mod // tm
    body = functools.partial(_merge_body, alpha=alpha)
    row = lambda w: pl.BlockSpec((tm, w), lambda i: (i, 0))
    return pl.pallas_call(
        body,
        out_shape=jax.ShapeDtypeStruct((r, d), F32),
        grid=(r // tm,),
        in_specs=[row(d), pl.BlockSpec((1, N_MOD, d), lambda i: (i // tpm, 0, 0)),
                  row(bw), row(bw), pl.BlockSpec((tm, bw), lambda i: (i, 2)), row(bw), row(bw),
                  _const_spec(wg.shape), _const_spec(wbr.shape), _const_spec(wo.shape),
                  _const_spec((1, bw)), _const_spec((1, d)), _const_spec((1, d))],
        out_specs=row(d),
        compiler_params=_cparams(("parallel",)),
        name="merge",
    )(h2, mod, of2, ob2, pa2, yb2, yc2, wg, wbr, wo, norm_w.reshape(1, bw), g.reshape(1, d), b.reshape(1, d))


def kernel(x, c, ctx, c_ctx, ada_w, ada_b, ln_g, ln_b, ffn_w_in, ffn_w_out, mix_w_in, hgrn_lb, hgrn_norm_w,
           hyena_conv_w, hyena_conv_b, hyena_w1, hyena_b1, hyena_f1, hyena_w2, hyena_b2, hyena_f2, hyena_w3,
           hyena_bias, attn_sink, branch_w, out_w):
    bsz, seq, d = x.shape
    lc = ctx.shape[1]
    depth = ada_w.shape[0]
    alpha = (2.0 * depth) ** 0.25
    bw = hgrn_lb.shape[2]
    wk = C_KV_HEADS * C_HEAD_DIM
    widths = (5 * bw, (B_ORDER + 1) * bw, bw, wk)
    off_g = widths[0] + widths[1] + widths[2] + 2 * wk
    tm = 512
    tmc = min(256, bsz * lc)

    s = jax.nn.softmax(hgrn_lb.astype(F32), axis=0)
    lower_bounds = jnp.cumsum(s, axis=0) - s[0:1]

    c8 = jnp.zeros((SUBLANES, d), F32).at[:bsz].set(c).at[bsz].set(c_ctx)
    mods = _ada_mod(c8, ada_w, ada_b).reshape(depth, SUBLANES, N_MOD, d)

    cos_t, sin_t = _rope_tables(seq)
    feats, tcol = _hyena_feats(seq)
    feats_c, tcol_c = _hyena_feats(lc)
    rates = _decay_rates(bw)
    tables = _dft_tables(seq)

    w_in_bf = ffn_w_in.astype(BF16)
    w_out_bf = ffn_w_out.astype(BF16)
    proj_bf = mix_w_in[:, :, :off_g].astype(BF16)
    gate_bf = mix_w_in[:, :, off_g:].astype(BF16)
    br_bf = branch_w.astype(BF16)
    out_bf = out_w.astype(BF16)

    h = x.reshape(bsz * seq, d)
    hc = ctx.reshape(bsz * lc, d)
    for l in range(depth):
        last = l == depth - 1
        mod = mods[l, :bsz]
        modc = mods[l, bsz:bsz + 1]
        w_proj = proj_bf[l]
        w_gate = gate_bf[l]
        ffn = lambda t, mm, m0, j, g, rpm, tt: _ffn(t, mm, m0, w_in_bf[l, j], w_out_bf[l, j], ln_g[l, g],
                                                     ln_b[l, g], rpm, alpha, tt)
        h = ffn(h, mod, 0, 0, 0, seq, tm)
        hc = ffn(hc, modc, 0, 0, 0, bsz * lc, tmc)

        pa, pb, pq, pk, pv = _inproj(h, mod, w_proj, cos_t, sin_t, seq, True, widths, 256)
        ca, cb, cq, ck, cv = _inproj(hc, modc, w_proj, cos_t, sin_t, bsz * lc, False, widths, tmc)

        s0 = jnp.zeros((bsz, A_HEADS, 2, A_DK, A_DK), F32)
        ocf, ocb, s_ctx = _hgrn(ca.reshape(bsz, lc, -1), lower_bounds[l], s0, lc)
        of, ob, _ = _hgrn(pa.reshape(bsz, seq, -1), lower_bounds[l], s_ctx, 512)

        parts = _shortconv(pb.reshape(bsz, seq, -1), hyena_conv_w[l], hyena_conv_b[l], 512)
        fargs = (hyena_w1[l], hyena_b1[l], hyena_f1[l], hyena_w2[l], hyena_b2[l], hyena_f2[l], hyena_w3[l], rates)
        filt, l1 = _hyena_filter(feats, tcol, *fargs, seq)
        yb = _hyena_long(parts, filt, l1, hyena_bias[l], tables)

        kx, vx = ck.reshape(bsz, lc, 4 * wk), cv.reshape(bsz, lc, 4 * wk)
        yc = _attention(pq.reshape(bsz, seq, bw), pk.reshape(bsz, seq, 4 * wk), pv.reshape(bsz, seq, 4 * wk),
                        kx, vx, attn_sink[l])

        merge = lambda t, mm, a1, a2, a3, a4, a5, rpm, tt: _merge(
            t, mm, a1, a2, a3, a4, a5, w_gate, br_bf[l], out_bf[l], hgrn_norm_w[l], ln_g[l, 1], ln_b[l, 1],
            rpm, alpha, tt)
        h = merge(h, mod, of.reshape(-1, bw), ob.reshape(-1, bw), pa, yb.reshape(-1, bw), yc.reshape(-1, bw),
                  seq, 256)
        h = ffn(h, mod, 6, 1, 2, seq, tm)
        if not last:
            cparts = _shortconv(cb.reshape(bsz, lc, -1), hyena_conv_w[l], hyena_conv_b[l], lc)
            cfilt, cl1 = _hyena_filter(feats_c, tcol_c, *fargs, lc)
            ycb = _hyena_ctx(cparts, cfilt, cl1, hyena_bias[l])
            ycc = _ctx_attention(cq.reshape(bsz, lc, bw), kx, vx, attn_sink[l])
            hc = merge(hc, modc, ocf.reshape(-1, bw), ocb.reshape(-1, bw), ca, ycb.reshape(-1, bw),
                       ycc.reshape(-1, bw), bsz * lc, tmc)
            hc = ffn(hc, modc, 6, 1, 2, bsz * lc, tmc)
    return h.reshape(bsz, seq, d)
```

```python
import functools
import math

import jax
import jax.numpy as jnp
from jax import lax
from jax.experimental import pallas as pl
from jax.experimental.pallas import tpu as pltpu

F32 = jnp.float32
BF16 = jnp.bfloat16
HI = lax.Precision.HIGHEST

LANES = 128
SUBLANES = 8
VMEM_LIMIT = 56 * 1024 * 1024

N_MOD = 9
A_HEADS = 4
A_DK = 128
B_ORDER = 2
B_EMB = 33
B_BANDS = 16
B_FFN = 64
B_FAST_DECAY = 0.3
B_SLOW_DECAY = 1.5
B_TARGET = 1e-2
C_HEAD_DIM = 64
C_HEADS = 8
C_KV_HEADS = 2
C_GROUP = 4
C_BLOCK = 128
GRID_W = 64
ROPE_BASE = 10000.0
LN_EPS = 1e-5
RMS_EPS = 1e-6
DFT_N2 = 256
NEG_BIG = -1e30


def _cparams(sem, vmem=VMEM_LIMIT):
    return pltpu.CompilerParams(dimension_semantics=sem, vmem_limit_bytes=vmem)


def _const_spec(shape):
    nd = len(shape)
    return pl.BlockSpec(shape, lambda *_: (0,) * nd, pipeline_mode=pl.Buffered(1))


def _bdot(a, b):
    return jnp.dot(a.astype(BF16), b.astype(BF16), preferred_element_type=F32)


def _bdot_nt(a, b):
    return lax.dot_general(a.astype(BF16), b.astype(BF16), (((1,), (1,)), ((), ())),
                           preferred_element_type=F32)


def _hdot(a, b):
    return jnp.dot(a, b, precision=HI, preferred_element_type=F32)


def _layer_norm(x, g, b):
    mu = jnp.mean(x, axis=-1, keepdims=True)
    xc = x - mu
    var = jnp.mean(xc * xc, axis=-1, keepdims=True)
    return xc * lax.rsqrt(var + LN_EPS) * g + b


def _silu(x):
    return x * jax.nn.sigmoid(x)


def _ada_body(c_ref, w_ref, b_ref, o_ref):
    o_ref[0] = _hdot(_silu(c_ref[...]), w_ref[0]) + b_ref[0]


def _ada_mod(c8, ada_w, ada_b):
    depth, d, nw = ada_w.shape
    tn = nw // 8
    return pl.pallas_call(
        _ada_body,
        out_shape=jax.ShapeDtypeStruct((depth, SUBLANES, nw), F32),
        grid=(depth, nw // tn),
        in_specs=[pl.BlockSpec((SUBLANES, d), lambda l, j: (0, 0)),
                  pl.BlockSpec((1, d, tn), lambda l, j: (l, 0, j)),
                  pl.BlockSpec((1, 1, tn), lambda l, j: (l, 0, j))],
        out_specs=pl.BlockSpec((1, SUBLANES, tn), lambda l, j: (l, 0, j)),
        compiler_params=_cparams(("parallel", "parallel")),
        name="ada_mod",
    )(c8, ada_w, ada_b.reshape(depth, 1, nw))


def _ffn_body(h_ref, mod_ref, win_ref, wout_ref, g_ref, b_ref, o_ref, *, m0, d_ff, fc, alpha):
    h = h_ref[...]
    u = (h * (1.0 + mod_ref[0, m0 + 1:m0 + 2, :]) + mod_ref[0, m0:m0 + 1, :]).astype(BF16)
    acc = jnp.zeros_like(h)
    for j in range(d_ff // fc):
        a = jnp.dot(u, win_ref[:, j * fc:(j + 1) * fc], preferred_element_type=F32)
        b = jnp.dot(u, win_ref[:, d_ff + j * fc:d_ff + (j + 1) * fc], preferred_element_type=F32)
        act = (_silu(a) * b).astype(BF16)
        acc = acc + jnp.dot(act, wout_ref[j * fc:(j + 1) * fc, :], preferred_element_type=F32)
    r = alpha * h + 0.5 * mod_ref[0, m0 + 2:m0 + 3, :] * acc
    o_ref[...] = _layer_norm(r, g_ref[...], b_ref[...])


def _ffn(h2, mod, m0, w_in, w_out, g, b, rows_per_mod, alpha, tm):
    r, d = h2.shape
    d_ff = w_out.shape[0]
    tpm = rows_per_mod // tm
    fc = 2 * LANES
    assert d_ff % fc == 0
    body = functools.partial(_ffn_body, m0=m0, d_ff=d_ff, fc=fc, alpha=alpha)
    return pl.pallas_call(
        body,
        out_shape=jax.ShapeDtypeStruct((r, d), F32),
        grid=(r // tm,),
        in_specs=[pl.BlockSpec((tm, d), lambda i: (i, 0)),
                  pl.BlockSpec((1, N_MOD, d), lambda i: (i // tpm, 0, 0)),
                  _const_spec(w_in.shape), _const_spec(w_out.shape),
                  _const_spec((1, d)), _const_spec((1, d))],
        out_specs=pl.BlockSpec((tm, d), lambda i: (i, 0)),
        compiler_params=_cparams(("parallel",)),
        name="ffn",
    )(h2, mod, w_in, w_out, g.reshape(1, d), b.reshape(1, d))


def _swap16(x):
    lane = lax.broadcasted_iota(jnp.int32, x.shape, 1)
    return jnp.where((lane % 32) < 16, pltpu.roll(x, LANES - 16, 1), pltpu.roll(x, 16, 1))


def _head_pair_variants(x):
    lane = lax.broadcasted_iota(jnp.int32, x.shape, 1)
    low = lane < C_HEAD_DIM
    xs = pltpu.roll(x, C_HEAD_DIM, 1)
    parts = (jnp.where(low, x, 0.0), jnp.where(low, 0.0, xs), jnp.where(low, xs, 0.0), jnp.where(low, 0.0, x))
    return jnp.concatenate(parts, axis=1).astype(BF16)


def _inproj_body(h_ref, hp_ref, hn_ref, mod_ref, w_ref, cw_ref, cb_ref, cos_ref, sin_ref,
                 pa_ref, b0_ref, b1_ref, b2_ref, pq_ref, pk_ref, pv_ref, *, wa, wb, wq, wk, rope, tps):
    assert wk == LANES
    i = pl.program_id(0)
    tm = h_ref.shape[0]
    scale1 = 1.0 + mod_ref[0, 4:5, :]
    shift = mod_ref[0, 3:4, :]
    u = (h_ref[...] * scale1 + shift).astype(BF16)
    pa_ref[...] = jnp.dot(u, w_ref[:, 0:wa], preferred_element_type=F32)
    wbm = w_ref[:, wa:wa + wb]
    pb = jnp.dot(u, wbm, preferred_element_type=F32)
    up = (hp_ref[...] * scale1 + shift).astype(BF16)
    un = (hn_ref[...] * scale1 + shift).astype(BF16)
    prev_row = jnp.dot(up, wbm, preferred_element_type=F32)[SUBLANES - 1:SUBLANES, :]
    next_row = jnp.dot(un, wbm, preferred_element_type=F32)[0:1, :]
    prev_row = jnp.where(i % tps > 0, prev_row, 0.0)
    next_row = jnp.where(i % tps < tps - 1, next_row, 0.0)
    row = lax.broadcasted_iota(jnp.int32, (tm, 1), 0)
    xm = jnp.where(row == 0, prev_row, pltpu.roll(pb, 1, 0))
    xp = jnp.where(row == tm - 1, next_row, pltpu.roll(pb, tm - 1, 0))
    y = xm * cw_ref[0:1, :] + pb * cw_ref[1:2, :] + xp * cw_ref[2:3, :] + cb_ref[...]
    cw = wb // 3
    b0_ref[...] = y[:, 0:cw]
    b1_ref[...] = y[:, cw:2 * cw]
    b2_ref[...] = y[:, 2 * cw:3 * cw]
    oq = wa + wb
    q = jnp.dot(u, w_ref[:, oq:oq + wq], preferred_element_type=F32)
    k = jnp.dot(u, w_ref[:, oq + wq:oq + wq + wk], preferred_element_type=F32)
    v = jnp.dot(u, w_ref[:, oq + wq + wk:oq + wq + 2 * wk], preferred_element_type=F32)
    scale = C_HEAD_DIM ** -0.5
    if rope:
        cs = cos_ref[...]
        sn = sin_ref[...]
        for j in range(wq // LANES):
            xq = q[:, j * LANES:(j + 1) * LANES]
            pq_ref[:, j * LANES:(j + 1) * LANES] = ((xq * cs + _swap16(xq) * sn) * scale).astype(BF16)
        k = k * cs + _swap16(k) * sn
    else:
        pq_ref[...] = (q * scale).astype(BF16)
    pk_ref[...] = _head_pair_variants(k)
    pv_ref[...] = _head_pair_variants(v)


def _inproj(h2, mod, w, conv_w, conv_b, cos_t, sin_t, rows_per_mod, rows_per_seq, rope, widths, tm):
    r, d = h2.shape
    wa, wb, wq, wk = widths
    tpm = rows_per_mod // tm
    tps = rows_per_seq // tm
    r8 = tm // SUBLANES
    n8 = r // SUBLANES
    body = functools.partial(_inproj_body, wa=wa, wb=wb, wq=wq, wk=wk, rope=rope, tps=tps)
    ow = (wa, wb // 3, wb // 3, wb // 3, wq, 4 * wk, 4 * wk)
    od = (F32, F32, F32, F32, BF16, BF16, BF16)
    return pl.pallas_call(
        body,
        out_shape=[jax.ShapeDtypeStruct((r, n), t) for n, t in zip(ow, od)],
        grid=(r // tm,),
        in_specs=[pl.BlockSpec((tm, d), lambda i: (i, 0)),
                  pl.BlockSpec((SUBLANES, d), lambda i: (jnp.maximum(i * r8 - 1, 0), 0)),
                  pl.BlockSpec((SUBLANES, d), lambda i: (jnp.minimum((i + 1) * r8, n8 - 1), 0)),
                  pl.BlockSpec((1, N_MOD, d), lambda i: (i // tpm, 0, 0)),
                  _const_spec(w.shape), _const_spec((3, wb)), _const_spec((1, wb)),
                  pl.BlockSpec((tm, LANES), lambda i: (i % tps, 0)),
                  pl.BlockSpec((tm, LANES), lambda i: (i % tps, 0))],
        out_specs=[pl.BlockSpec((tm, n), lambda i: (i, 0)) for n in ow],
        compiler_params=_cparams(("parallel",)),
        name="inproj",
    )(h2, h2, h2, mod, w, conv_w, conv_b.reshape(1, wb), cos_t, sin_t)


HGRN_CHUNK = 128


def _pivot_rows(b_ref, c, r0):
    t, w = b_ref.shape
    g = 2 * c
    if g >= SUBLANES:
        rows = [jnp.broadcast_to(b_ref[s + r0:s + r0 + 1, :], (g, w)) for s in range(0, t, g)]
        return rows[0] if len(rows) == 1 else jnp.concatenate(rows, axis=0)
    b = b_ref[...]
    row = lax.broadcasted_iota(jnp.int32, (t, 1), 0) % g
    out = b
    for m in range(g):
        if m != r0:
            out = jnp.where(row == m, pltpu.roll(b, (m - r0) % t, 0), out)
    return out


def _split3_bf16(x):
    c = 65537.0
    t = c * x
    hi = t - (t - x)
    rest = x - hi
    t = c * rest
    mid = t - (t - rest)
    lo = rest - mid
    return hi.astype(BF16), mid.astype(BF16), lo.astype(BF16)


def _hgrn_chunk(q, v, z, lb, tri, lvl, sgn_ref, st, b_ref, reverse):
    t = q.shape[0]
    e = jnp.exp(-jnp.abs(z))
    r = 1.0 / (1.0 + e)
    pos = z >= 0.0
    sig_p = jnp.where(pos, r, e * r)
    sig_n = jnp.where(pos, e * r, r)
    logf = jnp.log2(lb + (1.0 - lb) * sig_p)
    kk = (1.0 - lb) * sig_n
    bb = jnp.dot(tri, jnp.concatenate(_split3_bf16(logf), axis=1), preferred_element_type=F32)
    w = q.shape[1]
    b = bb[:, 0:w] + bb[:, w:2 * w] + bb[:, 2 * w:3 * w]
    b_ref[...] = b
    vb = v.astype(BF16)
    out = _bdot_nt(q * jnp.exp2(b), st)
    ri = lax.broadcasted_iota(jnp.int32, (t, t), 0)
    ci = lax.broadcasted_iota(jnp.int32, (t, t), 1)
    att = jnp.where(ri == ci, jnp.sum(q * kk, axis=-1, keepdims=True), 0.0)
    c = t // 2
    while c >= 1:
        level = c.bit_length() - 1
        piv = _pivot_rows(b_ref, c, c if reverse else c - 1)
        dec = jnp.exp2((b - piv) * sgn_ref[level])
        att = jnp.where(lvl == level, _bdot_nt(q * dec, kk * dec), att)
        c //= 2
    out = out + jnp.dot(att.astype(BF16), vb, preferred_element_type=F32)
    blast = b[0:1, :] if reverse else b[t - 1:t, :]
    ke = (kk * jnp.exp2(blast - b)).astype(BF16)
    return out, st * jnp.exp2(blast) + jnp.dot(vb.T, ke, preferred_element_type=F32)


def _hgrn_body(qf_ref, vf_ref, zf_ref, qb_ref, vb_ref, zb_ref, lb_ref, s0_ref, tri_ref, lvl_ref, sgn_ref,
               of_ref, ob_ref, sfin_ref, st_ref, b_ref, *, nc, nh):
    i = pl.program_id(1)

    @pl.when(i == 0)
    def _():
        st_ref[...] = s0_ref[0]

    def step(j, carry):
        fwd = pl.ds(pl.multiple_of(j * HGRN_CHUNK, HGRN_CHUNK), HGRN_CHUNK)
        bwd = pl.ds(pl.multiple_of((nc - 1 - j) * HGRN_CHUNK, HGRN_CHUNK), HGRN_CHUNK)
        streams = ((qf_ref, vf_ref, zf_ref, of_ref, fwd), (qb_ref, vb_ref, zb_ref, ob_ref, bwd))
        for h in range(nh):
            cols = slice(h * A_DK, (h + 1) * A_DK)
            for d, (q_r, v_r, z_r, o_r, rows) in enumerate(streams):
                out, st_new = _hgrn_chunk(q_r[0, rows, cols], v_r[0, rows, cols], z_r[0, rows, cols],
                                          lb_ref[d:d + 1, cols], tri_ref[d], lvl_ref[d], sgn_ref.at[d], st_ref[h, d],
                                          b_ref.at[h, d], reverse=(d == 1))
                o_r[0, rows, cols] = out
                st_ref[h, d] = st_new
        return carry

    lax.fori_loop(0, nc, step, 0)

    @pl.when(i == pl.num_programs(1) - 1)
    def _():
        sfin_ref[0] = st_ref[...]


def _hgrn(pa3, lb, s0, t):
    bsz, length, _ = pa3.shape
    w = lb.shape[1]
    nh = w // A_DK
    n = length // t
    ii = jnp.arange(HGRN_CHUNK, dtype=jnp.int32)
    tri = jnp.stack([ii[:, None] >= ii[None, :], ii[:, None] <= ii[None, :]]).astype(BF16)
    top = 31 - lax.clz(ii[:, None] ^ ii[None, :])
    lvl = jnp.stack([jnp.where(ii[:, None] > ii[None, :], top, -1), jnp.where(ii[:, None] < ii[None, :], top, -1)])
    nlev = HGRN_CHUNK.bit_length() - 1
    later = ((ii[None, :, None] >> jnp.arange(nlev, dtype=jnp.int32)[:, None, None]) & 1) == 1
    sgn_f = jnp.broadcast_to(jnp.where(later, 1.0, -1.0).astype(F32), (nlev, HGRN_CHUNK, A_DK))
    sgn = jnp.stack([sgn_f, -sgn_f])

    def col(off, rev):
        if rev:
            return pl.BlockSpec((1, t, w), lambda b, i: (b, n - 1 - i, off))
        return pl.BlockSpec((1, t, w), lambda b, i: (b, i, off))

    st_spec = pl.BlockSpec((1, nh, 2, A_DK, A_DK), lambda b, i: (b, 0, 0, 0, 0))
    sq = pl.BlockSpec((2, HGRN_CHUNK, HGRN_CHUNK), lambda b, i: (0, 0, 0))
    return pl.pallas_call(
        functools.partial(_hgrn_body, nc=t // HGRN_CHUNK, nh=nh),
        out_shape=[jax.ShapeDtypeStruct((bsz, length, w), F32),
                   jax.ShapeDtypeStruct((bsz, length, w), F32),
                   jax.ShapeDtypeStruct((bsz, nh, 2, A_DK, A_DK), F32)],
        grid=(bsz, n),
        in_specs=[col(0, False), col(1, False), col(3, False),
                  col(0, True), col(1, True), col(4, True),
                  pl.BlockSpec((2, w), lambda b, i: (0, 0)),
                  st_spec, sq, sq,
                  pl.BlockSpec((2, nlev, HGRN_CHUNK, A_DK), lambda b, i: (0, 0, 0, 0))],
        out_specs=[col(0, False), col(0, True), st_spec],
        scratch_shapes=[pltpu.VMEM((nh, 2, A_DK, A_DK), F32), pltpu.VMEM((nh, 2, HGRN_CHUNK, A_DK), F32)],
        compiler_params=_cparams(("parallel", "arbitrary")),
        name="hgrn",
    )(pa3, pa3, pa3, pa3, pa3, pa3, lb, s0, tri, lvl, sgn)


def _hyena_feats(length):
    n = jnp.arange(2 * length)
    pos = jnp.where(n < length, n, 2 * length - n).astype(F32)
    t = pos / (length - 1)
    w = 2.0 * math.pi * pos / length
    bands = jnp.linspace(1e-4, B_BANDS - 1, B_BANDS, dtype=F32)
    feats = jnp.concatenate([t[None, :], jnp.cos(bands[:, None] * w[None, :]), -jnp.sin(bands[:, None] * w[None, :])],
                            axis=0)
    return jnp.pad(feats, ((0, B_FFN - B_EMB), (0, 0))), t[:, None]


def _decay_rates(width):
    max_decay = math.log(B_TARGET) / B_FAST_DECAY
    min_decay = math.log(B_TARGET) / B_SLOW_DECAY
    return jnp.abs(jnp.linspace(min_decay, max_decay, width, dtype=F32)).reshape(1, width)


def _filter_body(f_ref, t_ref, w1_ref, b1_ref, f1_ref, w2_ref, b2_ref, f2_ref, w3_ref, rate_ref, o_ref, l1_ref,
                 *, tr, length):
    i = pl.program_id(0)
    h = jnp.sin(f1_ref[...] * (_hdot(w1_ref[...], f_ref[...]) + b1_ref[...]))
    h = jnp.sin(f2_ref[...] * (_hdot(w2_ref[...], h) + b2_ref[...]))
    h = jnp.dot(h.T.astype(BF16), w3_ref[...], preferred_element_type=F32)
    dec = jnp.exp(-(t_ref[...] * rate_ref[...]))
    h = h * jnp.concatenate([dec] * B_ORDER, axis=1)
    row = i * tr + lax.broadcasted_iota(jnp.int32, (tr, 1), 0)
    h = jnp.where(row == length, 0.0, h)
    o_ref[...] = h

    @pl.when(i == 0)
    def _():
        l1_ref[...] = jnp.zeros_like(l1_ref)

    l1_ref[...] += jnp.sum(jnp.abs(h), axis=0, keepdims=True)


def _hyena_filter(feats_t, tcol, w1, b1, f1, w2, b2, f2, w3, rates, length):
    width = rates.shape[1]
    ow = B_ORDER * width
    tr = min(512, length)
    nblk = 2 * length // tr
    w1t = jnp.pad(w1, ((0, B_FFN - B_EMB), (0, 0))).T
    body = functools.partial(_filter_body, tr=tr, length=length)
    small = lambda shape: pl.BlockSpec(shape, lambda i: (0, 0))
    col = lambda v: v.reshape(-1, 1)
    return pl.pallas_call(
        body,
        out_shape=[jax.ShapeDtypeStruct((2 * length, ow), F32), jax.ShapeDtypeStruct((1, ow), F32)],
        grid=(nblk,),
        in_specs=[pl.BlockSpec((B_FFN, tr), lambda i: (0, i)),
                  pl.BlockSpec((tr, 1), lambda i: (i, 0)),
                  small((B_FFN, B_FFN)), small((B_FFN, 1)), small((B_FFN, 1)),
                  small((B_FFN, B_FFN)), small((B_FFN, 1)), small((B_FFN, 1)),
                  pl.BlockSpec((B_FFN, ow), lambda i: (0, (2 * i) // nblk)),
                  small((1, width))],
        out_specs=[pl.BlockSpec((tr, ow), lambda i: (i, 0)), pl.BlockSpec((1, ow), lambda i: (0, 0))],
        compiler_params=_cparams(("arbitrary",)),
        name="hyena_filter",
    )(feats_t, tcol, w1t, col(b1), col(f1), w2.T, col(b2), col(f2), w3.astype(BF16), rates)


def _dft_tables(length):
    n = 2 * length
    n2 = DFT_N2
    n1 = n // n2
    n1h = n1 // 2
    k1 = jnp.arange(n1, dtype=jnp.int32)[None, :, None]
    j1 = jnp.arange(n1, dtype=jnp.int32)[None, None, :]
    j2 = jnp.arange(n2, dtype=jnp.int32)[:, None, None]
    ang = ((k1 * (n2 * j1 + j2)) % n).astype(F32) * (-2.0 * math.pi / n)
    er, ei = jnp.cos(ang), jnp.sin(ang)
    a_f = jnp.concatenate([er, ei], axis=1)
    erh, eih = er[:, :, :n1h], ei[:, :, :n1h]
    a_z = jnp.concatenate([jnp.concatenate([erh, -eih], axis=2),
                           jnp.concatenate([eih, erh], axis=2)], axis=1)
    a_inv = jnp.swapaxes(a_z, 1, 2) / n
    kk = jnp.arange(n2, dtype=jnp.int32)
    ang2 = ((kk[:, None] * kk[None, :]) % n2).astype(F32) * (-2.0 * math.pi / n2)
    fr, fi = jnp.cos(ang2), jnp.sin(ang2)
    f2 = jnp.concatenate([jnp.concatenate([fr, -fi], axis=1), jnp.concatenate([fi, fr], axis=1)], axis=0)
    return a_f.astype(BF16), a_z.astype(BF16), a_inv.astype(BF16), f2.astype(BF16), f2.T.astype(BF16)


DFT_BW = 512


def _to_slabs(x_ref, slab_ref, scale=None):
    rows, nb, w = x_ref.shape
    for lt in range(w // LANES):
        x = x_ref[:, :, lt * LANES:(lt + 1) * LANES]
        if scale is not None:
            x = x * scale[:, lt * LANES:(lt + 1) * LANES]
        slab_ref[lt] = x.reshape(rows * nb, LANES)


def _slab_rows(slab_ref, j, rows, nb):
    return jnp.concatenate([slab_ref[lt, pl.ds(j, rows, stride=nb), :] for lt in range(slab_ref.shape[0])], axis=1)


def _dfta_fwd_body(x_ref, a_ref, s_ref, o_ref, xs_ref, ys_ref):
    rows, nb, w = x_ref.shape
    m = o_ref.shape[0]
    _to_slabs(x_ref, xs_ref, s_ref[...])
    for j in range(nb):
        y = jnp.dot(a_ref[j], _slab_rows(xs_ref, j, rows, nb).astype(BF16), preferred_element_type=F32)
        for lt in range(w // LANES):
            ys_ref[lt, pl.ds(j, m, stride=nb), :] = y[:, lt * LANES:(lt + 1) * LANES]
    for lt in range(w // LANES):
        o_ref[:, :, lt * LANES:(lt + 1) * LANES] = ys_ref[lt].reshape(m, nb, LANES)


def _dfta_fwd(x3, a, scale):
    rows, n2, cw = x3.shape
    m = a.shape[1]
    nb, bwid = SUBLANES, DFT_BW
    return pl.pallas_call(
        _dfta_fwd_body,
        out_shape=jax.ShapeDtypeStruct((m, n2, cw), F32),
        grid=(n2 // nb, cw // bwid),
        in_specs=[pl.BlockSpec((rows, nb, bwid), lambda i, c: (0, i, c)),
                  pl.BlockSpec((nb, m, rows), lambda i, c: (i, 0, 0)),
                  pl.BlockSpec((1, bwid), lambda i, c: (0, c))],
        out_specs=pl.BlockSpec((m, nb, bwid), lambda i, c: (0, i, c)),
        scratch_shapes=[pltpu.VMEM((bwid // LANES, rows * nb, LANES), F32),
                        pltpu.VMEM((bwid // LANES, m * nb, LANES), F32)],
        compiler_params=_cparams(("parallel", "parallel")),
        name="dft_stage_a",
    )(x3, a, scale)


def _dfta_inv_body(u_ref, a_ref, xn_ref, z_ref, bias_ref, o_ref, us_ref, ys_ref):
    k, nb, w = u_ref.shape
    m = o_ref.shape[0]
    _to_slabs(u_ref, us_ref)
    for j in range(nb):
        y = jnp.dot(a_ref[j], _slab_rows(us_ref, j, k, nb).astype(BF16), preferred_element_type=F32)
        for lt in range(w // LANES):
            ys_ref[lt, pl.ds(j, m, stride=nb), :] = y[:, lt * LANES:(lt + 1) * LANES]
    for lt in range(w // LANES):
        sl = slice(lt * LANES, (lt + 1) * LANES)
        y = ys_ref[lt].reshape(m, nb, LANES)
        o_ref[:, :, sl] = xn_ref[:, :, sl] * (y + bias_ref[:, sl] * z_ref[:, :, sl])


def _dfta_inv(u3, a_inv, xn3, z3, bias):
    n2, m, k = a_inv.shape
    cw = u3.shape[2]
    nb, bwid = SUBLANES, DFT_BW
    return pl.pallas_call(
        _dfta_inv_body,
        out_shape=jax.ShapeDtypeStruct((m, n2, cw), F32),
        grid=(n2 // nb, cw // bwid),
        in_specs=[pl.BlockSpec((k, nb, bwid), lambda i, c: (0, i, c)),
                  pl.BlockSpec((nb, m, k), lambda i, c: (i, 0, 0)),
                  pl.BlockSpec((m, nb, bwid), lambda i, c: (0, i, c)),
                  pl.BlockSpec((m, nb, bwid), lambda i, c: (0, i, c)),
                  pl.BlockSpec((1, bwid), lambda i, c: (0, c))],
        out_specs=pl.BlockSpec((m, nb, bwid), lambda i, c: (0, i, c)),
        scratch_shapes=[pltpu.VMEM((bwid // LANES, k * nb, LANES), F32),
                        pltpu.VMEM((bwid // LANES, m * nb, LANES), F32)],
        compiler_params=_cparams(("parallel", "parallel")),
        name="dft_stage_a_inv",
    )(u3, a_inv, xn3, z3, bias)


def _dftc_filter_body(t_ref, f_ref, o_ref, *, kb):
    for j in range(kb):
        t = t_ref[:, j]
        t2 = t.reshape(2 * DFT_N2, t.shape[-1])
        y = jnp.dot(f_ref[...], t2.astype(BF16), preferred_element_type=F32)
        o_ref[j] = y.reshape(2, DFT_N2, t.shape[-1]).astype(BF16)


def _dftc_filter(t4, f2, kb):
    _, n1, n2, ow = t4.shape
    body = functools.partial(_dftc_filter_body, kb=kb)
    return pl.pallas_call(
        body,
        out_shape=jax.ShapeDtypeStruct((n1, 2, n2, ow), BF16),
        grid=(n1 // kb,),
        in_specs=[pl.BlockSpec((2, kb, n2, ow), lambda i: (0, i, 0, 0)),
                  pl.BlockSpec((2 * n2, 2 * n2), lambda i: (0, 0))],
        out_specs=pl.BlockSpec((kb, 2, n2, ow), lambda i: (i, 0, 0, 0)),
        compiler_params=_cparams(("parallel",)),
        name="dft_stage_c_filter",
    )(t4, f2)


def _dftc_mid_body(t_ref, h_ref, f_ref, fi_ref, o_ref, *, kb):
    cw = t_ref.shape[-1]
    for j in range(kb):
        t2 = t_ref[:, j].reshape(2 * DFT_N2, cw)
        y = jnp.dot(f_ref[...], t2.astype(BF16), preferred_element_type=F32)
        yr, yi = y[:DFT_N2], y[DFT_N2:]
        hr, hi = h_ref[j, 0].astype(F32), h_ref[j, 1].astype(F32)
        p = jnp.concatenate([yr * hr - yi * hi, yr * hi + yi * hr], axis=0)
        u = jnp.dot(fi_ref[...], p.astype(BF16), preferred_element_type=F32)
        o_ref[0, j] = u[:DFT_N2]
        o_ref[1, j] = u[DFT_N2:]


def _dftc_mid(t4, hspec, order, f2, f2inv, kb):
    _, n1, n2, cw = t4.shape
    body = functools.partial(_dftc_mid_body, kb=kb)
    return pl.pallas_call(
        body,
        out_shape=jax.ShapeDtypeStruct((2, n1, n2, cw), F32),
        grid=(n1 // kb,),
        in_specs=[pl.BlockSpec((2, kb, n2, cw), lambda i: (0, i, 0, 0)),
                  pl.BlockSpec((kb, 2, n2, cw), lambda i: (i, 0, 0, order)),
                  pl.BlockSpec((2 * n2, 2 * n2), lambda i: (0, 0)),
                  pl.BlockSpec((2 * n2, 2 * n2), lambda i: (0, 0))],
        out_specs=pl.BlockSpec((2, kb, n2, cw), lambda i: (0, i, 0, 0)),
        compiler_params=_cparams(("parallel",)),
        name="dft_stage_c_mid",
    )(t4, hspec, f2, f2inv)


def _hyena_long(parts, filt, l1, bias, tables):
    a_f, a_z, a_inv, f2, f2inv = tables
    bsz, length, cw = parts[0].shape
    assert bsz == 2, "the two batch rows ride as real and imaginary parts of one transform"
    n1 = 2 * length // DFT_N2
    ow = filt.shape[1]
    tf = _dfta_fwd(filt.reshape(n1, DFT_N2, ow), a_f, 1.0 / l1)
    hspec = _dftc_filter(tf.reshape(2, n1, DFT_N2, ow), f2, 2)
    ones = jnp.ones((1, cw), F32)
    z = parts[0].reshape(bsz * n1 // 2, DFT_N2, cw)
    for o in range(B_ORDER):
        t = _dfta_fwd(z, a_z, ones)
        u = _dftc_mid(t.reshape(2, n1, DFT_N2, cw), hspec, o, f2, f2inv, 4)
        z = _dfta_inv(u.reshape(2 * n1, DFT_N2, cw), a_inv, parts[o + 1].reshape(z.shape), z,
                      bias[o:o + 1])
    return z.reshape(bsz, length, cw)


def _ctx_conv_body(v_ref, x1_ref, x2_ref, filt_ref, il1_ref, bias_ref, ff_ref, fz_ref, fi_ref, o_ref, *, n, cw):
    hf = _hdot(ff_ref[...], filt_ref[...] * il1_ref[...])
    z = jnp.concatenate([v_ref[0], v_ref[1]], axis=0)
    for o, x_ref in enumerate((x1_ref, x2_ref)):
        zz = _hdot(fz_ref[...], z)
        zr, zi = zz[:n], zz[n:]
        hr, hi = hf[:n, o * cw:(o + 1) * cw], hf[n:, o * cw:(o + 1) * cw]
        y = _hdot(fi_ref[...], jnp.concatenate([zr * hr - zi * hi, zr * hi + zi * hr], axis=0))
        xn = jnp.concatenate([x_ref[0], x_ref[1]], axis=0)
        z = xn * (y + bias_ref[o:o + 1, :] * z)
    half = n // 2
    o_ref[0] = z[:half]
    o_ref[1] = z[half:]


def _hyena_ctx(parts, filt, l1, bias):
    bsz, length, cw = parts[0].shape
    assert bsz == 2
    n = 2 * length
    k = jnp.arange(n, dtype=jnp.int32)
    ang = ((k[:, None] * k[None, :]) % n).astype(F32) * (-2.0 * math.pi / n)
    cr, ci = jnp.cos(ang), jnp.sin(ang)
    ff = jnp.concatenate([cr, ci], axis=0)
    crh, cih = cr[:, :length], ci[:, :length]
    fz = jnp.concatenate([jnp.concatenate([crh, -cih], axis=1), jnp.concatenate([cih, crh], axis=1)], axis=0)
    fi = fz.T / n
    body = functools.partial(_ctx_conv_body, n=n, cw=cw)
    full = lambda a: pl.BlockSpec(a.shape, lambda i: (0,) * a.ndim)
    args = (parts[0], parts[1], parts[2], filt, 1.0 / l1, bias, ff, fz, fi)
    return pl.pallas_call(
        body,
        out_shape=jax.ShapeDtypeStruct((bsz, length, cw), F32),
        grid=(1,),
        in_specs=[full(a) for a in args],
        out_specs=pl.BlockSpec((bsz, length, cw), lambda i: (0, 0, 0)),
        compiler_params=_cparams(("arbitrary",)),
        name="hyena_ctx",
    )(*args)


def _rope_tables(length):
    n_rows = length // GRID_W
    row = jnp.repeat(jnp.arange(n_rows), GRID_W).astype(F32)
    col = jnp.tile(jnp.arange(GRID_W), n_rows).astype(F32)
    nf = C_HEAD_DIM // 4
    inv = ROPE_BASE ** (-jnp.arange(nf, dtype=F32) * 2.0 / (C_HEAD_DIM // 2))
    ar, ac = row[:, None] * inv, col[:, None] * inv
    cos_h = jnp.concatenate([jnp.cos(ar), jnp.cos(ar), jnp.cos(ac), jnp.cos(ac)], axis=-1)
    sin_h = jnp.concatenate([-jnp.sin(ar), jnp.sin(ar), -jnp.sin(ac), jnp.sin(ac)], axis=-1)
    reps = LANES // C_HEAD_DIM
    return jnp.tile(cos_h, (1, reps)), jnp.tile(sin_h, (1, reps))


def _softmax_heads(q_ref, keys, vals, sink_ref, masks, o_ref):
    rows = q_ref.shape[1]
    top = lax.broadcasted_iota(jnp.int32, (2 * rows, 1), 0) < rows
    for g in range(C_KV_HEADS):
        q = jnp.concatenate([q_ref[0, :, (2 * g) * LANES:(2 * g + 1) * LANES],
                             q_ref[0, :, (2 * g + 1) * LANES:(2 * g + 2) * LANES]], axis=0)
        acc = None
        for half in range(2):
            off = (2 * g + half) * LANES
            ss = []
            for k_ref, msk in zip(keys, masks):
                s = lax.dot_general(q, k_ref[0, :, off:off + LANES], (((1,), (1,)), ((), ())),
                                    preferred_element_type=F32)
                ss.append(s if msk is None else jnp.where(jnp.concatenate([msk, msk], axis=0), s, NEG_BIG))
            blocks = [s[:, j * LANES:(j + 1) * LANES] for s in ss for j in range(s.shape[1] // LANES)]
            mx = functools.reduce(jnp.maximum, blocks)
            sk = jnp.where(top, sink_ref[C_GROUP * g + half], sink_ref[C_GROUP * g + 2 + half])
            m = jnp.maximum(jnp.max(mx, axis=-1, keepdims=True), sk)
            ps = [jnp.exp(s - m) for s in ss]
            pblocks = [p[:, j * LANES:(j + 1) * LANES] for p in ps for j in range(p.shape[1] // LANES)]
            den = jnp.sum(functools.reduce(jnp.add, pblocks), axis=-1, keepdims=True) + jnp.exp(sk - m)
            o = None
            for p, v_ref in zip(ps, vals):
                t = jnp.dot(p.astype(BF16), v_ref[0, :, off:off + LANES], preferred_element_type=F32)
                o = t if o is None else o + t
            o = o * (1.0 / den)
            acc = o if acc is None else acc + o
        o_ref[0, :, (2 * g) * LANES:(2 * g + 1) * LANES] = acc[:rows]
        o_ref[0, :, (2 * g + 1) * LANES:(2 * g + 2) * LANES] = acc[rows:]


def _attn_body(sink_ref, q_ref, kp_ref, kc_ref, kn_ref, kx_ref, vp_ref, vc_ref, vn_ref, vx_ref, o_ref):
    n = pl.program_id(1)
    blk = q_ref.shape[1]
    ri = lax.broadcasted_iota(jnp.int32, (blk, blk), 0)
    ci = lax.broadcasted_iota(jnp.int32, (blk, blk), 1)
    mask_p = jnp.logical_and(ci >= ri, n > 0)
    mask_n = jnp.logical_and(ci <= ri, n < pl.num_programs(1) - 1)
    _softmax_heads(q_ref, (kp_ref, kc_ref, kn_ref, kx_ref), (vp_ref, vc_ref, vn_ref, vx_ref), sink_ref,
                   (mask_p, None, mask_n, None), o_ref)


def _attention(q3, k3, v3, kx3, vx3, sink):
    bsz, length, wq = q3.shape
    wk = k3.shape[2]
    lc = kx3.shape[1]
    nb = length // C_BLOCK
    cur = lambda w: pl.BlockSpec((1, C_BLOCK, w), lambda b, i: (b, i, 0))
    prv = lambda w: pl.BlockSpec((1, C_BLOCK, w), lambda b, i: (b, jnp.maximum(i - 1, 0), 0))
    nxt = lambda w: pl.BlockSpec((1, C_BLOCK, w), lambda b, i: (b, jnp.minimum(i + 1, nb - 1), 0))
    ctx = pl.BlockSpec((1, lc, wk), lambda b, i: (b, 0, 0))
    return pl.pallas_call(
        _attn_body,
        out_shape=jax.ShapeDtypeStruct((bsz, length, wq), F32),
        grid=(bsz, nb),
        in_specs=[pl.BlockSpec(memory_space=pltpu.SMEM), cur(wq), prv(wk), cur(wk), nxt(wk), ctx,
                  prv(wk), cur(wk), nxt(wk), ctx],
        out_specs=cur(wq),
        compiler_params=_cparams(("parallel", "parallel")),
        name="window_attn",
    )(sink, q3, k3, k3, k3, kx3, v3, v3, v3, vx3)


def _ctx_attn_body(sink_ref, q_ref, k_ref, v_ref, o_ref):
    _softmax_heads(q_ref, (k_ref,), (v_ref,), sink_ref, (None,), o_ref)


def _ctx_attention(q3, k3, v3, sink):
    bsz, lc, wq = q3.shape
    wk = k3.shape[2]
    spec = lambda w: pl.BlockSpec((1, lc, w), lambda b: (b, 0, 0))
    return pl.pallas_call(
        _ctx_attn_body,
        out_shape=jax.ShapeDtypeStruct((bsz, lc, wq), F32),
        grid=(bsz,),
        in_specs=[pl.BlockSpec(memory_space=pltpu.SMEM), spec(wq), spec(wk), spec(wk)],
        out_specs=spec(wq),
        compiler_params=_cparams(("parallel",)),
        name="ctx_attn",
    )(sink, q3, k3, v3)


def _merge_body(h_ref, mod_ref, of_ref, ob_ref, ga_ref, yb_ref, yc_ref, wg_ref, wbr_ref, wo_ref, nw_ref,
                g_ref, b_ref, o_ref, *, alpha):
    h = h_ref[...]
    d = h.shape[1]
    u = (h * (1.0 + mod_ref[0, 4:5, :]) + mod_ref[0, 3:4, :]).astype(BF16)
    o = of_ref[...] + ob_ref[...]
    heads = []
    for hh in range(o.shape[1] // A_DK):
        oh = o[:, hh * A_DK:(hh + 1) * A_DK]
        ms = jnp.mean(oh * oh, axis=-1, keepdims=True)
        heads.append(oh * lax.rsqrt(ms + RMS_EPS) * nw_ref[:, hh * A_DK:(hh + 1) * A_DK])
    ya = jnp.concatenate(heads, axis=1) * _silu(ga_ref[...])
    m = jnp.zeros_like(h)
    for n, y in enumerate((ya, yb_ref[...], yc_ref[...])):
        gate = jnp.dot(u, wg_ref[:, n * d:(n + 1) * d], preferred_element_type=F32)
        m = m + jax.nn.sigmoid(gate) * jnp.dot(y.astype(BF16), wbr_ref[n], preferred_element_type=F32)
    y = jnp.dot(m.astype(BF16), wo_ref[...], preferred_element_type=F32)
    r = alpha * h + mod_ref[0, 5:6, :] * y
    o_ref[...] = _layer_norm(r, g_ref[...], b_ref[...])


def _merge(h2, mod, of2, ob2, pa2, yb2, yc2, wg, wbr, wo, norm_w, g, b, rows_per_mod, alpha, tm):
    r, d = h2.shape
    bw = of2.shape[1]
    tpm = rows_per_mod // tm
    body = functools.partial(_merge_body, alpha=alpha)
    row = lambda w: pl.BlockSpec((tm, w), lambda i: (i, 0))
    return pl.pallas_call(
        body,
        out_shape=jax.ShapeDtypeStruct((r, d), F32),
        grid=(r // tm,),
        in_specs=[row(d), pl.BlockSpec((1, N_MOD, d), lambda i: (i // tpm, 0, 0)),
                  row(bw), row(bw), pl.BlockSpec((tm, bw), lambda i: (i, 2)), row(bw), row(bw),
                  _const_spec(wg.shape), _const_spec(wbr.shape), _const_spec(wo.shape),
                  _const_spec((1, bw)), _const_spec((1, d)), _const_spec((1, d))],
        out_specs=row(d),
        compiler_params=_cparams(("parallel",)),
        name="merge",
    )(h2, mod, of2, ob2, pa2, yb2, yc2, wg, wbr, wo, norm_w.reshape(1, bw), g.reshape(1, d), b.reshape(1, d))


def kernel(x, c, ctx, c_ctx, ada_w, ada_b, ln_g, ln_b, ffn_w_in, ffn_w_out, mix_w_in, hgrn_lb, hgrn_norm_w,
           hyena_conv_w, hyena_conv_b, hyena_w1, hyena_b1, hyena_f1, hyena_w2, hyena_b2, hyena_f2, hyena_w3,
           hyena_bias, attn_sink, branch_w, out_w):
    bsz, seq, d = x.shape
    lc = ctx.shape[1]
    depth = ada_w.shape[0]
    alpha = (2.0 * depth) ** 0.25
    bw = hgrn_lb.shape[2]
    wk = C_KV_HEADS * C_HEAD_DIM
    widths = (5 * bw, (B_ORDER + 1) * bw, bw, wk)
    off_g = widths[0] + widths[1] + widths[2] + 2 * wk
    tm = 1024
    tmp = 512
    tmc = min(256, lc)

    s = jax.nn.softmax(hgrn_lb.astype(F32), axis=0)
    lower_bounds = jnp.cumsum(s, axis=0) - s[0:1]

    c8 = jnp.zeros((SUBLANES, d), F32).at[:bsz].set(c).at[bsz].set(c_ctx)
    mods = _ada_mod(c8, ada_w, ada_b).reshape(depth, SUBLANES, N_MOD, d)

    cos_t, sin_t = _rope_tables(seq)
    feats, tcol = _hyena_feats(seq)
    feats_c, tcol_c = _hyena_feats(lc)
    rates = _decay_rates(bw)
    tables = _dft_tables(seq)

    w_in_bf = ffn_w_in.astype(BF16)
    w_out_bf = ffn_w_out.astype(BF16)
    proj_bf = mix_w_in[:, :, :off_g].astype(BF16)
    gate_bf = mix_w_in[:, :, off_g:].astype(BF16)
    br_bf = branch_w.astype(BF16)
    out_bf = out_w.astype(BF16)

    h = x.reshape(bsz * seq, d)
    hc = ctx.reshape(bsz * lc, d)
    for l in range(depth):
        last = l == depth - 1
        mod = mods[l, :bsz]
        modc = mods[l, bsz:bsz + 1]
        w_proj = proj_bf[l]
        w_gate = gate_bf[l]
        ffn = lambda t, mm, m0, j, g, rpm, tt: _ffn(t, mm, m0, w_in_bf[l, j], w_out_bf[l, j], ln_g[l, g],
                                                     ln_b[l, g], rpm, alpha, tt)
        h = ffn(h, mod, 0, 0, 0, seq, tm)
        hc = ffn(hc, modc, 0, 0, 0, bsz * lc, tmc)

        conv = (hyena_conv_w[l], hyena_conv_b[l])
        pa, b0, b1, b2, pq, pk, pv = _inproj(h, mod, w_proj, *conv, cos_t, sin_t, seq, seq, True, widths, tmp)
        ca, c0, c1, c2, cq, ck, cv = _inproj(hc, modc, w_proj, *conv, cos_t, sin_t, bsz * lc, lc, False, widths,
                                             tmc)

        s0 = jnp.zeros((bsz, A_HEADS, 2, A_DK, A_DK), F32)
        ocf, ocb, s_ctx = _hgrn(ca.reshape(bsz, lc, -1), lower_bounds[l], s0, lc)
        of, ob, _ = _hgrn(pa.reshape(bsz, seq, -1), lower_bounds[l], s_ctx, 512)

        parts = [p.reshape(bsz, seq, bw) for p in (b0, b1, b2)]
        fargs = (hyena_w1[l], hyena_b1[l], hyena_f1[l], hyena_w2[l], hyena_b2[l], hyena_f2[l], hyena_w3[l], rates)
        filt, l1 = _hyena_filter(feats, tcol, *fargs, seq)
        yb = _hyena_long(parts, filt, l1, hyena_bias[l], tables)

        kx, vx = ck.reshape(bsz, lc, 4 * wk), cv.reshape(bsz, lc, 4 * wk)
        yc = _attention(pq.reshape(bsz, seq, bw), pk.reshape(bsz, seq, 4 * wk), pv.reshape(bsz, seq, 4 * wk),
                        kx, vx, attn_sink[l])

        merge = lambda t, mm, a1, a2, a3, a4, a5, rpm, tt: _merge(
            t, mm, a1, a2, a3, a4, a5, w_gate, br_bf[l], out_bf[l], hgrn_norm_w[l], ln_g[l, 1], ln_b[l, 1],
            rpm, alpha, tt)
        h = merge(h, mod, of.reshape(-1, bw), ob.reshape(-1, bw), pa, yb.reshape(-1, bw), yc.reshape(-1, bw),
                  seq, tmp)
        h = ffn(h, mod, 6, 1, 2, seq, tm)
        if not last:
            cparts = [p.reshape(bsz, lc, bw) for p in (c0, c1, c2)]
            cfilt, cl1 = _hyena_filter(feats_c, tcol_c, *fargs, lc)
            ycb = _hyena_ctx(cparts, cfilt, cl1, hyena_bias[l])
            ycc = _ctx_attention(cq.reshape(bsz, lc, bw), kx, vx, attn_sink[l])
            hc = merge(hc, modc, ocf.reshape(-1, bw), ocb.reshape(-1, bw), ca, ycb.reshape(-1, bw),
                       ycc.reshape(-1, bw), bsz * lc, tmc)
            hc = ffn(hc, modc, 6, 1, 2, bsz * lc, tmc)
    return h.reshape(bsz, seq, d)
```

```python
import functools
import math

import jax
import jax.numpy as jnp
from jax import lax
from jax.experimental import pallas as pl
from jax.experimental.pallas import tpu as pltpu

F32 = jnp.float32
BF16 = jnp.bfloat16
HI = lax.Precision.HIGHEST

LANES = 128
SUBLANES = 8
VMEM_LIMIT = 56 * 1024 * 1024

N_MOD = 9
A_HEADS = 4
A_DK = 128
B_ORDER = 2
B_EMB = 33
B_BANDS = 16
B_FFN = 64
B_FAST_DECAY = 0.3
B_SLOW_DECAY = 1.5
B_TARGET = 1e-2
C_HEAD_DIM = 64
C_HEADS = 8
C_KV_HEADS = 2
C_GROUP = 4
C_BLOCK = 128
GRID_W = 64
ROPE_BASE = 10000.0
LN_EPS = 1e-5
RMS_EPS = 1e-6
DFT_N2 = 256
NEG_BIG = -1e30


def _cparams(sem, vmem=VMEM_LIMIT):
    return pltpu.CompilerParams(dimension_semantics=sem, vmem_limit_bytes=vmem)


def _const_spec(shape):
    nd = len(shape)
    return pl.BlockSpec(shape, lambda *_: (0,) * nd, pipeline_mode=pl.Buffered(1))


def _bdot(a, b):
    return jnp.dot(a.astype(BF16), b.astype(BF16), preferred_element_type=F32)


def _bdot_nt(a, b):
    return lax.dot_general(a.astype(BF16), b.astype(BF16), (((1,), (1,)), ((), ())),
                           preferred_element_type=F32)


def _hdot(a, b):
    return jnp.dot(a, b, precision=HI, preferred_element_type=F32)


def _layer_norm(x, g, b):
    mu = jnp.mean(x, axis=-1, keepdims=True)
    xc = x - mu
    var = jnp.mean(xc * xc, axis=-1, keepdims=True)
    return xc * lax.rsqrt(var + LN_EPS) * g + b


def _silu(x):
    return x * jax.nn.sigmoid(x)


def _ada_body(c_ref, w_ref, b_ref, o_ref):
    o_ref[0] = _hdot(_silu(c_ref[...]), w_ref[0]) + b_ref[0]


def _ada_mod(c8, ada_w, ada_b):
    depth, d, nw = ada_w.shape
    tn = nw // 8
    return pl.pallas_call(
        _ada_body,
        out_shape=jax.ShapeDtypeStruct((depth, SUBLANES, nw), F32),
        grid=(depth, nw // tn),
        in_specs=[pl.BlockSpec((SUBLANES, d), lambda l, j: (0, 0)),
                  pl.BlockSpec((1, d, tn), lambda l, j: (l, 0, j)),
                  pl.BlockSpec((1, 1, tn), lambda l, j: (l, 0, j))],
        out_specs=pl.BlockSpec((1, SUBLANES, tn), lambda l, j: (l, 0, j)),
        compiler_params=_cparams(("parallel", "parallel")),
        name="ada_mod",
    )(c8, ada_w, ada_b.reshape(depth, 1, nw))


def _ffn_body(h_ref, mod_ref, win_ref, wout_ref, g_ref, b_ref, o_ref, *, m0, d_ff, fc, alpha):
    h = h_ref[...]
    u = (h * (1.0 + mod_ref[0, m0 + 1:m0 + 2, :]) + mod_ref[0, m0:m0 + 1, :]).astype(BF16)
    acc = jnp.zeros_like(h)
    for j in range(d_ff // fc):
        a = jnp.dot(u, win_ref[:, j * fc:(j + 1) * fc], preferred_element_type=F32)
        b = jnp.dot(u, win_ref[:, d_ff + j * fc:d_ff + (j + 1) * fc], preferred_element_type=F32)
        act = (_silu(a) * b).astype(BF16)
        acc = acc + jnp.dot(act, wout_ref[j * fc:(j + 1) * fc, :], preferred_element_type=F32)
    r = alpha * h + 0.5 * mod_ref[0, m0 + 2:m0 + 3, :] * acc
    o_ref[...] = _layer_norm(r, g_ref[...], b_ref[...])


def _ffn(h2, mod, m0, w_in, w_out, g, b, rows_per_mod, alpha, tm):
    r, d = h2.shape
    d_ff = w_out.shape[0]
    tpm = rows_per_mod // tm
    fc = 2 * LANES
    assert d_ff % fc == 0
    body = functools.partial(_ffn_body, m0=m0, d_ff=d_ff, fc=fc, alpha=alpha)
    return pl.pallas_call(
        body,
        out_shape=jax.ShapeDtypeStruct((r, d), F32),
        grid=(r // tm,),
        in_specs=[pl.BlockSpec((tm, d), lambda i: (i, 0)),
                  pl.BlockSpec((1, N_MOD, d), lambda i: (i // tpm, 0, 0)),
                  _const_spec(w_in.shape), _const_spec(w_out.shape),
                  _const_spec((1, d)), _const_spec((1, d))],
        out_specs=pl.BlockSpec((tm, d), lambda i: (i, 0)),
        compiler_params=_cparams(("parallel",)),
        name="ffn",
    )(h2, mod, w_in, w_out, g.reshape(1, d), b.reshape(1, d))


def _swap16(x):
    lane = lax.broadcasted_iota(jnp.int32, x.shape, 1)
    return jnp.where((lane % 32) < 16, pltpu.roll(x, LANES - 16, 1), pltpu.roll(x, 16, 1))


def _head_pair_variants(x):
    lane = lax.broadcasted_iota(jnp.int32, x.shape, 1)
    low = lane < C_HEAD_DIM
    xs = pltpu.roll(x, C_HEAD_DIM, 1)
    parts = (jnp.where(low, x, 0.0), jnp.where(low, 0.0, xs), jnp.where(low, xs, 0.0), jnp.where(low, 0.0, x))
    return jnp.concatenate(parts, axis=1).astype(BF16)


def _inproj_body(h_ref, hp_ref, hn_ref, mod_ref, w_ref, cw_ref, cb_ref, cos_ref, sin_ref,
                 pa_ref, b0_ref, b1_ref, b2_ref, pq_ref, pk_ref, pv_ref, *, wa, wb, wq, wk, rope, tps):
    assert wk == LANES
    i = pl.program_id(0)
    tm = h_ref.shape[0]
    scale1 = 1.0 + mod_ref[0, 4:5, :]
    shift = mod_ref[0, 3:4, :]
    u = (h_ref[...] * scale1 + shift).astype(BF16)
    wbm = w_ref[:, wa:wa + wb]
    pb = jnp.dot(u, wbm, preferred_element_type=F32)
    up = (hp_ref[...] * scale1 + shift).astype(BF16)
    un = (hn_ref[...] * scale1 + shift).astype(BF16)
    prev_row = jnp.dot(up, wbm, preferred_element_type=F32)[SUBLANES - 1:SUBLANES, :]
    next_row = jnp.dot(un, wbm, preferred_element_type=F32)[0:1, :]
    prev_row = jnp.where(i % tps > 0, prev_row, 0.0)
    next_row = jnp.where(i % tps < tps - 1, next_row, 0.0)
    oq = wa + wb
    q = jnp.dot(u, w_ref[:, oq:oq + wq], preferred_element_type=F32)
    k = jnp.dot(u, w_ref[:, oq + wq:oq + wq + wk], preferred_element_type=F32)
    v = jnp.dot(u, w_ref[:, oq + wq + wk:oq + wq + 2 * wk], preferred_element_type=F32)
    pa_ref[...] = jnp.dot(u, w_ref[:, 0:wa], preferred_element_type=F32)
    row = lax.broadcasted_iota(jnp.int32, (tm, 1), 0)
    xm = jnp.where(row == 0, prev_row, pltpu.roll(pb, 1, 0))
    xp = jnp.where(row == tm - 1, next_row, pltpu.roll(pb, tm - 1, 0))
    y = xm * cw_ref[0:1, :] + pb * cw_ref[1:2, :] + xp * cw_ref[2:3, :] + cb_ref[...]
    cw = wb // 3
    b0_ref[...] = y[:, 0:cw]
    b1_ref[...] = y[:, cw:2 * cw]
    b2_ref[...] = y[:, 2 * cw:3 * cw]
    scale = C_HEAD_DIM ** -0.5
    if rope:
        cs = cos_ref[...]
        sn = sin_ref[...]
        for j in range(wq // LANES):
            xq = q[:, j * LANES:(j + 1) * LANES]
            pq_ref[:, j * LANES:(j + 1) * LANES] = ((xq * cs + _swap16(xq) * sn) * scale).astype(BF16)
        k = k * cs + _swap16(k) * sn
    else:
        pq_ref[...] = (q * scale).astype(BF16)
    pk_ref[...] = _head_pair_variants(k)
    pv_ref[...] = _head_pair_variants(v)


def _inproj(h2, mod, w, conv_w, conv_b, cos_t, sin_t, rows_per_mod, rows_per_seq, rope, widths, tm):
    r, d = h2.shape
    wa, wb, wq, wk = widths
    tpm = rows_per_mod // tm
    tps = rows_per_seq // tm
    r8 = tm // SUBLANES
    n8 = r // SUBLANES
    body = functools.partial(_inproj_body, wa=wa, wb=wb, wq=wq, wk=wk, rope=rope, tps=tps)
    ow = (wa, wb // 3, wb // 3, wb // 3, wq, 4 * wk, 4 * wk)
    od = (F32, F32, F32, F32, BF16, BF16, BF16)
    return pl.pallas_call(
        body,
        out_shape=[jax.ShapeDtypeStruct((r, n), t) for n, t in zip(ow, od)],
        grid=(r // tm,),
        in_specs=[pl.BlockSpec((tm, d), lambda i: (i, 0)),
                  pl.BlockSpec((SUBLANES, d), lambda i: (jnp.maximum(i * r8 - 1, 0), 0)),
                  pl.BlockSpec((SUBLANES, d), lambda i: (jnp.minimum((i + 1) * r8, n8 - 1), 0)),
                  pl.BlockSpec((1, N_MOD, d), lambda i: (i // tpm, 0, 0)),
                  _const_spec(w.shape), _const_spec((3, wb)), _const_spec((1, wb)),
                  pl.BlockSpec((tm, LANES), lambda i: (i % tps, 0)),
                  pl.BlockSpec((tm, LANES), lambda i: (i % tps, 0))],
        out_specs=[pl.BlockSpec((tm, n), lambda i: (i, 0)) for n in ow],
        compiler_params=_cparams(("parallel",)),
        name="inproj",
    )(h2, h2, h2, mod, w, conv_w, conv_b.reshape(1, wb), cos_t, sin_t)


HGRN_CHUNK = 128


def _pivot_rows(b_ref, c, r0):
    t, w = b_ref.shape
    g = 2 * c
    if g >= SUBLANES:
        rows = [jnp.broadcast_to(b_ref[s + r0:s + r0 + 1, :], (g, w)) for s in range(0, t, g)]
        return rows[0] if len(rows) == 1 else jnp.concatenate(rows, axis=0)
    b = b_ref[...]
    row = lax.broadcasted_iota(jnp.int32, (t, 1), 0) % g
    out = b
    for m in range(g):
        if m != r0:
            out = jnp.where(row == m, pltpu.roll(b, (m - r0) % t, 0), out)
    return out


def _split3_bf16(x):
    c = 65537.0
    t = c * x
    hi = t - (t - x)
    rest = x - hi
    t = c * rest
    mid = t - (t - rest)
    lo = rest - mid
    return hi.astype(BF16), mid.astype(BF16), lo.astype(BF16)


def _hgrn_gates(z, lb, tri, b_ref, k_ref):
    e = jnp.exp(-jnp.abs(z))
    r = 1.0 / (1.0 + e)
    pos = z >= 0.0
    sig_p = jnp.where(pos, r, e * r)
    sig_n = jnp.where(pos, e * r, r)
    logf = jnp.log2(lb + (1.0 - lb) * sig_p)
    k_ref[...] = (1.0 - lb) * sig_n
    bb = jnp.dot(tri, jnp.concatenate(_split3_bf16(logf), axis=1), preferred_element_type=F32)
    w = z.shape[1]
    b_ref[...] = bb[:, 0:w] + bb[:, w:2 * w] + bb[:, 2 * w:3 * w]


def _hgrn_diag(q, k_ref, att_ref):
    t = q.shape[0]
    ri = lax.broadcasted_iota(jnp.int32, (t, t), 0)
    ci = lax.broadcasted_iota(jnp.int32, (t, t), 1)
    att_ref[...] = jnp.where(ri == ci, jnp.sum(q * k_ref[...], axis=-1, keepdims=True), 0.0)


def _hgrn_level(q, lvl, sgn_ref, b_ref, k_ref, att_ref, c, reverse):
    level = c.bit_length() - 1
    piv = _pivot_rows(b_ref, c, c if reverse else c - 1)
    dec = jnp.exp2((b_ref[...] - piv) * sgn_ref[level])
    att_ref[...] = jnp.where(lvl == level, _bdot_nt(q * dec, k_ref[...] * dec), att_ref[...])


def _hgrn_finish(q, v, st, b_ref, k_ref, att_ref, reverse):
    t = q.shape[0]
    b = b_ref[...]
    vb = v.astype(BF16)
    out = _bdot_nt(q * jnp.exp2(b), st) + jnp.dot(att_ref[...].astype(BF16), vb, preferred_element_type=F32)
    blast = b[0:1, :] if reverse else b[t - 1:t, :]
    ke = (k_ref[...] * jnp.exp2(blast - b)).astype(BF16)
    return out, st * jnp.exp2(blast) + jnp.dot(vb.T, ke, preferred_element_type=F32)


def _hgrn_body(qf_ref, vf_ref, zf_ref, qb_ref, vb_ref, zb_ref, lb_ref, s0_ref, tri_ref, lvl_ref, sgn_ref,
               of_ref, ob_ref, sfin_ref, st_ref, b_ref, k_ref, att_ref, *, nc, nh):
    i = pl.program_id(1)

    @pl.when(i == 0)
    def _():
        st_ref[...] = s0_ref[0]

    def step(j, carry):
        fwd = pl.ds(pl.multiple_of(j * HGRN_CHUNK, HGRN_CHUNK), HGRN_CHUNK)
        bwd = pl.ds(pl.multiple_of((nc - 1 - j) * HGRN_CHUNK, HGRN_CHUNK), HGRN_CHUNK)
        streams = ((qf_ref, vf_ref, zf_ref, of_ref, fwd), (qb_ref, vb_ref, zb_ref, ob_ref, bwd))
        chains = [(h, d) for h in range(nh) for d in range(2)]
        cols = lambda h: slice(h * A_DK, (h + 1) * A_DK)
        for h, d in chains:
            z_r, rows = streams[d][2], streams[d][4]
            _hgrn_gates(z_r[0, rows, cols(h)], lb_ref[d:d + 1, cols(h)], tri_ref[d], b_ref.at[h, d], k_ref.at[h, d])
        for h, d in chains:
            q_r, rows = streams[d][0], streams[d][4]
            _hgrn_diag(q_r[0, rows, cols(h)], k_ref.at[h, d], att_ref.at[h, d])
        c = HGRN_CHUNK // 2
        while c >= 1:
            for h, d in chains:
                q_r, rows = streams[d][0], streams[d][4]
                _hgrn_level(q_r[0, rows, cols(h)], lvl_ref[d], sgn_ref.at[d], b_ref.at[h, d], k_ref.at[h, d],
                            att_ref.at[h, d], c, reverse=(d == 1))
            c //= 2
        for h, d in chains:
            q_r, v_r, _, o_r, rows = streams[d]
            out, st_new = _hgrn_finish(q_r[0, rows, cols(h)], v_r[0, rows, cols(h)], st_ref[h, d], b_ref.at[h, d],
                                       k_ref.at[h, d], att_ref.at[h, d], reverse=(d == 1))
            o_r[0, rows, cols(h)] = out
            st_ref[h, d] = st_new
        return carry

    lax.fori_loop(0, nc, step, 0)

    @pl.when(i == pl.num_programs(1) - 1)
    def _():
        sfin_ref[0] = st_ref[...]


def _hgrn(pa3, lb, s0, t):
    bsz, length, _ = pa3.shape
    w = lb.shape[1]
    nh = w // A_DK
    n = length // t
    ii = jnp.arange(HGRN_CHUNK, dtype=jnp.int32)
    tri = jnp.stack([ii[:, None] >= ii[None, :], ii[:, None] <= ii[None, :]]).astype(BF16)
    top = 31 - lax.clz(ii[:, None] ^ ii[None, :])
    lvl = jnp.stack([jnp.where(ii[:, None] > ii[None, :], top, -1), jnp.where(ii[:, None] < ii[None, :], top, -1)])
    nlev = HGRN_CHUNK.bit_length() - 1
    later = ((ii[None, :, None] >> jnp.arange(nlev, dtype=jnp.int32)[:, None, None]) & 1) == 1
    sgn_f = jnp.broadcast_to(jnp.where(later, 1.0, -1.0).astype(F32), (nlev, HGRN_CHUNK, A_DK))
    sgn = jnp.stack([sgn_f, -sgn_f])

    def col(off, rev):
        if rev:
            return pl.BlockSpec((1, t, w), lambda b, i: (b, n - 1 - i, off))
        return pl.BlockSpec((1, t, w), lambda b, i: (b, i, off))

    st_spec = pl.BlockSpec((1, nh, 2, A_DK, A_DK), lambda b, i: (b, 0, 0, 0, 0))
    sq = pl.BlockSpec((2, HGRN_CHUNK, HGRN_CHUNK), lambda b, i: (0, 0, 0))
    return pl.pallas_call(
        functools.partial(_hgrn_body, nc=t // HGRN_CHUNK, nh=nh),
        out_shape=[jax.ShapeDtypeStruct((bsz, length, w), F32),
                   jax.ShapeDtypeStruct((bsz, length, w), F32),
                   jax.ShapeDtypeStruct((bsz, nh, 2, A_DK, A_DK), F32)],
        grid=(bsz, n),
        in_specs=[col(0, False), col(1, False), col(3, False),
                  col(0, True), col(1, True), col(4, True),
                  pl.BlockSpec((2, w), lambda b, i: (0, 0)),
                  st_spec, sq, sq,
                  pl.BlockSpec((2, nlev, HGRN_CHUNK, A_DK), lambda b, i: (0, 0, 0, 0))],
        out_specs=[col(0, False), col(0, True), st_spec],
        scratch_shapes=[pltpu.VMEM((nh, 2, A_DK, A_DK), F32),
                        pltpu.VMEM((nh, 2, HGRN_CHUNK, A_DK), F32),
                        pltpu.VMEM((nh, 2, HGRN_CHUNK, A_DK), F32),
                        pltpu.VMEM((nh, 2, HGRN_CHUNK, HGRN_CHUNK), F32)],
        compiler_params=_cparams(("parallel", "arbitrary")),
        name="hgrn",
    )(pa3, pa3, pa3, pa3, pa3, pa3, lb, s0, tri, lvl, sgn)


def _hyena_feats(length):
    n = jnp.arange(2 * length)
    pos = jnp.where(n < length, n, 2 * length - n).astype(F32)
    t = pos / (length - 1)
    w = 2.0 * math.pi * pos / length
    bands = jnp.linspace(1e-4, B_BANDS - 1, B_BANDS, dtype=F32)
    feats = jnp.concatenate([t[None, :], jnp.cos(bands[:, None] * w[None, :]), -jnp.sin(bands[:, None] * w[None, :])],
                            axis=0)
    return jnp.pad(feats, ((0, B_FFN - B_EMB), (0, 0))), t[:, None]


def _decay_rates(width):
    max_decay = math.log(B_TARGET) / B_FAST_DECAY
    min_decay = math.log(B_TARGET) / B_SLOW_DECAY
    return jnp.abs(jnp.linspace(min_decay, max_decay, width, dtype=F32)).reshape(1, width)


def _filter_body(f_ref, t_ref, w1_ref, b1_ref, f1_ref, w2_ref, b2_ref, f2_ref, w3_ref, rate_ref, o_ref, l1_ref,
                 *, tr, length):
    i = pl.program_id(0)
    h = jnp.sin(f1_ref[...] * (_hdot(w1_ref[...], f_ref[...]) + b1_ref[...]))
    h = jnp.sin(f2_ref[...] * (_hdot(w2_ref[...], h) + b2_ref[...]))
    h = jnp.dot(h.T.astype(BF16), w3_ref[...], preferred_element_type=F32)
    dec = jnp.exp(-(t_ref[...] * rate_ref[...]))
    h = h * jnp.concatenate([dec] * B_ORDER, axis=1)
    row = i * tr + lax.broadcasted_iota(jnp.int32, (tr, 1), 0)
    h = jnp.where(row == length, 0.0, h)
    o_ref[...] = h

    @pl.when(i == 0)
    def _():
        l1_ref[...] = jnp.zeros_like(l1_ref)

    l1_ref[...] += jnp.sum(jnp.abs(h), axis=0, keepdims=True)


def _hyena_filter(feats_t, tcol, w1, b1, f1, w2, b2, f2, w3, rates, length):
    width = rates.shape[1]
    ow = B_ORDER * width
    tr = min(512, length)
    nblk = 2 * length // tr
    w1t = jnp.pad(w1, ((0, B_FFN - B_EMB), (0, 0))).T
    body = functools.partial(_filter_body, tr=tr, length=length)
    small = lambda shape: pl.BlockSpec(shape, lambda i: (0, 0))
    col = lambda v: v.reshape(-1, 1)
    return pl.pallas_call(
        body,
        out_shape=[jax.ShapeDtypeStruct((2 * length, ow), F32), jax.ShapeDtypeStruct((1, ow), F32)],
        grid=(nblk,),
        in_specs=[pl.BlockSpec((B_FFN, tr), lambda i: (0, i)),
                  pl.BlockSpec((tr, 1), lambda i: (i, 0)),
                  small((B_FFN, B_FFN)), small((B_FFN, 1)), small((B_FFN, 1)),
                  small((B_FFN, B_FFN)), small((B_FFN, 1)), small((B_FFN, 1)),
                  pl.BlockSpec((B_FFN, ow), lambda i: (0, (2 * i) // nblk)),
                  small((1, width))],
        out_specs=[pl.BlockSpec((tr, ow), lambda i: (i, 0)), pl.BlockSpec((1, ow), lambda i: (0, 0))],
        compiler_params=_cparams(("arbitrary",)),
        name="hyena_filter",
    )(feats_t, tcol, w1t, col(b1), col(f1), w2.T, col(b2), col(f2), w3.astype(BF16), rates)


def _dft_tables(length):
    n = 2 * length
    n2 = DFT_N2
    n1 = n // n2
    n1h = n1 // 2
    k1 = jnp.arange(n1, dtype=jnp.int32)[None, :, None]
    j1 = jnp.arange(n1, dtype=jnp.int32)[None, None, :]
    j2 = jnp.arange(n2, dtype=jnp.int32)[:, None, None]
    ang = ((k1 * (n2 * j1 + j2)) % n).astype(F32) * (-2.0 * math.pi / n)
    er, ei = jnp.cos(ang), jnp.sin(ang)
    a_f = jnp.concatenate([er, ei], axis=1)
    erh, eih = er[:, :, :n1h], ei[:, :, :n1h]
    a_z = jnp.concatenate([jnp.concatenate([erh, -eih], axis=2),
                           jnp.concatenate([eih, erh], axis=2)], axis=1)
    a_inv = jnp.swapaxes(a_z, 1, 2) / n
    kk = jnp.arange(n2, dtype=jnp.int32)
    ang2 = ((kk[:, None] * kk[None, :]) % n2).astype(F32) * (-2.0 * math.pi / n2)
    fr, fi = jnp.cos(ang2), jnp.sin(ang2)
    f2 = jnp.concatenate([jnp.concatenate([fr, -fi], axis=1), jnp.concatenate([fi, fr], axis=1)], axis=0)
    return a_f.astype(BF16), a_z.astype(BF16), a_inv.astype(BF16), f2.astype(BF16), f2.T.astype(BF16)


DFT_BW = 512


def _to_slabs(x_ref, slab_ref, scale=None):
    rows, nb, w = x_ref.shape
    for lt in range(w // LANES):
        x = x_ref[:, :, lt * LANES:(lt + 1) * LANES]
        if scale is not None:
            x = x * scale[:, lt * LANES:(lt + 1) * LANES]
        slab_ref[lt] = x.reshape(rows * nb, LANES)


def _slab_rows(slab_ref, j, rows, nb):
    return jnp.concatenate([slab_ref[lt, pl.ds(j, rows, stride=nb), :] for lt in range(slab_ref.shape[0])], axis=1)


def _dfta_fwd_body(x_ref, a_ref, s_ref, o_ref, xs_ref, ys_ref):
    rows, nb, w = x_ref.shape
    m = o_ref.shape[0]
    _to_slabs(x_ref, xs_ref, s_ref[...])
    for j in range(nb):
        y = jnp.dot(a_ref[j], _slab_rows(xs_ref, j, rows, nb).astype(BF16), preferred_element_type=F32)
        for lt in range(w // LANES):
            ys_ref[lt, pl.ds(j, m, stride=nb), :] = y[:, lt * LANES:(lt + 1) * LANES]
    for lt in range(w // LANES):
        o_ref[:, :, lt * LANES:(lt + 1) * LANES] = ys_ref[lt].reshape(m, nb, LANES)


def _dfta_fwd(x3, a, scale):
    rows, n2, cw = x3.shape
    m = a.shape[1]
    nb, bwid = SUBLANES, DFT_BW
    return pl.pallas_call(
        _dfta_fwd_body,
        out_shape=jax.ShapeDtypeStruct((m, n2, cw), F32),
        grid=(n2 // nb, cw // bwid),
        in_specs=[pl.BlockSpec((rows, nb, bwid), lambda i, c: (0, i, c)),
                  pl.BlockSpec((nb, m, rows), lambda i, c: (i, 0, 0)),
                  pl.BlockSpec((1, bwid), lambda i, c: (0, c))],
        out_specs=pl.BlockSpec((m, nb, bwid), lambda i, c: (0, i, c)),
        scratch_shapes=[pltpu.VMEM((bwid // LANES, rows * nb, LANES), F32),
                        pltpu.VMEM((bwid // LANES, m * nb, LANES), F32)],
        compiler_params=_cparams(("parallel", "parallel")),
        name="dft_stage_a",
    )(x3, a, scale)


def _dfta_inv_body(u_ref, a_ref, xn_ref, z_ref, bias_ref, o_ref, us_ref, ys_ref):
    k, nb, w = u_ref.shape
    m = o_ref.shape[0]
    _to_slabs(u_ref, us_ref)
    for j in range(nb):
        y = jnp.dot(a_ref[j], _slab_rows(us_ref, j, k, nb).astype(BF16), preferred_element_type=F32)
        for lt in range(w // LANES):
            ys_ref[lt, pl.ds(j, m, stride=nb), :] = y[:, lt * LANES:(lt + 1) * LANES]
    for lt in range(w // LANES):
        sl = slice(lt * LANES, (lt + 1) * LANES)
        y = ys_ref[lt].reshape(m, nb, LANES)
        o_ref[:, :, sl] = xn_ref[:, :, sl] * (y + bias_ref[:, sl] * z_ref[:, :, sl])


def _dfta_inv(u3, a_inv, xn3, z3, bias):
    n2, m, k = a_inv.shape
    cw = u3.shape[2]
    nb, bwid = SUBLANES, DFT_BW
    return pl.pallas_call(
        _dfta_inv_body,
        out_shape=jax.ShapeDtypeStruct((m, n2, cw), F32),
        grid=(n2 // nb, cw // bwid),
        in_specs=[pl.BlockSpec((k, nb, bwid), lambda i, c: (0, i, c)),
                  pl.BlockSpec((nb, m, k), lambda i, c: (i, 0, 0)),
                  pl.BlockSpec((m, nb, bwid), lambda i, c: (0, i, c)),
                  pl.BlockSpec((m, nb, bwid), lambda i, c: (0, i, c)),
                  pl.BlockSpec((1, bwid), lambda i, c: (0, c))],
        out_specs=pl.BlockSpec((m, nb, bwid), lambda i, c: (0, i, c)),
        scratch_shapes=[pltpu.VMEM((bwid // LANES, k * nb, LANES), F32),
                        pltpu.VMEM((bwid // LANES, m * nb, LANES), F32)],
        compiler_params=_cparams(("parallel", "parallel")),
        name="dft_stage_a_inv",
    )(u3, a_inv, xn3, z3, bias)


def _dftc_filter_body(t_ref, f_ref, o_ref, *, kb):
    for j in range(kb):
        t = t_ref[:, j]
        t2 = t.reshape(2 * DFT_N2, t.shape[-1])
        y = jnp.dot(f_ref[...], t2.astype(BF16), preferred_element_type=F32)
        o_ref[j] = y.reshape(2, DFT_N2, t.shape[-1]).astype(BF16)


def _dftc_filter(t4, f2, kb):
    _, n1, n2, ow = t4.shape
    body = functools.partial(_dftc_filter_body, kb=kb)
    return pl.pallas_call(
        body,
        out_shape=jax.ShapeDtypeStruct((n1, 2, n2, ow), BF16),
        grid=(n1 // kb,),
        in_specs=[pl.BlockSpec((2, kb, n2, ow), lambda i: (0, i, 0, 0)),
                  pl.BlockSpec((2 * n2, 2 * n2), lambda i: (0, 0))],
        out_specs=pl.BlockSpec((kb, 2, n2, ow), lambda i: (i, 0, 0, 0)),
        compiler_params=_cparams(("parallel",)),
        name="dft_stage_c_filter",
    )(t4, f2)


def _dftc_mid_body(t_ref, h_ref, f_ref, fi_ref, o_ref, *, kb):
    cw = t_ref.shape[-1]
    ys = [jnp.dot(f_ref[...], t_ref[:, j].reshape(2 * DFT_N2, cw).astype(BF16), preferred_element_type=F32)
          for j in range(kb)]
    ps = []
    for j, y in enumerate(ys):
        yr, yi = y[:DFT_N2], y[DFT_N2:]
        hr, hi = h_ref[j, 0].astype(F32), h_ref[j, 1].astype(F32)
        ps.append(jnp.concatenate([yr * hr - yi * hi, yr * hi + yi * hr], axis=0).astype(BF16))
    for j, p in enumerate(ps):
        u = jnp.dot(fi_ref[...], p, preferred_element_type=F32)
        o_ref[0, j] = u[:DFT_N2]
        o_ref[1, j] = u[DFT_N2:]


def _dftc_mid(t4, hspec, order, f2, f2inv, kb):
    _, n1, n2, cw = t4.shape
    body = functools.partial(_dftc_mid_body, kb=kb)
    return pl.pallas_call(
        body,
        out_shape=jax.ShapeDtypeStruct((2, n1, n2, cw), F32),
        grid=(n1 // kb,),
        in_specs=[pl.BlockSpec((2, kb, n2, cw), lambda i: (0, i, 0, 0)),
                  pl.BlockSpec((kb, 2, n2, cw), lambda i: (i, 0, 0, order)),
                  pl.BlockSpec((2 * n2, 2 * n2), lambda i: (0, 0)),
                  pl.BlockSpec((2 * n2, 2 * n2), lambda i: (0, 0))],
        out_specs=pl.BlockSpec((2, kb, n2, cw), lambda i: (0, i, 0, 0)),
        compiler_params=_cparams(("parallel",)),
        name="dft_stage_c_mid",
    )(t4, hspec, f2, f2inv)


def _hyena_long(parts, filt, l1, bias, tables):
    a_f, a_z, a_inv, f2, f2inv = tables
    bsz, length, cw = parts[0].shape
    assert bsz == 2, "the two batch rows ride as real and imaginary parts of one transform"
    n1 = 2 * length // DFT_N2
    ow = filt.shape[1]
    tf = _dfta_fwd(filt.reshape(n1, DFT_N2, ow), a_f, 1.0 / l1)
    hspec = _dftc_filter(tf.reshape(2, n1, DFT_N2, ow), f2, 2)
    ones = jnp.ones((1, cw), F32)
    z = parts[0].reshape(bsz * n1 // 2, DFT_N2, cw)
    for o in range(B_ORDER):
        t = _dfta_fwd(z, a_z, ones)
        u = _dftc_mid(t.reshape(2, n1, DFT_N2, cw), hspec, o, f2, f2inv, 4)
        z = _dfta_inv(u.reshape(2 * n1, DFT_N2, cw), a_inv, parts[o + 1].reshape(z.shape), z,
                      bias[o:o + 1])
    return z.reshape(bsz, length, cw)


def _ctx_conv_body(v_ref, x1_ref, x2_ref, filt_ref, il1_ref, bias_ref, ff_ref, fz_ref, fi_ref, o_ref, *, n, cw):
    hf = _hdot(ff_ref[...], filt_ref[...] * il1_ref[...])
    z = jnp.concatenate([v_ref[0], v_ref[1]], axis=0)
    for o, x_ref in enumerate((x1_ref, x2_ref)):
        zz = _hdot(fz_ref[...], z)
        zr, zi = zz[:n], zz[n:]
        hr, hi = hf[:n, o * cw:(o + 1) * cw], hf[n:, o * cw:(o + 1) * cw]
        y = _hdot(fi_ref[...], jnp.concatenate([zr * hr - zi * hi, zr * hi + zi * hr], axis=0))
        xn = jnp.concatenate([x_ref[0], x_ref[1]], axis=0)
        z = xn * (y + bias_ref[o:o + 1, :] * z)
    half = n // 2
    o_ref[0] = z[:half]
    o_ref[1] = z[half:]


def _hyena_ctx(parts, filt, l1, bias):
    bsz, length, cw = parts[0].shape
    assert bsz == 2
    n = 2 * length
    k = jnp.arange(n, dtype=jnp.int32)
    ang = ((k[:, None] * k[None, :]) % n).astype(F32) * (-2.0 * math.pi / n)
    cr, ci = jnp.cos(ang), jnp.sin(ang)
    ff = jnp.concatenate([cr, ci], axis=0)
    crh, cih = cr[:, :length], ci[:, :length]
    fz = jnp.concatenate([jnp.concatenate([crh, -cih], axis=1), jnp.concatenate([cih, crh], axis=1)], axis=0)
    fi = fz.T / n
    body = functools.partial(_ctx_conv_body, n=n, cw=cw)
    full = lambda a: pl.BlockSpec(a.shape, lambda i: (0,) * a.ndim)
    args = (parts[0], parts[1], parts[2], filt, 1.0 / l1, bias, ff, fz, fi)
    return pl.pallas_call(
        body,
        out_shape=jax.ShapeDtypeStruct((bsz, length, cw), F32),
        grid=(1,),
        in_specs=[full(a) for a in args],
        out_specs=pl.BlockSpec((bsz, length, cw), lambda i: (0, 0, 0)),
        compiler_params=_cparams(("arbitrary",)),
        name="hyena_ctx",
    )(*args)


def _rope_tables(length):
    n_rows = length // GRID_W
    row = jnp.repeat(jnp.arange(n_rows), GRID_W).astype(F32)
    col = jnp.tile(jnp.arange(GRID_W), n_rows).astype(F32)
    nf = C_HEAD_DIM // 4
    inv = ROPE_BASE ** (-jnp.arange(nf, dtype=F32) * 2.0 / (C_HEAD_DIM // 2))
    ar, ac = row[:, None] * inv, col[:, None] * inv
    cos_h = jnp.concatenate([jnp.cos(ar), jnp.cos(ar), jnp.cos(ac), jnp.cos(ac)], axis=-1)
    sin_h = jnp.concatenate([-jnp.sin(ar), jnp.sin(ar), -jnp.sin(ac), jnp.sin(ac)], axis=-1)
    reps = LANES // C_HEAD_DIM
    return jnp.tile(cos_h, (1, reps)), jnp.tile(sin_h, (1, reps))


def _softmax_heads(q_ref, keys, vals, sink_ref, masks, o_ref):
    rows = q_ref.shape[1]
    top = lax.broadcasted_iota(jnp.int32, (2 * rows, 1), 0) < rows
    chains = [(g, half) for g in range(C_KV_HEADS) for half in range(2)]
    off = lambda g, half: (2 * g + half) * LANES
    lane_blocks = lambda xs: [x[:, j * LANES:(j + 1) * LANES] for x in xs for j in range(x.shape[1] // LANES)]
    qs = [jnp.concatenate([q_ref[0, :, (2 * g) * LANES:(2 * g + 1) * LANES],
                           q_ref[0, :, (2 * g + 1) * LANES:(2 * g + 2) * LANES]], axis=0)
          for g in range(C_KV_HEADS)]
    scores, shifts, sinks = {}, {}, {}
    for g, half in chains:
        ss = []
        for k_ref, msk in zip(keys, masks):
            s = lax.dot_general(qs[g], k_ref[0, :, off(g, half):off(g, half) + LANES], (((1,), (1,)), ((), ())),
                                preferred_element_type=F32)
            ss.append(s if msk is None else jnp.where(jnp.concatenate([msk, msk], axis=0), s, NEG_BIG))
        sk = jnp.where(top, sink_ref[C_GROUP * g + half], sink_ref[C_GROUP * g + 2 + half])
        mx = functools.reduce(jnp.maximum, lane_blocks(ss))
        scores[g, half], sinks[g, half] = ss, sk
        shifts[g, half] = jnp.maximum(jnp.max(mx, axis=-1, keepdims=True), sk)
    probs, rdens = {}, {}
    for c in chains:
        ps = [jnp.exp(s - shifts[c]) for s in scores[c]]
        den = jnp.sum(functools.reduce(jnp.add, lane_blocks(ps)), axis=-1, keepdims=True) + jnp.exp(sinks[c] - shifts[c])
        probs[c], rdens[c] = [p.astype(BF16) for p in ps], 1.0 / den
    outs = {}
    for g, half in chains:
        o = None
        for p, v_ref in zip(probs[g, half], vals):
            t = jnp.dot(p, v_ref[0, :, off(g, half):off(g, half) + LANES], preferred_element_type=F32)
            o = t if o is None else o + t
        outs[g, half] = o * rdens[g, half]
    for g in range(C_KV_HEADS):
        acc = outs[g, 0] + outs[g, 1]
        o_ref[0, :, (2 * g) * LANES:(2 * g + 1) * LANES] = acc[:rows]
        o_ref[0, :, (2 * g + 1) * LANES:(2 * g + 2) * LANES] = acc[rows:]


def _attn_body(sink_ref, q_ref, kp_ref, kc_ref, kn_ref, kx_ref, vp_ref, vc_ref, vn_ref, vx_ref, o_ref):
    n = pl.program_id(1)
    blk = q_ref.shape[1]
    ri = lax.broadcasted_iota(jnp.int32, (blk, blk), 0)
    ci = lax.broadcasted_iota(jnp.int32, (blk, blk), 1)
    mask_p = jnp.logical_and(ci >= ri, n > 0)
    mask_n = jnp.logical_and(ci <= ri, n < pl.num_programs(1) - 1)
    _softmax_heads(q_ref, (kp_ref, kc_ref, kn_ref, kx_ref), (vp_ref, vc_ref, vn_ref, vx_ref), sink_ref,
                   (mask_p, None, mask_n, None), o_ref)


def _attention(q3, k3, v3, kx3, vx3, sink):
    bsz, length, wq = q3.shape
    wk = k3.shape[2]
    lc = kx3.shape[1]
    nb = length // C_BLOCK
    cur = lambda w: pl.BlockSpec((1, C_BLOCK, w), lambda b, i: (b, i, 0))
    prv = lambda w: pl.BlockSpec((1, C_BLOCK, w), lambda b, i: (b, jnp.maximum(i - 1, 0), 0))
    nxt = lambda w: pl.BlockSpec((1, C_BLOCK, w), lambda b, i: (b, jnp.minimum(i + 1, nb - 1), 0))
    ctx = pl.BlockSpec((1, lc, wk), lambda b, i: (b, 0, 0))
    return pl.pallas_call(
        _attn_body,
        out_shape=jax.ShapeDtypeStruct((bsz, length, wq), F32),
        grid=(bsz, nb),
        in_specs=[pl.BlockSpec(memory_space=pltpu.SMEM), cur(wq), prv(wk), cur(wk), nxt(wk), ctx,
                  prv(wk), cur(wk), nxt(wk), ctx],
        out_specs=cur(wq),
        compiler_params=_cparams(("parallel", "parallel")),
        name="window_attn",
    )(sink, q3, k3, k3, k3, kx3, v3, v3, v3, vx3)


def _ctx_attn_body(sink_ref, q_ref, k_ref, v_ref, o_ref):
    _softmax_heads(q_ref, (k_ref,), (v_ref,), sink_ref, (None,), o_ref)


def _ctx_attention(q3, k3, v3, sink):
    bsz, lc, wq = q3.shape
    wk = k3.shape[2]
    spec = lambda w: pl.BlockSpec((1, lc, w), lambda b: (b, 0, 0))
    return pl.pallas_call(
        _ctx_attn_body,
        out_shape=jax.ShapeDtypeStruct((bsz, lc, wq), F32),
        grid=(bsz,),
        in_specs=[pl.BlockSpec(memory_space=pltpu.SMEM), spec(wq), spec(wk), spec(wk)],
        out_specs=spec(wq),
        compiler_params=_cparams(("parallel",)),
        name="ctx_attn",
    )(sink, q3, k3, v3)


def _merge_body(h_ref, mod_ref, of_ref, ob_ref, ga_ref, yb_ref, yc_ref, wg_ref, wbr_ref, wo_ref, nw_ref,
                g_ref, b_ref, o_ref, *, alpha):
    h = h_ref[...]
    d = h.shape[1]
    u = (h * (1.0 + mod_ref[0, 4:5, :]) + mod_ref[0, 3:4, :]).astype(BF16)
    gates = [jnp.dot(u, wg_ref[:, n * d:(n + 1) * d], preferred_element_type=F32) for n in range(3)]
    branches = [jnp.dot(y_ref[...].astype(BF16), wbr_ref[n], preferred_element_type=F32)
                for n, y_ref in ((1, yb_ref), (2, yc_ref))]
    o = of_ref[...] + ob_ref[...]
    heads = []
    for hh in range(o.shape[1] // A_DK):
        oh = o[:, hh * A_DK:(hh + 1) * A_DK]
        ms = jnp.mean(oh * oh, axis=-1, keepdims=True)
        heads.append(oh * lax.rsqrt(ms + RMS_EPS) * nw_ref[:, hh * A_DK:(hh + 1) * A_DK])
    ya = jnp.concatenate(heads, axis=1) * _silu(ga_ref[...])
    branches.insert(0, jnp.dot(ya.astype(BF16), wbr_ref[0], preferred_element_type=F32))
    m = jax.nn.sigmoid(gates[0]) * branches[0]
    for n in range(1, 3):
        m = m + jax.nn.sigmoid(gates[n]) * branches[n]
    y = jnp.dot(m.astype(BF16), wo_ref[...], preferred_element_type=F32)
    r = alpha * h + mod_ref[0, 5:6, :] * y
    o_ref[...] = _layer_norm(r, g_ref[...], b_ref[...])


def _merge(h2, mod, of2, ob2, pa2, yb2, yc2, wg, wbr, wo, norm_w, g, b, rows_per_mod, alpha, tm):
    r, d = h2.shape
    bw = of2.shape[1]
    tpm = rows_per_mod // tm
    body = functools.partial(_merge_body, alpha=alpha)
    row = lambda w: pl.BlockSpec((tm, w), lambda i: (i, 0))
    return pl.pallas_call(
        body,
        out_shape=jax.ShapeDtypeStruct((r, d), F32),
        grid=(r // tm,),
        in_specs=[row(d), pl.BlockSpec((1, N_MOD, d), lambda i: (i // tpm, 0, 0)),
                  row(bw), row(bw), pl.BlockSpec((tm, bw), lambda i: (i, 2)), row(bw), row(bw),
                  _const_spec(wg.shape), _const_spec(wbr.shape), _const_spec(wo.shape),
                  _const_spec((1, bw)), _const_spec((1, d)), _const_spec((1, d))],
        out_specs=row(d),
        compiler_params=_cparams(("parallel",)),
        name="merge",
    )(h2, mod, of2, ob2, pa2, yb2, yc2, wg, wbr, wo, norm_w.reshape(1, bw), g.reshape(1, d), b.reshape(1, d))


def kernel(x, c, ctx, c_ctx, ada_w, ada_b, ln_g, ln_b, ffn_w_in, ffn_w_out, mix_w_in, hgrn_lb, hgrn_norm_w,
           hyena_conv_w, hyena_conv_b, hyena_w1, hyena_b1, hyena_f1, hyena_w2, hyena_b2, hyena_f2, hyena_w3,
           hyena_bias, attn_sink, branch_w, out_w):
    bsz, seq, d = x.shape
    lc = ctx.shape[1]
    depth = ada_w.shape[0]
    alpha = (2.0 * depth) ** 0.25
    bw = hgrn_lb.shape[2]
    wk = C_KV_HEADS * C_HEAD_DIM
    widths = (5 * bw, (B_ORDER + 1) * bw, bw, wk)
    off_g = widths[0] + widths[1] + widths[2] + 2 * wk
    tm = 1024
    tmp = 512
    tmc = min(256, lc)

    s = jax.nn.softmax(hgrn_lb.astype(F32), axis=0)
    lower_bounds = jnp.cumsum(s, axis=0) - s[0:1]

    c8 = jnp.zeros((SUBLANES, d), F32).at[:bsz].set(c).at[bsz].set(c_ctx)
    mods = _ada_mod(c8, ada_w, ada_b).reshape(depth, SUBLANES, N_MOD, d)

    cos_t, sin_t = _rope_tables(seq)
    feats, tcol = _hyena_feats(seq)
    feats_c, tcol_c = _hyena_feats(lc)
    rates = _decay_rates(bw)
    tables = _dft_tables(seq)

    w_in_bf = ffn_w_in.astype(BF16)
    w_out_bf = ffn_w_out.astype(BF16)
    proj_bf = mix_w_in[:, :, :off_g].astype(BF16)
    gate_bf = mix_w_in[:, :, off_g:].astype(BF16)
    br_bf = branch_w.astype(BF16)
    out_bf = out_w.astype(BF16)

    h = x.reshape(bsz * seq, d)
    hc = ctx.reshape(bsz * lc, d)
    for l in range(depth):
        last = l == depth - 1
        mod = mods[l, :bsz]
        modc = mods[l, bsz:bsz + 1]
        w_proj = proj_bf[l]
        w_gate = gate_bf[l]
        ffn = lambda t, mm, m0, j, g, rpm, tt: _ffn(t, mm, m0, w_in_bf[l, j], w_out_bf[l, j], ln_g[l, g],
                                                     ln_b[l, g], rpm, alpha, tt)
        h = ffn(h, mod, 0, 0, 0, seq, tm)
        hc = ffn(hc, modc, 0, 0, 0, bsz * lc, tmc)

        conv = (hyena_conv_w[l], hyena_conv_b[l])
        pa, b0, b1, b2, pq, pk, pv = _inproj(h, mod, w_proj, *conv, cos_t, sin_t, seq, seq, True, widths, tmp)
        ca, c0, c1, c2, cq, ck, cv = _inproj(hc, modc, w_proj, *conv, cos_t, sin_t, bsz * lc, lc, False, widths,
                                             tmc)

        s0 = jnp.zeros((bsz, A_HEADS, 2, A_DK, A_DK), F32)
        ocf, ocb, s_ctx = _hgrn(ca.reshape(bsz, lc, -1), lower_bounds[l], s0, lc)
        of, ob, _ = _hgrn(pa.reshape(bsz, seq, -1), lower_bounds[l], s_ctx, 512)

        parts = [p.reshape(bsz, seq, bw) for p in (b0, b1, b2)]
        fargs = (hyena_w1[l], hyena_b1[l], hyena_f1[l], hyena_w2[l], hyena_b2[l], hyena_f2[l], hyena_w3[l], rates)
        filt, l1 = _hyena_filter(feats, tcol, *fargs, seq)
        yb = _hyena_long(parts, filt, l1, hyena_bias[l], tables)

        kx, vx = ck.reshape(bsz, lc, 4 * wk), cv.reshape(bsz, lc, 4 * wk)
        yc = _attention(pq.reshape(bsz, seq, bw), pk.reshape(bsz, seq, 4 * wk), pv.reshape(bsz, seq, 4 * wk),
                        kx, vx, attn_sink[l])

        merge = lambda t, mm, a1, a2, a3, a4, a5, rpm, tt: _merge(
            t, mm, a1, a2, a3, a4, a5, w_gate, br_bf[l], out_bf[l], hgrn_norm_w[l], ln_g[l, 1], ln_b[l, 1],
            rpm, alpha, tt)
        h = merge(h, mod, of.reshape(-1, bw), ob.reshape(-1, bw), pa, yb.reshape(-1, bw), yc.reshape(-1, bw),
                  seq, tmp)
        h = ffn(h, mod, 6, 1, 2, seq, tm)
        if not last:
            cparts = [p.reshape(bsz, lc, bw) for p in (c0, c1, c2)]
            cfilt, cl1 = _hyena_filter(feats_c, tcol_c, *fargs, lc)
            ycb = _hyena_ctx(cparts, cfilt, cl1, hyena_bias[l])
            ycc = _ctx_attention(cq.reshape(bsz, lc, bw), kx, vx, attn_sink[l])
            hc = merge(hc, modc, ocf.reshape(-1, bw), ocb.reshape(-1, bw), ca, ycb.reshape(-1, bw),
                       ycc.reshape(-1, bw), bsz * lc, tmc)
            hc = ffn(hc, modc, 6, 1, 2, bsz * lc, tmc)
    return h.reshape(bsz, seq, d)
```

```python
import functools
import math

import jax
import jax.numpy as jnp
from jax import lax
from jax.experimental import pallas as pl
from jax.experimental.pallas import tpu as pltpu

F32 = jnp.float32
BF16 = jnp.bfloat16
HI = lax.Precision.HIGHEST

LANES = 128
SUBLANES = 8
VMEM_LIMIT = 56 * 1024 * 1024

N_MOD = 9
A_HEADS = 4
A_DK = 128
B_ORDER = 2
B_EMB = 33
B_BANDS = 16
B_FFN = 64
B_FAST_DECAY = 0.3
B_SLOW_DECAY = 1.5
B_TARGET = 1e-2
C_HEAD_DIM = 64
C_HEADS = 8
C_KV_HEADS = 2
C_GROUP = 4
C_BLOCK = 128
GRID_W = 64
ROPE_BASE = 10000.0
LN_EPS = 1e-5
RMS_EPS = 1e-6
DFT_N2 = 256
NEG_BIG = -1e30


def _cparams(sem, vmem=VMEM_LIMIT):
    return pltpu.CompilerParams(dimension_semantics=sem, vmem_limit_bytes=vmem)


def _const_spec(shape):
    nd = len(shape)
    return pl.BlockSpec(shape, lambda *_: (0,) * nd, pipeline_mode=pl.Buffered(1))


def _layer_spec(stacked, idx):
    tail = stacked.shape[len(idx):]
    return pl.BlockSpec((None,) * len(idx) + tail, lambda *_: tuple(idx) + (0,) * len(tail),
                        pipeline_mode=pl.Buffered(1))


def _bdot(a, b):
    return jnp.dot(a.astype(BF16), b.astype(BF16), preferred_element_type=F32)


def _bdot_nt(a, b):
    return lax.dot_general(a.astype(BF16), b.astype(BF16), (((1,), (1,)), ((), ())),
                           preferred_element_type=F32)


def _hdot(a, b):
    return jnp.dot(a, b, precision=HI, preferred_element_type=F32)


def _layer_norm(x, g, b):
    mu = jnp.mean(x, axis=-1, keepdims=True)
    xc = x - mu
    var = jnp.mean(xc * xc, axis=-1, keepdims=True)
    return xc * lax.rsqrt(var + LN_EPS) * g + b


def _silu(x):
    return x * jax.nn.sigmoid(x)


def _ada_body(c_ref, w_ref, b_ref, o_ref):
    o_ref[0] = _hdot(_silu(c_ref[...]), w_ref[0]) + b_ref[0]


def _ada_mod(c8, ada_w, ada_b):
    depth, d, nw = ada_w.shape
    tn = nw // 8
    return pl.pallas_call(
        _ada_body,
        out_shape=jax.ShapeDtypeStruct((depth, SUBLANES, nw), F32),
        grid=(depth, nw // tn),
        in_specs=[pl.BlockSpec((SUBLANES, d), lambda l, j: (0, 0)),
                  pl.BlockSpec((1, d, tn), lambda l, j: (l, 0, j)),
                  pl.BlockSpec((1, 1, tn), lambda l, j: (l, 0, j))],
        out_specs=pl.BlockSpec((1, SUBLANES, tn), lambda l, j: (l, 0, j)),
        compiler_params=_cparams(("parallel", "parallel")),
        name="ada_mod",
    )(c8, ada_w, ada_b.reshape(depth, 1, nw))


def _ffn_body(h_ref, mod_ref, win_ref, wout_ref, g_ref, b_ref, o_ref, *, m0, d_ff, fc, alpha):
    h = h_ref[...]
    u = (h * (1.0 + mod_ref[0, m0 + 1:m0 + 2, :]) + mod_ref[0, m0:m0 + 1, :]).astype(BF16)
    acc = jnp.zeros_like(h)
    for j in range(d_ff // fc):
        a = jnp.dot(u, win_ref[:, j * fc:(j + 1) * fc], preferred_element_type=F32)
        b = jnp.dot(u, win_ref[:, d_ff + j * fc:d_ff + (j + 1) * fc], preferred_element_type=F32)
        act = (_silu(a) * b).astype(BF16)
        acc = acc + jnp.dot(act, wout_ref[j * fc:(j + 1) * fc, :], preferred_element_type=F32)
    r = alpha * h + 0.5 * mod_ref[0, m0 + 2:m0 + 3, :] * acc
    o_ref[...] = _layer_norm(r, g_ref[...], b_ref[...])


def _ffn(h2, mod, m0, w_in, w_out, idx, g, b, rows_per_mod, alpha, tm):
    r, d = h2.shape
    d_ff = w_out.shape[-2]
    tpm = rows_per_mod // tm
    fc = 2 * LANES
    assert d_ff % fc == 0
    body = functools.partial(_ffn_body, m0=m0, d_ff=d_ff, fc=fc, alpha=alpha)
    return pl.pallas_call(
        body,
        out_shape=jax.ShapeDtypeStruct((r, d), F32),
        grid=(r // tm,),
        in_specs=[pl.BlockSpec((tm, d), lambda i: (i, 0)),
                  pl.BlockSpec((1, N_MOD, d), lambda i: (i // tpm, 0, 0)),
                  _layer_spec(w_in, idx), _layer_spec(w_out, idx),
                  _const_spec((1, d)), _const_spec((1, d))],
        out_specs=pl.BlockSpec((tm, d), lambda i: (i, 0)),
        compiler_params=_cparams(("parallel",)),
        name="ffn",
    )(h2, mod, w_in, w_out, g.reshape(1, d), b.reshape(1, d))


def _swap16(x):
    lane = lax.broadcasted_iota(jnp.int32, x.shape, 1)
    return jnp.where((lane % 32) < 16, pltpu.roll(x, LANES - 16, 1), pltpu.roll(x, 16, 1))


def _head_pair_variants(x):
    lane = lax.broadcasted_iota(jnp.int32, x.shape, 1)
    low = lane < C_HEAD_DIM
    xs = pltpu.roll(x, C_HEAD_DIM, 1)
    parts = (jnp.where(low, x, 0.0), jnp.where(low, 0.0, xs), jnp.where(low, xs, 0.0), jnp.where(low, 0.0, x))
    return jnp.concatenate(parts, axis=1).astype(BF16)


def _inproj_body(h_ref, hp_ref, hn_ref, mod_ref, w_ref, cw_ref, cb_ref, cos_ref, sin_ref,
                 pa_ref, b0_ref, b1_ref, b2_ref, pq_ref, pk_ref, pv_ref, *, wa, wb, wq, wk, rope, tps):
    assert wk == LANES
    i = pl.program_id(0)
    tm = h_ref.shape[0]
    scale1 = 1.0 + mod_ref[0, 4:5, :]
    shift = mod_ref[0, 3:4, :]
    u = (h_ref[...] * scale1 + shift).astype(BF16)
    wbm = w_ref[:, wa:wa + wb]
    pb = jnp.dot(u, wbm, preferred_element_type=F32)
    up = (hp_ref[...] * scale1 + shift).astype(BF16)
    un = (hn_ref[...] * scale1 + shift).astype(BF16)
    prev_row = jnp.dot(up, wbm, preferred_element_type=F32)[SUBLANES - 1:SUBLANES, :]
    next_row = jnp.dot(un, wbm, preferred_element_type=F32)[0:1, :]
    prev_row = jnp.where(i % tps > 0, prev_row, 0.0)
    next_row = jnp.where(i % tps < tps - 1, next_row, 0.0)
    oq = wa + wb
    q = jnp.dot(u, w_ref[:, oq:oq + wq], preferred_element_type=F32)
    k = jnp.dot(u, w_ref[:, oq + wq:oq + wq + wk], preferred_element_type=F32)
    v = jnp.dot(u, w_ref[:, oq + wq + wk:oq + wq + 2 * wk], preferred_element_type=F32)
    pa_ref[...] = jnp.dot(u, w_ref[:, 0:wa], preferred_element_type=F32)
    row = lax.broadcasted_iota(jnp.int32, (tm, 1), 0)
    xm = jnp.where(row == 0, prev_row, pltpu.roll(pb, 1, 0))
    xp = jnp.where(row == tm - 1, next_row, pltpu.roll(pb, tm - 1, 0))
    y = xm * cw_ref[0:1, :] + pb * cw_ref[1:2, :] + xp * cw_ref[2:3, :] + cb_ref[...]
    cw = wb // 3
    b0_ref[...] = y[:, 0:cw]
    b1_ref[...] = y[:, cw:2 * cw]
    b2_ref[...] = y[:, 2 * cw:3 * cw]
    scale = C_HEAD_DIM ** -0.5
    if rope:
        cs = cos_ref[...]
        sn = sin_ref[...]
        for j in range(wq // LANES):
            xq = q[:, j * LANES:(j + 1) * LANES]
            pq_ref[:, j * LANES:(j + 1) * LANES] = ((xq * cs + _swap16(xq) * sn) * scale).astype(BF16)
        k = k * cs + _swap16(k) * sn
    else:
        pq_ref[...] = (q * scale).astype(BF16)
    pk_ref[...] = _head_pair_variants(k)
    pv_ref[...] = _head_pair_variants(v)


def _inproj(h2, mod, w, idx, conv_w, conv_b, cos_t, sin_t, rows_per_mod, rows_per_seq, rope, widths, tm):
    r, d = h2.shape
    wa, wb, wq, wk = widths
    tpm = rows_per_mod // tm
    tps = rows_per_seq // tm
    r8 = tm // SUBLANES
    n8 = r // SUBLANES
    body = functools.partial(_inproj_body, wa=wa, wb=wb, wq=wq, wk=wk, rope=rope, tps=tps)
    ow = (wa, wb // 3, wb // 3, wb // 3, wq, 4 * wk, 4 * wk)
    od = (F32, F32, F32, F32, BF16, BF16, BF16)
    return pl.pallas_call(
        body,
        out_shape=[jax.ShapeDtypeStruct((r, n), t) for n, t in zip(ow, od)],
        grid=(r // tm,),
        in_specs=[pl.BlockSpec((tm, d), lambda i: (i, 0)),
                  pl.BlockSpec((SUBLANES, d), lambda i: (jnp.maximum(i * r8 - 1, 0), 0)),
                  pl.BlockSpec((SUBLANES, d), lambda i: (jnp.minimum((i + 1) * r8, n8 - 1), 0)),
                  pl.BlockSpec((1, N_MOD, d), lambda i: (i // tpm, 0, 0)),
                  _layer_spec(w, idx), _const_spec((3, wb)), _const_spec((1, wb)),
                  pl.BlockSpec((tm, LANES), lambda i: (i % tps, 0)),
                  pl.BlockSpec((tm, LANES), lambda i: (i % tps, 0))],
        out_specs=[pl.BlockSpec((tm, n), lambda i: (i, 0)) for n in ow],
        compiler_params=_cparams(("parallel",)),
        name="inproj",
    )(h2, h2, h2, mod, w, conv_w, conv_b.reshape(1, wb), cos_t, sin_t)


HGRN_CHUNK = 128


def _pivot_rows(b_ref, c, r0):
    t, w = b_ref.shape
    g = 2 * c
    if g >= SUBLANES:
        rows = [jnp.broadcast_to(b_ref[s + r0:s + r0 + 1, :], (g, w)) for s in range(0, t, g)]
        return rows[0] if len(rows) == 1 else jnp.concatenate(rows, axis=0)
    b = b_ref[...]
    row = lax.broadcasted_iota(jnp.int32, (t, 1), 0) % g
    out = b
    for m in range(g):
        if m != r0:
            out = jnp.where(row == m, pltpu.roll(b, (m - r0) % t, 0), out)
    return out


def _split3_bf16(x):
    c = 65537.0
    t = c * x
    hi = t - (t - x)
    rest = x - hi
    t = c * rest
    mid = t - (t - rest)
    lo = rest - mid
    return hi.astype(BF16), mid.astype(BF16), lo.astype(BF16)


def _hgrn_gates(z, lb, tri, b_ref, k_ref):
    e = jnp.exp(-jnp.abs(z))
    r = 1.0 / (1.0 + e)
    pos = z >= 0.0
    sig_p = jnp.where(pos, r, e * r)
    sig_n = jnp.where(pos, e * r, r)
    logf = jnp.log2(lb + (1.0 - lb) * sig_p)
    k_ref[...] = ((1.0 - lb) * sig_n).astype(BF16)
    bb = jnp.dot(tri, jnp.concatenate(_split3_bf16(logf), axis=1), preferred_element_type=F32)
    w = z.shape[1]
    b_ref[...] = bb[:, 0:w] + bb[:, w:2 * w] + bb[:, 2 * w:3 * w]


def _hgrn_diag(q, k_ref, qh_ref, att_ref):
    t = q.shape[0]
    qh_ref[...] = q.astype(BF16)
    ri = lax.broadcasted_iota(jnp.int32, (t, t), 0)
    ci = lax.broadcasted_iota(jnp.int32, (t, t), 1)
    att_ref[...] = jnp.where(ri == ci, jnp.sum(q * k_ref[...].astype(F32), axis=-1, keepdims=True), 0.0)


def _hgrn_level(lvl, sgn_ref, b_ref, k_ref, qh_ref, att_ref, c, reverse):
    level = c.bit_length() - 1
    piv = _pivot_rows(b_ref, c, c if reverse else c - 1)
    dec = jnp.exp2((b_ref[...] - piv) * sgn_ref[level]).astype(BF16)
    att_ref[...] = jnp.where(lvl == level, _bdot_nt(qh_ref[...] * dec, k_ref[...] * dec), att_ref[...])


def _hgrn_finish(q, v, st, b_ref, k_ref, att_ref, reverse):
    t = q.shape[0]
    b = b_ref[...]
    vb = v.astype(BF16)
    out = _bdot_nt(q * jnp.exp2(b), st) + jnp.dot(att_ref[...].astype(BF16), vb, preferred_element_type=F32)
    blast = b[0:1, :] if reverse else b[t - 1:t, :]
    ke = k_ref[...] * jnp.exp2(blast - b).astype(BF16)
    return out, st * jnp.exp2(blast) + jnp.dot(vb.T, ke, preferred_element_type=F32)


def _hgrn_body(qf_ref, vf_ref, zf_ref, qb_ref, vb_ref, zb_ref, lb_ref, s0_ref, tri_ref, lvl_ref, sgn_ref,
               of_ref, ob_ref, sfin_ref, st_ref, b_ref, k_ref, qh_ref, att_ref, *, nc, nh):
    i = pl.program_id(1)

    @pl.when(i == 0)
    def _():
        st_ref[...] = s0_ref[0]

    def step(j, carry):
        fwd = pl.ds(pl.multiple_of(j * HGRN_CHUNK, HGRN_CHUNK), HGRN_CHUNK)
        bwd = pl.ds(pl.multiple_of((nc - 1 - j) * HGRN_CHUNK, HGRN_CHUNK), HGRN_CHUNK)
        streams = ((qf_ref, vf_ref, zf_ref, of_ref, fwd), (qb_ref, vb_ref, zb_ref, ob_ref, bwd))
        chains = [(h, d) for h in range(nh) for d in range(2)]
        cols = lambda h: slice(h * A_DK, (h + 1) * A_DK)
        for h, d in chains:
            z_r, rows = streams[d][2], streams[d][4]
            _hgrn_gates(z_r[0, rows, cols(h)], lb_ref[d:d + 1, cols(h)], tri_ref[d], b_ref.at[h, d], k_ref.at[h, d])
        for h, d in chains:
            q_r, rows = streams[d][0], streams[d][4]
            _hgrn_diag(q_r[0, rows, cols(h)], k_ref.at[h, d], qh_ref.at[h, d], att_ref.at[h, d])
        c = HGRN_CHUNK // 2
        while c >= 1:
            for h, d in chains:
                _hgrn_level(lvl_ref[d], sgn_ref.at[d], b_ref.at[h, d], k_ref.at[h, d], qh_ref.at[h, d],
                            att_ref.at[h, d], c, reverse=(d == 1))
            c //= 2
        for h, d in chains:
            q_r, v_r, _, o_r, rows = streams[d]
            out, st_new = _hgrn_finish(q_r[0, rows, cols(h)], v_r[0, rows, cols(h)], st_ref[h, d], b_ref.at[h, d],
                                       k_ref.at[h, d], att_ref.at[h, d], reverse=(d == 1))
            o_r[0, rows, cols(h)] = out
            st_ref[h, d] = st_new
        return carry

    lax.fori_loop(0, nc, step, 0)

    @pl.when(i == pl.num_programs(1) - 1)
    def _():
        sfin_ref[0] = st_ref[...]


def _hgrn(pa3, lb, s0, t):
    bsz, length, _ = pa3.shape
    w = lb.shape[1]
    nh = w // A_DK
    n = length // t
    ii = jnp.arange(HGRN_CHUNK, dtype=jnp.int32)
    tri = jnp.stack([ii[:, None] >= ii[None, :], ii[:, None] <= ii[None, :]]).astype(BF16)
    top = 31 - lax.clz(ii[:, None] ^ ii[None, :])
    lvl = jnp.stack([jnp.where(ii[:, None] > ii[None, :], top, -1), jnp.where(ii[:, None] < ii[None, :], top, -1)])
    nlev = HGRN_CHUNK.bit_length() - 1
    later = ((ii[None, :, None] >> jnp.arange(nlev, dtype=jnp.int32)[:, None, None]) & 1) == 1
    sgn_f = jnp.broadcast_to(jnp.where(later, 1.0, -1.0).astype(F32), (nlev, HGRN_CHUNK, A_DK))
    sgn = jnp.stack([sgn_f, -sgn_f])

    def col(off, rev):
        if rev:
            return pl.BlockSpec((1, t, w), lambda b, i: (b, n - 1 - i, off))
        return pl.BlockSpec((1, t, w), lambda b, i: (b, i, off))

    st_spec = pl.BlockSpec((1, nh, 2, A_DK, A_DK), lambda b, i: (b, 0, 0, 0, 0))
    sq = pl.BlockSpec((2, HGRN_CHUNK, HGRN_CHUNK), lambda b, i: (0, 0, 0))
    return pl.pallas_call(
        functools.partial(_hgrn_body, nc=t // HGRN_CHUNK, nh=nh),
        out_shape=[jax.ShapeDtypeStruct((bsz, length, w), F32),
                   jax.ShapeDtypeStruct((bsz, length, w), F32),
                   jax.ShapeDtypeStruct((bsz, nh, 2, A_DK, A_DK), F32)],
        grid=(bsz, n),
        in_specs=[col(0, False), col(1, False), col(3, False),
                  col(0, True), col(1, True), col(4, True),
                  pl.BlockSpec((2, w), lambda b, i: (0, 0)),
                  st_spec, sq, sq,
                  pl.BlockSpec((2, nlev, HGRN_CHUNK, A_DK), lambda b, i: (0, 0, 0, 0))],
        out_specs=[col(0, False), col(0, True), st_spec],
        scratch_shapes=[pltpu.VMEM((nh, 2, A_DK, A_DK), F32),
                        pltpu.VMEM((nh, 2, HGRN_CHUNK, A_DK), F32),
                        pltpu.VMEM((nh, 2, HGRN_CHUNK, A_DK), BF16),
                        pltpu.VMEM((nh, 2, HGRN_CHUNK, A_DK), BF16),
                        pltpu.VMEM((nh, 2, HGRN_CHUNK, HGRN_CHUNK), F32)],
        compiler_params=_cparams(("parallel", "arbitrary")),
        name="hgrn",
    )(pa3, pa3, pa3, pa3, pa3, pa3, lb, s0, tri, lvl, sgn)


def _hyena_feats(length):
    n = jnp.arange(2 * length)
    pos = jnp.where(n < length, n, 2 * length - n).astype(F32)
    t = pos / (length - 1)
    w = 2.0 * math.pi * pos / length
    bands = jnp.linspace(1e-4, B_BANDS - 1, B_BANDS, dtype=F32)
    feats = jnp.concatenate([t[None, :], jnp.cos(bands[:, None] * w[None, :]), -jnp.sin(bands[:, None] * w[None, :])],
                            axis=0)
    return jnp.pad(feats, ((0, B_FFN - B_EMB), (0, 0))), t[:, None]


def _decay_rates(width):
    max_decay = math.log(B_TARGET) / B_FAST_DECAY
    min_decay = math.log(B_TARGET) / B_SLOW_DECAY
    return jnp.abs(jnp.linspace(min_decay, max_decay, width, dtype=F32)).reshape(1, width)


def _filter_body(f_ref, t_ref, w1_ref, b1_ref, f1_ref, w2_ref, b2_ref, f2_ref, w3_ref, rate_ref, o_ref, l1_ref,
                 *, tr, length):
    i = pl.program_id(0)
    h = jnp.sin(f1_ref[...] * (_hdot(w1_ref[...], f_ref[...]) + b1_ref[...]))
    h = jnp.sin(f2_ref[...] * (_hdot(w2_ref[...], h) + b2_ref[...]))
    h = jnp.dot(h.T.astype(BF16), w3_ref[...], preferred_element_type=F32)
    dec = jnp.exp(-(t_ref[...] * rate_ref[...]))
    h = h * jnp.concatenate([dec] * B_ORDER, axis=1)
    row = i * tr + lax.broadcasted_iota(jnp.int32, (tr, 1), 0)
    h = jnp.where(row == length, 0.0, h)
    o_ref[...] = h

    @pl.when(i == 0)
    def _():
        l1_ref[...] = jnp.zeros_like(l1_ref)

    l1_ref[...] += jnp.sum(jnp.abs(h), axis=0, keepdims=True)


def _hyena_filter(feats_t, tcol, w1, b1, f1, w2, b2, f2, w3, rates, length):
    width = rates.shape[1]
    ow = B_ORDER * width
    tr = min(512, length)
    nblk = 2 * length // tr
    w1t = jnp.pad(w1, ((0, B_FFN - B_EMB), (0, 0))).T
    body = functools.partial(_filter_body, tr=tr, length=length)
    small = lambda shape: pl.BlockSpec(shape, lambda i: (0, 0))
    col = lambda v: v.reshape(-1, 1)
    return pl.pallas_call(
        body,
        out_shape=[jax.ShapeDtypeStruct((2 * length, ow), F32), jax.ShapeDtypeStruct((1, ow), F32)],
        grid=(nblk,),
        in_specs=[pl.BlockSpec((B_FFN, tr), lambda i: (0, i)),
                  pl.BlockSpec((tr, 1), lambda i: (i, 0)),
                  small((B_FFN, B_FFN)), small((B_FFN, 1)), small((B_FFN, 1)),
                  small((B_FFN, B_FFN)), small((B_FFN, 1)), small((B_FFN, 1)),
                  pl.BlockSpec((B_FFN, ow), lambda i: (0, (2 * i) // nblk)),
                  small((1, width))],
        out_specs=[pl.BlockSpec((tr, ow), lambda i: (i, 0)), pl.BlockSpec((1, ow), lambda i: (0, 0))],
        compiler_params=_cparams(("arbitrary",)),
        name="hyena_filter",
    )(feats_t, tcol, w1t, col(b1), col(f1), w2.T, col(b2), col(f2), w3.astype(BF16), rates)


def _dft_tables(length):
    n = 2 * length
    n2 = DFT_N2
    n1 = n // n2
    n1h = n1 // 2
    k1 = jnp.arange(n1, dtype=jnp.int32)
    j2 = jnp.arange(n2, dtype=jnp.int32)
    ang1 = ((k1[:, None] * k1[None, :]) % n1).astype(F32) * (2.0 * math.pi / n1)
    ang2 = (j2[:, None] * k1[None, :]).astype(F32) * (2.0 * math.pi / n)
    c1, s1, c2, s2 = jnp.cos(ang1), jnp.sin(ang1), jnp.cos(ang2), jnp.sin(ang2)
    er = c1[None] * c2[:, :, None] - s1[None] * s2[:, :, None]
    ei = -(c1[None] * s2[:, :, None] + s1[None] * c2[:, :, None])
    er_t = c1[None] * c2[:, None, :] - s1[None] * s2[:, None, :]
    ei_t = -(c1[None] * s2[:, None, :] + s1[None] * c2[:, None, :])
    a_f = jnp.concatenate([er, ei], axis=1)
    erh, eih = er[:, :, :n1h], ei[:, :, :n1h]
    a_z = jnp.concatenate([jnp.concatenate([erh, -eih], axis=2),
                           jnp.concatenate([eih, erh], axis=2)], axis=1)
    erth, eith = er_t[:, :n1h, :], ei_t[:, :n1h, :]
    a_inv = jnp.concatenate([jnp.concatenate([erth, eith], axis=2),
                             jnp.concatenate([-eith, erth], axis=2)], axis=1) / n
    kk = jnp.arange(n2, dtype=jnp.int32)
    ang2 = ((kk[:, None] * kk[None, :]) % n2).astype(F32) * (-2.0 * math.pi / n2)
    fr, fi = jnp.cos(ang2), jnp.sin(ang2)
    f2 = jnp.concatenate([jnp.concatenate([fr, -fi], axis=1), jnp.concatenate([fi, fr], axis=1)], axis=0)
    return a_f.astype(BF16), a_z.astype(BF16), a_inv.astype(BF16), f2.astype(BF16), f2.T.astype(BF16)


DFT_BW = 512


def _to_slabs(x_ref, slab_ref, scale=None):
    rows, nb, w = x_ref.shape
    for lt in range(w // LANES):
        x = x_ref[:, :, lt * LANES:(lt + 1) * LANES]
        if scale is not None:
            x = x * scale[:, lt * LANES:(lt + 1) * LANES]
        slab_ref[lt] = x.reshape(rows * nb, LANES)


def _slab_rows(slab_ref, j, rows, nb):
    return jnp.concatenate([slab_ref[lt, pl.ds(j, rows, stride=nb), :] for lt in range(slab_ref.shape[0])], axis=1)


def _dfta_fwd_body(x_ref, a_ref, s_ref, o_ref, xs_ref, ys_ref):
    rows, nb, w = x_ref.shape
    m = o_ref.shape[0]
    _to_slabs(x_ref, xs_ref, s_ref[...])
    for j in range(nb):
        y = jnp.dot(a_ref[j], _slab_rows(xs_ref, j, rows, nb).astype(BF16), preferred_element_type=F32)
        for lt in range(w // LANES):
            ys_ref[lt, pl.ds(j, m, stride=nb), :] = y[:, lt * LANES:(lt + 1) * LANES]
    for lt in range(w // LANES):
        o_ref[:, :, lt * LANES:(lt + 1) * LANES] = ys_ref[lt].reshape(m, nb, LANES)


def _dfta_fwd(x3, a, scale):
    rows, n2, cw = x3.shape
    m = a.shape[1]
    nb, bwid = SUBLANES, DFT_BW
    return pl.pallas_call(
        _dfta_fwd_body,
        out_shape=jax.ShapeDtypeStruct((m, n2, cw), F32),
        grid=(n2 // nb, cw // bwid),
        in_specs=[pl.BlockSpec((rows, nb, bwid), lambda i, c: (0, i, c)),
                  pl.BlockSpec((nb, m, rows), lambda i, c: (i, 0, 0)),
                  pl.BlockSpec((1, bwid), lambda i, c: (0, c))],
        out_specs=pl.BlockSpec((m, nb, bwid), lambda i, c: (0, i, c)),
        scratch_shapes=[pltpu.VMEM((bwid // LANES, rows * nb, LANES), F32),
                        pltpu.VMEM((bwid // LANES, m * nb, LANES), F32)],
        compiler_params=_cparams(("parallel", "parallel")),
        name="dft_stage_a",
    )(x3, a, scale)


def _dfta_inv_body(u_ref, a_ref, xn_ref, z_ref, bias_ref, o_ref, us_ref, ys_ref):
    k, nb, w = u_ref.shape
    m = o_ref.shape[0]
    _to_slabs(u_ref, us_ref)
    for j in range(nb):
        y = jnp.dot(a_ref[j], _slab_rows(us_ref, j, k, nb).astype(BF16), preferred_element_type=F32)
        for lt in range(w // LANES):
            ys_ref[lt, pl.ds(j, m, stride=nb), :] = y[:, lt * LANES:(lt + 1) * LANES]
    for lt in range(w // LANES):
        sl = slice(lt * LANES, (lt + 1) * LANES)
        y = ys_ref[lt].reshape(m, nb, LANES)
        o_ref[:, :, sl] = xn_ref[:, :, sl] * (y + bias_ref[:, sl] * z_ref[:, :, sl])


def _dfta_inv(u3, a_inv, xn3, z3, bias):
    n2, m, k = a_inv.shape
    cw = u3.shape[2]
    nb, bwid = SUBLANES, DFT_BW
    return pl.pallas_call(
        _dfta_inv_body,
        out_shape=jax.ShapeDtypeStruct((m, n2, cw), F32),
        grid=(n2 // nb, cw // bwid),
        in_specs=[pl.BlockSpec((k, nb, bwid), lambda i, c: (0, i, c)),
                  pl.BlockSpec((nb, m, k), lambda i, c: (i, 0, 0)),
                  pl.BlockSpec((m, nb, bwid), lambda i, c: (0, i, c)),
                  pl.BlockSpec((m, nb, bwid), lambda i, c: (0, i, c)),
                  pl.BlockSpec((1, bwid), lambda i, c: (0, c))],
        out_specs=pl.BlockSpec((m, nb, bwid), lambda i, c: (0, i, c)),
        scratch_shapes=[pltpu.VMEM((bwid // LANES, k * nb, LANES), F32),
                        pltpu.VMEM((bwid // LANES, m * nb, LANES), F32)],
        compiler_params=_cparams(("parallel", "parallel")),
        name="dft_stage_a_inv",
    )(u3, a_inv, xn3, z3, bias)


def _dftc_filter_body(t_ref, f_ref, o_ref, *, kb):
    for j in range(kb):
        t = t_ref[:, j]
        t2 = t.reshape(2 * DFT_N2, t.shape[-1])
        y = jnp.dot(f_ref[...], t2.astype(BF16), preferred_element_type=F32)
        o_ref[j] = y.reshape(2, DFT_N2, t.shape[-1]).astype(BF16)


def _dftc_filter(t4, f2, kb):
    _, n1, n2, ow = t4.shape
    body = functools.partial(_dftc_filter_body, kb=kb)
    return pl.pallas_call(
        body,
        out_shape=jax.ShapeDtypeStruct((n1, 2, n2, ow), BF16),
        grid=(n1 // kb,),
        in_specs=[pl.BlockSpec((2, kb, n2, ow), lambda i: (0, i, 0, 0)),
                  pl.BlockSpec((2 * n2, 2 * n2), lambda i: (0, 0))],
        out_specs=pl.BlockSpec((kb, 2, n2, ow), lambda i: (i, 0, 0, 0)),
        compiler_params=_cparams(("parallel",)),
        name="dft_stage_c_filter",
    )(t4, f2)


def _dftc_mid_body(t_ref, h_ref, f_ref, fi_ref, o_ref, *, kb):
    cw = t_ref.shape[-1]
    ys = [jnp.dot(f_ref[...], t_ref[:, j].reshape(2 * DFT_N2, cw).astype(BF16), preferred_element_type=F32)
          for j in range(kb)]
    ps = []
    for j, y in enumerate(ys):
        yr, yi = y[:DFT_N2], y[DFT_N2:]
        hr, hi = h_ref[j, 0].astype(F32), h_ref[j, 1].astype(F32)
        ps.append(jnp.concatenate([yr * hr - yi * hi, yr * hi + yi * hr], axis=0).astype(BF16))
    for j, p in enumerate(ps):
        u = jnp.dot(fi_ref[...], p, preferred_element_type=F32)
        o_ref[0, j] = u[:DFT_N2]
        o_ref[1, j] = u[DFT_N2:]


def _dftc_mid(t4, hspec, order, f2, f2inv, kb):
    _, n1, n2, cw = t4.shape
    body = functools.partial(_dftc_mid_body, kb=kb)
    return pl.pallas_call(
        body,
        out_shape=jax.ShapeDtypeStruct((2, n1, n2, cw), F32),
        grid=(n1 // kb,),
        in_specs=[pl.BlockSpec((2, kb, n2, cw), lambda i: (0, i, 0, 0)),
                  pl.BlockSpec((kb, 2, n2, cw), lambda i: (i, 0, 0, order)),
                  pl.BlockSpec((2 * n2, 2 * n2), lambda i: (0, 0)),
                  pl.BlockSpec((2 * n2, 2 * n2), lambda i: (0, 0))],
        out_specs=pl.BlockSpec((2, kb, n2, cw), lambda i: (0, i, 0, 0)),
        compiler_params=_cparams(("parallel",)),
        name="dft_stage_c_mid",
    )(t4, hspec, f2, f2inv)


def _hyena_long(parts, filt, l1, bias, tables):
    a_f, a_z, a_inv, f2, f2inv = tables
    bsz, length, cw = parts[0].shape
    assert bsz == 2, "the two batch rows ride as real and imaginary parts of one transform"
    n1 = 2 * length // DFT_N2
    ow = filt.shape[1]
    tf = _dfta_fwd(filt.reshape(n1, DFT_N2, ow), a_f, 1.0 / l1)
    hspec = _dftc_filter(tf.reshape(2, n1, DFT_N2, ow), f2, 2)
    ones = jnp.ones((1, cw), F32)
    z = parts[0].reshape(bsz * n1 // 2, DFT_N2, cw)
    for o in range(B_ORDER):
        t = _dfta_fwd(z, a_z, ones)
        u = _dftc_mid(t.reshape(2, n1, DFT_N2, cw), hspec, o, f2, f2inv, 4)
        z = _dfta_inv(u.reshape(2 * n1, DFT_N2, cw), a_inv, parts[o + 1].reshape(z.shape), z,
                      bias[o:o + 1])
    return z.reshape(bsz, length, cw)


def _ctx_conv_body(v_ref, x1_ref, x2_ref, filt_ref, il1_ref, bias_ref, ff_ref, fz_ref, fi_ref, o_ref, *, n, cw):
    hf = _hdot(ff_ref[...], filt_ref[...] * il1_ref[...])
    z = jnp.concatenate([v_ref[0], v_ref[1]], axis=0)
    for o, x_ref in enumerate((x1_ref, x2_ref)):
        zz = _hdot(fz_ref[...], z)
        zr, zi = zz[:n], zz[n:]
        hr, hi = hf[:n, o * cw:(o + 1) * cw], hf[n:, o * cw:(o + 1) * cw]
        y = _hdot(fi_ref[...], jnp.concatenate([zr * hr - zi * hi, zr * hi + zi * hr], axis=0))
        xn = jnp.concatenate([x_ref[0], x_ref[1]], axis=0)
        z = xn * (y + bias_ref[o:o + 1, :] * z)
    half = n // 2
    o_ref[0] = z[:half]
    o_ref[1] = z[half:]


def _hyena_ctx(parts, filt, l1, bias):
    bsz, length, cw = parts[0].shape
    assert bsz == 2
    n = 2 * length
    k = jnp.arange(n, dtype=jnp.int32)
    ang = ((k[:, None] * k[None, :]) % n).astype(F32) * (-2.0 * math.pi / n)
    cr, ci = jnp.cos(ang), jnp.sin(ang)
    ff = jnp.concatenate([cr, ci], axis=0)
    crh, cih = cr[:, :length], ci[:, :length]
    fz = jnp.concatenate([jnp.concatenate([crh, -cih], axis=1), jnp.concatenate([cih, crh], axis=1)], axis=0)
    fi = fz.T / n
    body = functools.partial(_ctx_conv_body, n=n, cw=cw)
    full = lambda a: pl.BlockSpec(a.shape, lambda i: (0,) * a.ndim)
    args = (parts[0], parts[1], parts[2], filt, 1.0 / l1, bias, ff, fz, fi)
    return pl.pallas_call(
        body,
        out_shape=jax.ShapeDtypeStruct((bsz, length, cw), F32),
        grid=(1,),
        in_specs=[full(a) for a in args],
        out_specs=pl.BlockSpec((bsz, length, cw), lambda i: (0, 0, 0)),
        compiler_params=_cparams(("arbitrary",)),
        name="hyena_ctx",
    )(*args)


def _rope_tables(length):
    n_rows = length // GRID_W
    row = jnp.repeat(jnp.arange(n_rows), GRID_W).astype(F32)
    col = jnp.tile(jnp.arange(GRID_W), n_rows).astype(F32)
    nf = C_HEAD_DIM // 4
    inv = ROPE_BASE ** (-jnp.arange(nf, dtype=F32) * 2.0 / (C_HEAD_DIM // 2))
    ar, ac = row[:, None] * inv, col[:, None] * inv
    cos_h = jnp.concatenate([jnp.cos(ar), jnp.cos(ar), jnp.cos(ac), jnp.cos(ac)], axis=-1)
    sin_h = jnp.concatenate([-jnp.sin(ar), jnp.sin(ar), -jnp.sin(ac), jnp.sin(ac)], axis=-1)
    reps = LANES // C_HEAD_DIM
    return jnp.tile(cos_h, (1, reps)), jnp.tile(sin_h, (1, reps))


def _softmax_heads(q_ref, keys, vals, sink_ref, masks, o_ref):
    rows = q_ref.shape[1]
    top = lax.broadcasted_iota(jnp.int32, (2 * rows, 1), 0) < rows
    chains = [(g, half) for g in range(C_KV_HEADS) for half in range(2)]
    off = lambda g, half: (2 * g + half) * LANES
    lane_blocks = lambda xs: [x[:, j * LANES:(j + 1) * LANES] for x in xs for j in range(x.shape[1] // LANES)]
    qs = [jnp.concatenate([q_ref[0, :, (2 * g) * LANES:(2 * g + 1) * LANES],
                           q_ref[0, :, (2 * g + 1) * LANES:(2 * g + 2) * LANES]], axis=0)
          for g in range(C_KV_HEADS)]
    scores, shifts, sinks = {}, {}, {}
    for g, half in chains:
        ss = []
        for k_ref, msk in zip(keys, masks):
            s = lax.dot_general(qs[g], k_ref[0, :, off(g, half):off(g, half) + LANES], (((1,), (1,)), ((), ())),
                                preferred_element_type=F32)
            ss.append(s if msk is None else jnp.where(jnp.concatenate([msk, msk], axis=0), s, NEG_BIG))
        sk = jnp.where(top, sink_ref[C_GROUP * g + half], sink_ref[C_GROUP * g + 2 + half])
        mx = functools.reduce(jnp.maximum, lane_blocks(ss))
        scores[g, half], sinks[g, half] = ss, sk
        shifts[g, half] = jnp.maximum(jnp.max(mx, axis=-1, keepdims=True), sk)
    probs, rdens = {}, {}
    for c in chains:
        ps = [jnp.exp(s - shifts[c]) for s in scores[c]]
        den = jnp.sum(functools.reduce(jnp.add, lane_blocks(ps)), axis=-1, keepdims=True) + jnp.exp(sinks[c] - shifts[c])
        probs[c], rdens[c] = [p.astype(BF16) for p in ps], 1.0 / den
    outs = {}
    for g, half in chains:
        o = None
        for p, v_ref in zip(probs[g, half], vals):
            t = jnp.dot(p, v_ref[0, :, off(g, half):off(g, half) + LANES], preferred_element_type=F32)
            o = t if o is None else o + t
        outs[g, half] = o * rdens[g, half]
    for g in range(C_KV_HEADS):
        acc = outs[g, 0] + outs[g, 1]
        o_ref[0, :, (2 * g) * LANES:(2 * g + 1) * LANES] = acc[:rows]
        o_ref[0, :, (2 * g + 1) * LANES:(2 * g + 2) * LANES] = acc[rows:]


def _attn_body(sink_ref, q_ref, kp_ref, kc_ref, kn_ref, kx_ref, vp_ref, vc_ref, vn_ref, vx_ref, o_ref):
    n = pl.program_id(1)
    blk = q_ref.shape[1]
    ri = lax.broadcasted_iota(jnp.int32, (blk, blk), 0)
    ci = lax.broadcasted_iota(jnp.int32, (blk, blk), 1)
    mask_p = jnp.logical_and(ci >= ri, n > 0)
    mask_n = jnp.logical_and(ci <= ri, n < pl.num_programs(1) - 1)
    _softmax_heads(q_ref, (kp_ref, kc_ref, kn_ref, kx_ref), (vp_ref, vc_ref, vn_ref, vx_ref), sink_ref,
                   (mask_p, None, mask_n, None), o_ref)


def _attention(q3, k3, v3, kx3, vx3, sink):
    bsz, length, wq = q3.shape
    wk = k3.shape[2]
    lc = kx3.shape[1]
    nb = length // C_BLOCK
    cur = lambda w: pl.BlockSpec((1, C_BLOCK, w), lambda b, i: (b, i, 0))
    prv = lambda w: pl.BlockSpec((1, C_BLOCK, w), lambda b, i: (b, jnp.maximum(i - 1, 0), 0))
    nxt = lambda w: pl.BlockSpec((1, C_BLOCK, w), lambda b, i: (b, jnp.minimum(i + 1, nb - 1), 0))
    ctx = pl.BlockSpec((1, lc, wk), lambda b, i: (b, 0, 0))
    return pl.pallas_call(
        _attn_body,
        out_shape=jax.ShapeDtypeStruct((bsz, length, wq), F32),
        grid=(bsz, nb),
        in_specs=[pl.BlockSpec(memory_space=pltpu.SMEM), cur(wq), prv(wk), cur(wk), nxt(wk), ctx,
                  prv(wk), cur(wk), nxt(wk), ctx],
        out_specs=cur(wq),
        compiler_params=_cparams(("parallel", "parallel")),
        name="window_attn",
    )(sink, q3, k3, k3, k3, kx3, v3, v3, v3, vx3)


def _ctx_attn_body(sink_ref, q_ref, k_ref, v_ref, o_ref):
    _softmax_heads(q_ref, (k_ref,), (v_ref,), sink_ref, (None,), o_ref)


def _ctx_attention(q3, k3, v3, sink):
    bsz, lc, wq = q3.shape
    wk = k3.shape[2]
    spec = lambda w: pl.BlockSpec((1, lc, w), lambda b: (b, 0, 0))
    return pl.pallas_call(
        _ctx_attn_body,
        out_shape=jax.ShapeDtypeStruct((bsz, lc, wq), F32),
        grid=(bsz,),
        in_specs=[pl.BlockSpec(memory_space=pltpu.SMEM), spec(wq), spec(wk), spec(wk)],
        out_specs=spec(wq),
        compiler_params=_cparams(("parallel",)),
        name="ctx_attn",
    )(sink, q3, k3, v3)


def _merge_body(h_ref, mod_ref, of_ref, ob_ref, ga_ref, yb_ref, yc_ref, wg_ref, wbr_ref, wo_ref, nw_ref,
                g_ref, b_ref, o_ref, *, alpha):
    h = h_ref[...]
    d = h.shape[1]
    u = (h * (1.0 + mod_ref[0, 4:5, :]) + mod_ref[0, 3:4, :]).astype(BF16)
    gates = [jnp.dot(u, wg_ref[:, n * d:(n + 1) * d], preferred_element_type=F32) for n in range(3)]
    branches = [jnp.dot(y_ref[...].astype(BF16), wbr_ref[n], preferred_element_type=F32)
                for n, y_ref in ((1, yb_ref), (2, yc_ref))]
    o = of_ref[...] + ob_ref[...]
    heads = []
    for hh in range(o.shape[1] // A_DK):
        oh = o[:, hh * A_DK:(hh + 1) * A_DK]
        ms = jnp.mean(oh * oh, axis=-1, keepdims=True)
        heads.append(oh * lax.rsqrt(ms + RMS_EPS) * nw_ref[:, hh * A_DK:(hh + 1) * A_DK])
    ya = jnp.concatenate(heads, axis=1) * _silu(ga_ref[...])
    branches.insert(0, jnp.dot(ya.astype(BF16), wbr_ref[0], preferred_element_type=F32))
    m = jax.nn.sigmoid(gates[0]) * branches[0]
    for n in range(1, 3):
        m = m + jax.nn.sigmoid(gates[n]) * branches[n]
    y = jnp.dot(m.astype(BF16), wo_ref[...], preferred_element_type=F32)
    r = alpha * h + mod_ref[0, 5:6, :] * y
    o_ref[...] = _layer_norm(r, g_ref[...], b_ref[...])


def _merge(h2, mod, of2, ob2, pa2, yb2, yc2, wg, wbr, wo, idx, norm_w, g, b, rows_per_mod, alpha, tm):
    r, d = h2.shape
    bw = of2.shape[1]
    tpm = rows_per_mod // tm
    body = functools.partial(_merge_body, alpha=alpha)
    row = lambda w: pl.BlockSpec((tm, w), lambda i: (i, 0))
    return pl.pallas_call(
        body,
        out_shape=jax.ShapeDtypeStruct((r, d), F32),
        grid=(r // tm,),
        in_specs=[row(d), pl.BlockSpec((1, N_MOD, d), lambda i: (i // tpm, 0, 0)),
                  row(bw), row(bw), pl.BlockSpec((tm, bw), lambda i: (i, 2)), row(bw), row(bw),
                  _layer_spec(wg, idx), _layer_spec(wbr, idx), _layer_spec(wo, idx),
                  _const_spec((1, bw)), _const_spec((1, d)), _const_spec((1, d))],
        out_specs=row(d),
        compiler_params=_cparams(("parallel",)),
        name="merge",
    )(h2, mod, of2, ob2, pa2, yb2, yc2, wg, wbr, wo, norm_w.reshape(1, bw), g.reshape(1, d), b.reshape(1, d))


def kernel(x, c, ctx, c_ctx, ada_w, ada_b, ln_g, ln_b, ffn_w_in, ffn_w_out, mix_w_in, hgrn_lb, hgrn_norm_w,
           hyena_conv_w, hyena_conv_b, hyena_w1, hyena_b1, hyena_f1, hyena_w2, hyena_b2, hyena_f2, hyena_w3,
           hyena_bias, attn_sink, branch_w, out_w):
    bsz, seq, d = x.shape
    lc = ctx.shape[1]
    depth = ada_w.shape[0]
    alpha = (2.0 * depth) ** 0.25
    bw = hgrn_lb.shape[2]
    wk = C_KV_HEADS * C_HEAD_DIM
    widths = (5 * bw, (B_ORDER + 1) * bw, bw, wk)
    off_g = widths[0] + widths[1] + widths[2] + 2 * wk
    tm = 1024
    tmp = 512
    tmc = min(256, lc)

    s = jax.nn.softmax(hgrn_lb.astype(F32), axis=0)
    lower_bounds = jnp.cumsum(s, axis=0) - s[0:1]

    c8 = jnp.zeros((SUBLANES, d), F32).at[:bsz].set(c).at[bsz].set(c_ctx)
    mods = _ada_mod(c8, ada_w, ada_b).reshape(depth, SUBLANES, N_MOD, d)

    cos_t, sin_t = _rope_tables(seq)
    feats, tcol = _hyena_feats(seq)
    feats_c, tcol_c = _hyena_feats(lc)
    rates = _decay_rates(bw)
    tables = _dft_tables(seq)

    w_in_bf = ffn_w_in.astype(BF16)
    w_out_bf = ffn_w_out.astype(BF16)
    proj_bf = mix_w_in[:, :, :off_g].astype(BF16)
    gate_bf = mix_w_in[:, :, off_g:].astype(BF16)
    br_bf = branch_w.astype(BF16)
    out_bf = out_w.astype(BF16)

    h = x.reshape(bsz * seq, d)
    hc = ctx.reshape(bsz * lc, d)
    for l in range(depth):
        last = l == depth - 1
        mod = mods[l, :bsz]
        modc = mods[l, bsz:bsz + 1]
        ffn = lambda t, mm, m0, j, g, rpm, tt: _ffn(t, mm, m0, w_in_bf, w_out_bf, (l, j), ln_g[l, g],
                                                     ln_b[l, g], rpm, alpha, tt)
        h = ffn(h, mod, 0, 0, 0, seq, tm)
        hc = ffn(hc, modc, 0, 0, 0, bsz * lc, tmc)

        conv = (hyena_conv_w[l], hyena_conv_b[l])
        pa, b0, b1, b2, pq, pk, pv = _inproj(h, mod, proj_bf, (l,), *conv, cos_t, sin_t, seq, seq, True, widths, tmp)
        ca, c0, c1, c2, cq, ck, cv = _inproj(hc, modc, proj_bf, (l,), *conv, cos_t, sin_t, bsz * lc, lc, False, widths,
                                             tmc)

        s0 = jnp.zeros((bsz, A_HEADS, 2, A_DK, A_DK), F32)
        ocf, ocb, s_ctx = _hgrn(ca.reshape(bsz, lc, -1), lower_bounds[l], s0, lc)
        of, ob, _ = _hgrn(pa.reshape(bsz, seq, -1), lower_bounds[l], s_ctx, 512)

        parts = [p.reshape(bsz, seq, bw) for p in (b0, b1, b2)]
        fargs = (hyena_w1[l], hyena_b1[l], hyena_f1[l], hyena_w2[l], hyena_b2[l], hyena_f2[l], hyena_w3[l], rates)
        filt, l1 = _hyena_filter(feats, tcol, *fargs, seq)
        yb = _hyena_long(parts, filt, l1, hyena_bias[l], tables)

        kx, vx = ck.reshape(bsz, lc, 4 * wk), cv.reshape(bsz, lc, 4 * wk)
        yc = _attention(pq.reshape(bsz, seq, bw), pk.reshape(bsz, seq, 4 * wk), pv.reshape(bsz, seq, 4 * wk),
                        kx, vx, attn_sink[l])

        merge = lambda t, mm, a1, a2, a3, a4, a5, rpm, tt: _merge(
            t, mm, a1, a2, a3, a4, a5, gate_bf, br_bf, out_bf, (l,), hgrn_norm_w[l], ln_g[l, 1], ln_b[l, 1],
            rpm, alpha, tt)
        h = merge(h, mod, of.reshape(-1, bw), ob.reshape(-1, bw), pa, yb.reshape(-1, bw), yc.reshape(-1, bw),
                  seq, tmp)
        h = ffn(h, mod, 6, 1, 2, seq, tm)
        if not last:
            cparts = [p.reshape(bsz, lc, bw) for p in (c0, c1, c2)]
            cfilt, cl1 = _hyena_filter(feats_c, tcol_c, *fargs, lc)
            ycb = _hyena_ctx(cparts, cfilt, cl1, hyena_bias[l])
            ycc = _ctx_attention(cq.reshape(bsz, lc, bw), kx, vx, attn_sink[l])
            hc = merge(hc, modc, ocf.reshape(-1, bw), ocb.reshape(-1, bw), ca, ycb.reshape(-1, bw),
                       ycc.reshape(-1, bw), bsz * lc, tmc)
            hc = ffn(hc, modc, 6, 1, 2, bsz * lc, tmc)
    return h.reshape(bsz, seq, d)
```

```python
import functools
import math

import jax
import jax.numpy as jnp
from jax import lax
from jax.experimental import pallas as pl
from jax.experimental.pallas import tpu as pltpu

F32 = jnp.float32
BF16 = jnp.bfloat16
HI = lax.Precision.HIGHEST

LANES = 128
SUBLANES = 8
VMEM_LIMIT = 56 * 1024 * 1024

N_MOD = 9
A_HEADS = 4
A_DK = 128
B_ORDER = 2
B_EMB = 33
B_BANDS = 16
B_FFN = 64
B_FAST_DECAY = 0.3
B_SLOW_DECAY = 1.5
B_TARGET = 1e-2
C_HEAD_DIM = 64
C_HEADS = 8
C_KV_HEADS = 2
C_GROUP = 4
C_BLOCK = 128
GRID_W = 64
ROPE_BASE = 10000.0
LN_EPS = 1e-5
RMS_EPS = 1e-6
DFT_N2 = 256
NEG_BIG = -1e30


def _cparams(sem, vmem=VMEM_LIMIT):
    return pltpu.CompilerParams(dimension_semantics=sem, vmem_limit_bytes=vmem)


def _const_spec(shape):
    nd = len(shape)
    return pl.BlockSpec(shape, lambda *_: (0,) * nd, pipeline_mode=pl.Buffered(1))


def _layer_spec(stacked, idx):
    tail = stacked.shape[len(idx):]
    return pl.BlockSpec((None,) * len(idx) + tail, lambda *_: tuple(idx) + (0,) * len(tail),
                        pipeline_mode=pl.Buffered(1))


def _bdot(a, b):
    return jnp.dot(a.astype(BF16), b.astype(BF16), preferred_element_type=F32)


def _bdot_nt(a, b):
    return lax.dot_general(a.astype(BF16), b.astype(BF16), (((1,), (1,)), ((), ())),
                           preferred_element_type=F32)


def _hdot(a, b):
    return jnp.dot(a, b, precision=HI, preferred_element_type=F32)


def _layer_norm(x, g, b):
    mu = jnp.mean(x, axis=-1, keepdims=True)
    xc = x - mu
    var = jnp.mean(xc * xc, axis=-1, keepdims=True)
    return xc * lax.rsqrt(var + LN_EPS) * g + b


def _silu(x):
    return x * jax.nn.sigmoid(x)


def _ada_body(c_ref, w_ref, b_ref, o_ref):
    o_ref[0] = _hdot(_silu(c_ref[...]), w_ref[0]) + b_ref[0]


def _ada_mod(c8, ada_w, ada_b):
    depth, d, nw = ada_w.shape
    tn = nw // 8
    return pl.pallas_call(
        _ada_body,
        out_shape=jax.ShapeDtypeStruct((depth, SUBLANES, nw), F32),
        grid=(depth, nw // tn),
        in_specs=[pl.BlockSpec((SUBLANES, d), lambda l, j: (0, 0)),
                  pl.BlockSpec((1, d, tn), lambda l, j: (l, 0, j)),
                  pl.BlockSpec((1, 1, tn), lambda l, j: (l, 0, j))],
        out_specs=pl.BlockSpec((1, SUBLANES, tn), lambda l, j: (l, 0, j)),
        compiler_params=_cparams(("parallel", "parallel")),
        name="ada_mod",
    )(c8, ada_w, ada_b.reshape(depth, 1, nw))


def _ffn_body(h_ref, mod_ref, win_ref, wout_ref, g_ref, b_ref, o_ref, *, m0, d_ff, fc, alpha):
    h = h_ref[...]
    u = (h * (1.0 + mod_ref[0, m0 + 1:m0 + 2, :]) + mod_ref[0, m0:m0 + 1, :]).astype(BF16)
    acc = jnp.zeros_like(h)
    for j in range(d_ff // fc):
        a = jnp.dot(u, win_ref[:, j * fc:(j + 1) * fc], preferred_element_type=F32)
        b = jnp.dot(u, win_ref[:, d_ff + j * fc:d_ff + (j + 1) * fc], preferred_element_type=F32)
        act = (_silu(a) * b).astype(BF16)
        acc = acc + jnp.dot(act, wout_ref[j * fc:(j + 1) * fc, :], preferred_element_type=F32)
    r = alpha * h + 0.5 * mod_ref[0, m0 + 2:m0 + 3, :] * acc
    o_ref[...] = _layer_norm(r, g_ref[...], b_ref[...])


def _ffn(h2, mod, m0, w_in, w_out, idx, g, b, rows_per_mod, alpha, tm):
    r, d = h2.shape
    d_ff = w_out.shape[-2]
    tpm = rows_per_mod // tm
    fc = 2 * LANES
    assert d_ff % fc == 0
    body = functools.partial(_ffn_body, m0=m0, d_ff=d_ff, fc=fc, alpha=alpha)
    return pl.pallas_call(
        body,
        out_shape=jax.ShapeDtypeStruct((r, d), F32),
        grid=(r // tm,),
        in_specs=[pl.BlockSpec((tm, d), lambda i: (i, 0)),
                  pl.BlockSpec((1, N_MOD, d), lambda i: (i // tpm, 0, 0)),
                  _layer_spec(w_in, idx), _layer_spec(w_out, idx),
                  _const_spec((1, d)), _const_spec((1, d))],
        out_specs=pl.BlockSpec((tm, d), lambda i: (i, 0)),
        compiler_params=_cparams(("parallel",)),
        name="ffn",
    )(h2, mod, w_in, w_out, g.reshape(1, d), b.reshape(1, d))


def _swap16(x):
    lane = lax.broadcasted_iota(jnp.int32, x.shape, 1)
    return jnp.where((lane % 32) < 16, pltpu.roll(x, LANES - 16, 1), pltpu.roll(x, 16, 1))


def _head_pair_variants(x):
    lane = lax.broadcasted_iota(jnp.int32, x.shape, 1)
    low = lane < C_HEAD_DIM
    xs = pltpu.roll(x, C_HEAD_DIM, 1)
    parts = (jnp.where(low, x, 0.0), jnp.where(low, 0.0, xs), jnp.where(low, xs, 0.0), jnp.where(low, 0.0, x))
    return jnp.concatenate(parts, axis=1).astype(BF16)


def _inproj_body(h_ref, hp_ref, hn_ref, mod_ref, w_ref, cw_ref, cb_ref, cos_ref, sin_ref,
                 pa_ref, b0_ref, b1_ref, b2_ref, pq_ref, pk_ref, pv_ref, *, wa, wb, wq, wk, rope, tps):
    assert wk == LANES
    i = pl.program_id(0)
    tm = h_ref.shape[0]
    scale1 = 1.0 + mod_ref[0, 4:5, :]
    shift = mod_ref[0, 3:4, :]
    u = (h_ref[...] * scale1 + shift).astype(BF16)
    wbm = w_ref[:, wa:wa + wb]
    pb = jnp.dot(u, wbm, preferred_element_type=F32)
    up = (hp_ref[...] * scale1 + shift).astype(BF16)
    un = (hn_ref[...] * scale1 + shift).astype(BF16)
    prev_row = jnp.dot(up, wbm, preferred_element_type=F32)[SUBLANES - 1:SUBLANES, :]
    next_row = jnp.dot(un, wbm, preferred_element_type=F32)[0:1, :]
    prev_row = jnp.where(i % tps > 0, prev_row, 0.0)
    next_row = jnp.where(i % tps < tps - 1, next_row, 0.0)
    oq = wa + wb
    q = jnp.dot(u, w_ref[:, oq:oq + wq], preferred_element_type=F32)
    k = jnp.dot(u, w_ref[:, oq + wq:oq + wq + wk], preferred_element_type=F32)
    v = jnp.dot(u, w_ref[:, oq + wq + wk:oq + wq + 2 * wk], preferred_element_type=F32)
    pa_ref[...] = jnp.dot(u, w_ref[:, 0:wa], preferred_element_type=F32)
    row = lax.broadcasted_iota(jnp.int32, (tm, 1), 0)
    xm = jnp.where(row == 0, prev_row, pltpu.roll(pb, 1, 0))
    xp = jnp.where(row == tm - 1, next_row, pltpu.roll(pb, tm - 1, 0))
    y = xm * cw_ref[0:1, :] + pb * cw_ref[1:2, :] + xp * cw_ref[2:3, :] + cb_ref[...]
    cw = wb // 3
    b0_ref[...] = y[:, 0:cw]
    b1_ref[...] = y[:, cw:2 * cw]
    b2_ref[...] = y[:, 2 * cw:3 * cw]
    scale = C_HEAD_DIM ** -0.5
    if rope:
        cs = cos_ref[...]
        sn = sin_ref[...]
        for j in range(wq // LANES):
            xq = q[:, j * LANES:(j + 1) * LANES]
            pq_ref[:, j * LANES:(j + 1) * LANES] = ((xq * cs + _swap16(xq) * sn) * scale).astype(BF16)
        k = k * cs + _swap16(k) * sn
    else:
        pq_ref[...] = (q * scale).astype(BF16)
    pk_ref[...] = _head_pair_variants(k)
    pv_ref[...] = _head_pair_variants(v)


def _inproj(h2, mod, w, idx, conv_w, conv_b, cos_t, sin_t, rows_per_mod, rows_per_seq, rope, widths, tm):
    r, d = h2.shape
    wa, wb, wq, wk = widths
    tpm = rows_per_mod // tm
    tps = rows_per_seq // tm
    r8 = tm // SUBLANES
    n8 = r // SUBLANES
    body = functools.partial(_inproj_body, wa=wa, wb=wb, wq=wq, wk=wk, rope=rope, tps=tps)
    ow = (wa, wb // 3, wb // 3, wb // 3, wq, 4 * wk, 4 * wk)
    od = (F32, F32, F32, F32, BF16, BF16, BF16)
    return pl.pallas_call(
        body,
        out_shape=[jax.ShapeDtypeStruct((r, n), t) for n, t in zip(ow, od)],
        grid=(r // tm,),
        in_specs=[pl.BlockSpec((tm, d), lambda i: (i, 0)),
                  pl.BlockSpec((SUBLANES, d), lambda i: (jnp.maximum(i * r8 - 1, 0), 0)),
                  pl.BlockSpec((SUBLANES, d), lambda i: (jnp.minimum((i + 1) * r8, n8 - 1), 0)),
                  pl.BlockSpec((1, N_MOD, d), lambda i: (i // tpm, 0, 0)),
                  _layer_spec(w, idx), _const_spec((3, wb)), _const_spec((1, wb)),
                  pl.BlockSpec((tm, LANES), lambda i: (i % tps, 0)),
                  pl.BlockSpec((tm, LANES), lambda i: (i % tps, 0))],
        out_specs=[pl.BlockSpec((tm, n), lambda i: (i, 0)) for n in ow],
        compiler_params=_cparams(("parallel",)),
        name="inproj",
    )(h2, h2, h2, mod, w, conv_w, conv_b.reshape(1, wb), cos_t, sin_t)


HGRN_CHUNK = 128
HGRN_LEVEL_GROUPS = ((64, 32), (16, 8), (4, 2, 1))


def _pivot_rows(b_ref, c, r0):
    t, w = b_ref.shape
    g = 2 * c
    if g >= SUBLANES:
        rows = [jnp.broadcast_to(b_ref[s + r0:s + r0 + 1, :], (g, w)) for s in range(0, t, g)]
        return rows[0] if len(rows) == 1 else jnp.concatenate(rows, axis=0)
    b = b_ref[...]
    row = lax.broadcasted_iota(jnp.int32, (t, 1), 0) % g
    out = b
    for m in range(g):
        if m != r0:
            out = jnp.where(row == m, pltpu.roll(b, (m - r0) % t, 0), out)
    return out


def _split3_bf16(x):
    c = 65537.0
    t = c * x
    hi = t - (t - x)
    rest = x - hi
    t = c * rest
    mid = t - (t - rest)
    lo = rest - mid
    return hi.astype(BF16), mid.astype(BF16), lo.astype(BF16)


def _hgrn_gates(z, lb, tri, b_ref, k_ref):
    e = jnp.exp(-jnp.abs(z))
    r = 1.0 / (1.0 + e)
    pos = z >= 0.0
    sig_p = jnp.where(pos, r, e * r)
    sig_n = jnp.where(pos, e * r, r)
    logf = jnp.log2(lb + (1.0 - lb) * sig_p)
    k_ref[...] = ((1.0 - lb) * sig_n).astype(BF16)
    bb = jnp.dot(tri, jnp.concatenate(_split3_bf16(logf), axis=1), preferred_element_type=F32)
    w = z.shape[1]
    b_ref[...] = bb[:, 0:w] + bb[:, w:2 * w] + bb[:, 2 * w:3 * w]


def _hgrn_diag(q, k_ref, qh_ref, att_ref):
    t = q.shape[0]
    qh_ref[...] = q.astype(BF16)
    ri = lax.broadcasted_iota(jnp.int32, (t, t), 0)
    ci = lax.broadcasted_iota(jnp.int32, (t, t), 1)
    att_ref[...] = jnp.where(ri == ci, jnp.sum(q * k_ref[...].astype(F32), axis=-1, keepdims=True), 0.0)


def _hgrn_levels(lvl, sgn_ref, b_ref, k_ref, qh_ref, att_ref, cs, reverse):
    b, qh, kh = b_ref[...], qh_ref[...], k_ref[...]
    prods = []
    for c in cs:
        level = c.bit_length() - 1
        piv = _pivot_rows(b_ref, c, c if reverse else c - 1)
        dec = jnp.exp2((b - piv) * sgn_ref[level]).astype(BF16)
        prods.append((level, _bdot_nt(qh * dec, kh * dec)))
    att = att_ref[...]
    for level, a in prods:
        att = jnp.where(lvl == level, a, att)
    att_ref[...] = att


def _hgrn_finish(q, v, st, b_ref, k_ref, att_ref, reverse):
    t = q.shape[0]
    b = b_ref[...]
    vb = v.astype(BF16)
    out = _bdot_nt(q * jnp.exp2(b), st) + jnp.dot(att_ref[...].astype(BF16), vb, preferred_element_type=F32)
    blast = b[0:1, :] if reverse else b[t - 1:t, :]
    ke = k_ref[...] * jnp.exp2(blast - b).astype(BF16)
    return out, st * jnp.exp2(blast) + jnp.dot(vb.T, ke, preferred_element_type=F32)


def _hgrn_body(qf_ref, vf_ref, zf_ref, qb_ref, vb_ref, zb_ref, lb_ref, s0_ref, tri_ref, lvl_ref, sgn_ref,
               of_ref, ob_ref, sfin_ref, st_ref, b_ref, k_ref, qh_ref, att_ref, *, nc, nh):
    i = pl.program_id(1)

    @pl.when(i == 0)
    def _():
        st_ref[...] = s0_ref[0]

    def step(j, carry):
        fwd = pl.ds(pl.multiple_of(j * HGRN_CHUNK, HGRN_CHUNK), HGRN_CHUNK)
        bwd = pl.ds(pl.multiple_of((nc - 1 - j) * HGRN_CHUNK, HGRN_CHUNK), HGRN_CHUNK)
        streams = ((qf_ref, vf_ref, zf_ref, of_ref, fwd), (qb_ref, vb_ref, zb_ref, ob_ref, bwd))
        chains = [(h, d) for h in range(nh) for d in range(2)]
        cols = lambda h: slice(h * A_DK, (h + 1) * A_DK)
        for h, d in chains:
            z_r, rows = streams[d][2], streams[d][4]
            _hgrn_gates(z_r[0, rows, cols(h)], lb_ref[d:d + 1, cols(h)], tri_ref[d], b_ref.at[h, d], k_ref.at[h, d])
        for h, d in chains:
            q_r, rows = streams[d][0], streams[d][4]
            _hgrn_diag(q_r[0, rows, cols(h)], k_ref.at[h, d], qh_ref.at[h, d], att_ref.at[h, d])
        for cs in HGRN_LEVEL_GROUPS:
            for h, d in chains:
                _hgrn_levels(lvl_ref[d], sgn_ref.at[d], b_ref.at[h, d], k_ref.at[h, d], qh_ref.at[h, d],
                             att_ref.at[h, d], cs, reverse=(d == 1))
        for h, d in chains:
            q_r, v_r, _, o_r, rows = streams[d]
            out, st_new = _hgrn_finish(q_r[0, rows, cols(h)], v_r[0, rows, cols(h)], st_ref[h, d], b_ref.at[h, d],
                                       k_ref.at[h, d], att_ref.at[h, d], reverse=(d == 1))
            o_r[0, rows, cols(h)] = out
            st_ref[h, d] = st_new
        return carry

    lax.fori_loop(0, nc, step, 0)

    @pl.when(i == pl.num_programs(1) - 1)
    def _():
        sfin_ref[0] = st_ref[...]


def _hgrn(pa3, lb, s0, t):
    bsz, length, _ = pa3.shape
    w = lb.shape[1]
    nh = w // A_DK
    n = length // t
    ii = jnp.arange(HGRN_CHUNK, dtype=jnp.int32)
    tri = jnp.stack([ii[:, None] >= ii[None, :], ii[:, None] <= ii[None, :]]).astype(BF16)
    top = 31 - lax.clz(ii[:, None] ^ ii[None, :])
    lvl = jnp.stack([jnp.where(ii[:, None] > ii[None, :], top, -1), jnp.where(ii[:, None] < ii[None, :], top, -1)])
    nlev = HGRN_CHUNK.bit_length() - 1
    later = ((ii[None, :, None] >> jnp.arange(nlev, dtype=jnp.int32)[:, None, None]) & 1) == 1
    sgn_f = jnp.broadcast_to(jnp.where(later, 1.0, -1.0).astype(F32), (nlev, HGRN_CHUNK, A_DK))
    sgn = jnp.stack([sgn_f, -sgn_f])

    def col(off, rev):
        if rev:
            return pl.BlockSpec((1, t, w), lambda b, i: (b, n - 1 - i, off))
        return pl.BlockSpec((1, t, w), lambda b, i: (b, i, off))

    st_spec = pl.BlockSpec((1, nh, 2, A_DK, A_DK), lambda b, i: (b, 0, 0, 0, 0))
    sq = pl.BlockSpec((2, HGRN_CHUNK, HGRN_CHUNK), lambda b, i: (0, 0, 0))
    return pl.pallas_call(
        functools.partial(_hgrn_body, nc=t // HGRN_CHUNK, nh=nh),
        out_shape=[jax.ShapeDtypeStruct((bsz, length, w), F32),
                   jax.ShapeDtypeStruct((bsz, length, w), F32),
                   jax.ShapeDtypeStruct((bsz, nh, 2, A_DK, A_DK), F32)],
        grid=(bsz, n),
        in_specs=[col(0, False), col(1, False), col(3, False),
                  col(0, True), col(1, True), col(4, True),
                  pl.BlockSpec((2, w), lambda b, i: (0, 0)),
                  st_spec, sq, sq,
                  pl.BlockSpec((2, nlev, HGRN_CHUNK, A_DK), lambda b, i: (0, 0, 0, 0))],
        out_specs=[col(0, False), col(0, True), st_spec],
        scratch_shapes=[pltpu.VMEM((nh, 2, A_DK, A_DK), F32),
                        pltpu.VMEM((nh, 2, HGRN_CHUNK, A_DK), F32),
                        pltpu.VMEM((nh, 2, HGRN_CHUNK, A_DK), BF16),
                        pltpu.VMEM((nh, 2, HGRN_CHUNK, A_DK), BF16),
                        pltpu.VMEM((nh, 2, HGRN_CHUNK, HGRN_CHUNK), F32)],
        compiler_params=_cparams(("parallel", "arbitrary")),
        name="hgrn",
    )(pa3, pa3, pa3, pa3, pa3, pa3, lb, s0, tri, lvl, sgn)


def _hyena_feats(length, order=None):
    n = jnp.arange(2 * length) if order is None else order
    pos = jnp.where(n < length, n, 2 * length - n).astype(F32)
    t = pos / (length - 1)
    w = 2.0 * math.pi * pos / length
    bands = jnp.linspace(1e-4, B_BANDS - 1, B_BANDS, dtype=F32)
    feats = jnp.concatenate([t[None, :], jnp.cos(bands[:, None] * w[None, :]), -jnp.sin(bands[:, None] * w[None, :])],
                            axis=0)
    return jnp.pad(feats, ((0, B_FFN - B_EMB), (0, 0))), t[:, None]


def _decay_rates(width):
    max_decay = math.log(B_TARGET) / B_FAST_DECAY
    min_decay = math.log(B_TARGET) / B_SLOW_DECAY
    return jnp.abs(jnp.linspace(min_decay, max_decay, width, dtype=F32)).reshape(1, width)


def _filter_body(f_ref, t_ref, w1_ref, b1_ref, f1_ref, w2_ref, b2_ref, f2_ref, w3_ref, rate_ref, o_ref, l1_ref,
                 *, tr, length):
    i = pl.program_id(0)
    h = jnp.sin(f1_ref[...] * (_hdot(w1_ref[...], f_ref[...]) + b1_ref[...]))
    h = jnp.sin(f2_ref[...] * (_hdot(w2_ref[...], h) + b2_ref[...]))
    h = jnp.dot(h.T.astype(BF16), w3_ref[...], preferred_element_type=F32)
    dec = jnp.exp(-(t_ref[...] * rate_ref[...]))
    h = h * jnp.concatenate([dec] * B_ORDER, axis=1)
    row = i * tr + lax.broadcasted_iota(jnp.int32, (tr, 1), 0)
    h = jnp.where(row == length, 0.0, h)
    o_ref[...] = h

    @pl.when(i == 0)
    def _():
        l1_ref[...] = jnp.zeros_like(l1_ref)

    l1_ref[...] += jnp.sum(jnp.abs(h), axis=0, keepdims=True)


def _hyena_filter(feats_t, tcol, w1, b1, f1, w2, b2, f2, w3, rates, length):
    width = rates.shape[1]
    ow = B_ORDER * width
    tr = min(512, length)
    nblk = 2 * length // tr
    w1t = jnp.pad(w1, ((0, B_FFN - B_EMB), (0, 0))).T
    body = functools.partial(_filter_body, tr=tr, length=length)
    small = lambda shape: pl.BlockSpec(shape, lambda i: (0, 0))
    col = lambda v: v.reshape(-1, 1)
    return pl.pallas_call(
        body,
        out_shape=[jax.ShapeDtypeStruct((2 * length, ow), F32), jax.ShapeDtypeStruct((1, ow), F32)],
        grid=(nblk,),
        in_specs=[pl.BlockSpec((B_FFN, tr), lambda i: (0, i)),
                  pl.BlockSpec((tr, 1), lambda i: (i, 0)),
                  small((B_FFN, B_FFN)), small((B_FFN, 1)), small((B_FFN, 1)),
                  small((B_FFN, B_FFN)), small((B_FFN, 1)), small((B_FFN, 1)),
                  pl.BlockSpec((B_FFN, ow), lambda i: (0, (2 * i) // nblk)),
                  small((1, width))],
        out_specs=[pl.BlockSpec((tr, ow), lambda i: (i, 0)), pl.BlockSpec((1, ow), lambda i: (0, 0))],
        compiler_params=_cparams(("arbitrary",)),
        name="hyena_filter",
    )(feats_t, tcol, w1t, col(b1), col(f1), w2.T, col(b2), col(f2), w3.astype(BF16), rates)


def _dft_tables(length):
    n = 2 * length
    n2 = DFT_N2
    n1 = n // n2
    n1h = n1 // 2
    k1 = jnp.arange(n1, dtype=jnp.int32)
    j2 = jnp.arange(n2, dtype=jnp.int32)
    ang1 = ((k1[:, None] * k1[None, :]) % n1).astype(F32) * (2.0 * math.pi / n1)
    ang2 = (j2[:, None] * k1[None, :]).astype(F32) * (2.0 * math.pi / n)
    c1, s1, c2, s2 = jnp.cos(ang1), jnp.sin(ang1), jnp.cos(ang2), jnp.sin(ang2)
    er = c1[None] * c2[:, :, None] - s1[None] * s2[:, :, None]
    ei = -(c1[None] * s2[:, :, None] + s1[None] * c2[:, :, None])
    er_t = c1[None] * c2[:, None, :] - s1[None] * s2[:, None, :]
    ei_t = -(c1[None] * s2[:, None, :] + s1[None] * c2[:, None, :])
    a_f = jnp.concatenate([er, ei], axis=1)
    erh, eih = er[:, :, :n1h], ei[:, :, :n1h]
    a_z = jnp.concatenate([jnp.concatenate([erh, -eih], axis=2),
                           jnp.concatenate([eih, erh], axis=2)], axis=1)
    erth, eith = er_t[:, :n1h, :], ei_t[:, :n1h, :]
    a_inv = jnp.concatenate([jnp.concatenate([erth, eith], axis=2),
                             jnp.concatenate([-eith, erth], axis=2)], axis=1) / n
    kk = jnp.arange(n2, dtype=jnp.int32)
    ang2 = ((kk[:, None] * kk[None, :]) % n2).astype(F32) * (-2.0 * math.pi / n2)
    fr, fi = jnp.cos(ang2), jnp.sin(ang2)
    f2 = jnp.concatenate([jnp.concatenate([fr, -fi], axis=1), jnp.concatenate([fi, fr], axis=1)], axis=0)
    return a_f.astype(BF16), a_z.astype(BF16), a_inv.astype(BF16), f2.astype(BF16), f2.T.astype(BF16)


DFT_BW = 512


def _to_slabs(x_ref, slab_ref, scale=None):
    rows, nb, w = x_ref.shape
    for lt in range(w // LANES):
        x = x_ref[:, :, lt * LANES:(lt + 1) * LANES]
        if scale is not None:
            x = x * scale[:, lt * LANES:(lt + 1) * LANES]
        slab_ref[lt] = x.reshape(rows * nb, LANES)


def _slab_rows(slab_ref, j, rows, nb):
    return jnp.concatenate([slab_ref[lt, pl.ds(j, rows, stride=nb), :] for lt in range(slab_ref.shape[0])], axis=1)


def _dfta_fwd_body(x_ref, a_ref, s_ref, o_ref, xs_ref, ys_ref):
    rows, nb, w = x_ref.shape
    m = o_ref.shape[0]
    _to_slabs(x_ref, xs_ref, s_ref[...])
    for j in range(nb):
        y = jnp.dot(a_ref[j], _slab_rows(xs_ref, j, rows, nb).astype(BF16), preferred_element_type=F32)
        for lt in range(w // LANES):
            ys_ref[lt, pl.ds(j, m, stride=nb), :] = y[:, lt * LANES:(lt + 1) * LANES]
    for lt in range(w // LANES):
        o_ref[:, :, lt * LANES:(lt + 1) * LANES] = ys_ref[lt].reshape(m, nb, LANES)


def _dfta_fwd(x3, a, scale):
    rows, n2, cw = x3.shape
    m = a.shape[1]
    nb, bwid = SUBLANES, DFT_BW
    return pl.pallas_call(
        _dfta_fwd_body,
        out_shape=jax.ShapeDtypeStruct((m, n2, cw), F32),
        grid=(n2 // nb, cw // bwid),
        in_specs=[pl.BlockSpec((rows, nb, bwid), lambda i, c: (0, i, c)),
                  pl.BlockSpec((nb, m, rows), lambda i, c: (i, 0, 0)),
                  pl.BlockSpec((1, bwid), lambda i, c: (0, c))],
        out_specs=pl.BlockSpec((m, nb, bwid), lambda i, c: (0, i, c)),
        scratch_shapes=[pltpu.VMEM((bwid // LANES, rows * nb, LANES), F32),
                        pltpu.VMEM((bwid // LANES, m * nb, LANES), F32)],
        compiler_params=_cparams(("parallel", "parallel")),
        name="dft_stage_a",
    )(x3, a, scale)


def _dfta_inv_body(u_ref, a_ref, xn_ref, z_ref, bias_ref, o_ref, us_ref, ys_ref):
    k, nb, w = u_ref.shape
    m = o_ref.shape[0]
    _to_slabs(u_ref, us_ref)
    for j in range(nb):
        y = jnp.dot(a_ref[j], _slab_rows(us_ref, j, k, nb).astype(BF16), preferred_element_type=F32)
        for lt in range(w // LANES):
            ys_ref[lt, pl.ds(j, m, stride=nb), :] = y[:, lt * LANES:(lt + 1) * LANES]
    for lt in range(w // LANES):
        sl = slice(lt * LANES, (lt + 1) * LANES)
        y = ys_ref[lt].reshape(m, nb, LANES)
        o_ref[:, :, sl] = xn_ref[:, :, sl] * (y + bias_ref[:, sl] * z_ref[:, :, sl])


def _dfta_inv(u3, a_inv, xn3, z3, bias):
    n2, m, k = a_inv.shape
    cw = u3.shape[2]
    nb, bwid = SUBLANES, DFT_BW
    return pl.pallas_call(
        _dfta_inv_body,
        out_shape=jax.ShapeDtypeStruct((m, n2, cw), F32),
        grid=(n2 // nb, cw // bwid),
        in_specs=[pl.BlockSpec((k, nb, bwid), lambda i, c: (0, i, c)),
                  pl.BlockSpec((nb, m, k), lambda i, c: (i, 0, 0)),
                  pl.BlockSpec((m, nb, bwid), lambda i, c: (0, i, c)),
                  pl.BlockSpec((m, nb, bwid), lambda i, c: (0, i, c)),
                  pl.BlockSpec((1, bwid), lambda i, c: (0, c))],
        out_specs=pl.BlockSpec((m, nb, bwid), lambda i, c: (0, i, c)),
        scratch_shapes=[pltpu.VMEM((bwid // LANES, k * nb, LANES), F32),
                        pltpu.VMEM((bwid // LANES, m * nb, LANES), F32)],
        compiler_params=_cparams(("parallel", "parallel")),
        name="dft_stage_a_inv",
    )(u3, a_inv, xn3, z3, bias)


def _filter_stage_a_body(f_ref, t_ref, w1_ref, b1_ref, f1_ref, w2_ref, b2_ref, f2_ref, w3_ref, rate_ref, a_ref,
                         o_ref, l1_ref, xs_ref, ys_ref):
    i = pl.program_id(0)
    m, nb, ow = o_ref.shape
    rn = xs_ref.shape[1]
    rows = rn // nb
    h = jnp.sin(f1_ref[...] * (_hdot(w1_ref[...], f_ref[...]) + b1_ref[...]))
    h = jnp.sin(f2_ref[...] * (_hdot(w2_ref[...], h) + b2_ref[...]))
    ht = h.T.astype(BF16)
    half = rn // 2
    x = jnp.concatenate([jnp.dot(ht[:half], w3_ref[:, :ow], preferred_element_type=F32),
                         jnp.dot(ht[half:], w3_ref[:, ow:], preferred_element_type=F32)], axis=0)
    dec = jnp.exp(-(t_ref[...] * rate_ref[...]))
    x = x * jnp.concatenate([dec] * B_ORDER, axis=1)
    r = lax.broadcasted_iota(jnp.int32, (rn, 1), 0)
    x = jnp.where(jnp.logical_and(i == 0, r == half), 0.0, x)

    @pl.when(i == 0)
    def _():
        l1_ref[...] = jnp.zeros_like(l1_ref)

    l1_ref[...] += jnp.sum(jnp.abs(x), axis=0, keepdims=True)
    for lt in range(ow // LANES):
        xs_ref[lt] = x[:, lt * LANES:(lt + 1) * LANES]
    for j in range(nb):
        y = jnp.dot(a_ref[j], _slab_rows(xs_ref, j, rows, nb).astype(BF16), preferred_element_type=F32)
        for lt in range(ow // LANES):
            ys_ref[lt, pl.ds(j, m, stride=nb), :] = y[:, lt * LANES:(lt + 1) * LANES]
    for lt in range(ow // LANES):
        o_ref[:, :, lt * LANES:(lt + 1) * LANES] = ys_ref[lt].reshape(m, nb, LANES)


def _filter_stage_a(w1, b1, f1, w2, b2, f2, w3, rates, a_f, length):
    n = 2 * length
    n1 = n // DFT_N2
    nb = SUBLANES
    nblk = DFT_N2 // nb
    ow = B_ORDER * rates.shape[1]
    order = jnp.arange(n).reshape(n1, nblk, nb).transpose(1, 0, 2).reshape(-1)
    feats_t, tcol = _hyena_feats(length, order)
    w1t = jnp.pad(w1, ((0, B_FFN - B_EMB), (0, 0))).T
    small = lambda shape: pl.BlockSpec(shape, lambda i: (0, 0))
    col = lambda v: v.reshape(-1, 1)
    return pl.pallas_call(
        _filter_stage_a_body,
        out_shape=[jax.ShapeDtypeStruct((2 * n1, DFT_N2, ow), F32), jax.ShapeDtypeStruct((1, ow), F32)],
        grid=(nblk,),
        in_specs=[pl.BlockSpec((B_FFN, n1 * nb), lambda i: (0, i)),
                  pl.BlockSpec((n1 * nb, 1), lambda i: (i, 0)),
                  small((B_FFN, B_FFN)), small((B_FFN, 1)), small((B_FFN, 1)),
                  small((B_FFN, B_FFN)), small((B_FFN, 1)), small((B_FFN, 1)),
                  small((B_FFN, 2 * ow)), small((1, ow // B_ORDER)),
                  pl.BlockSpec((nb, 2 * n1, n1), lambda i: (i, 0, 0))],
        out_specs=[pl.BlockSpec((2 * n1, nb, ow), lambda i: (0, i, 0)), pl.BlockSpec((1, ow), lambda i: (0, 0))],
        scratch_shapes=[pltpu.VMEM((ow // LANES, n1 * nb, LANES), F32),
                        pltpu.VMEM((ow // LANES, 2 * n1 * nb, LANES), F32)],
        compiler_params=_cparams(("arbitrary",)),
        name="filter_stage_a",
    )(feats_t, tcol, w1t, col(b1), col(f1), w2.T, col(b2), col(f2), w3.astype(BF16), rates, a_f)


def _dftc_filter_body(t_ref, f_ref, s_ref, o_ref, *, kb):
    for j in range(kb):
        t = t_ref[:, j]
        t2 = t.reshape(2 * DFT_N2, t.shape[-1])
        y = jnp.dot(f_ref[...], t2.astype(BF16), preferred_element_type=F32) * s_ref[...]
        o_ref[j] = y.reshape(2, DFT_N2, t.shape[-1]).astype(BF16)


def _dftc_filter(t4, f2, scale, kb):
    _, n1, n2, ow = t4.shape
    body = functools.partial(_dftc_filter_body, kb=kb)
    return pl.pallas_call(
        body,
        out_shape=jax.ShapeDtypeStruct((n1, 2, n2, ow), BF16),
        grid=(n1 // kb,),
        in_specs=[pl.BlockSpec((2, kb, n2, ow), lambda i: (0, i, 0, 0)),
                  pl.BlockSpec((2 * n2, 2 * n2), lambda i: (0, 0)),
                  pl.BlockSpec((1, ow), lambda i: (0, 0))],
        out_specs=pl.BlockSpec((kb, 2, n2, ow), lambda i: (i, 0, 0, 0)),
        compiler_params=_cparams(("parallel",)),
        name="dft_stage_c_filter",
    )(t4, f2, scale)


def _dftc_mid_body(t_ref, h_ref, f_ref, fi_ref, o_ref, *, kb):
    cw = t_ref.shape[-1]
    ys = [jnp.dot(f_ref[...], t_ref[:, j].reshape(2 * DFT_N2, cw).astype(BF16), preferred_element_type=F32)
          for j in range(kb)]
    ps = []
    for j, y in enumerate(ys):
        yr, yi = y[:DFT_N2], y[DFT_N2:]
        hr, hi = h_ref[j, 0].astype(F32), h_ref[j, 1].astype(F32)
        ps.append(jnp.concatenate([yr * hr - yi * hi, yr * hi + yi * hr], axis=0).astype(BF16))
    for j, p in enumerate(ps):
        u = jnp.dot(fi_ref[...], p, preferred_element_type=F32)
        o_ref[0, j] = u[:DFT_N2]
        o_ref[1, j] = u[DFT_N2:]


def _dftc_mid(t4, hspec, order, f2, f2inv, kb):
    _, n1, n2, cw = t4.shape
    body = functools.partial(_dftc_mid_body, kb=kb)
    return pl.pallas_call(
        body,
        out_shape=jax.ShapeDtypeStruct((2, n1, n2, cw), F32),
        grid=(n1 // kb,),
        in_specs=[pl.BlockSpec((2, kb, n2, cw), lambda i: (0, i, 0, 0)),
                  pl.BlockSpec((kb, 2, n2, cw), lambda i: (i, 0, 0, order)),
                  pl.BlockSpec((2 * n2, 2 * n2), lambda i: (0, 0)),
                  pl.BlockSpec((2 * n2, 2 * n2), lambda i: (0, 0))],
        out_specs=pl.BlockSpec((2, kb, n2, cw), lambda i: (0, i, 0, 0)),
        compiler_params=_cparams(("parallel",)),
        name="dft_stage_c_mid",
    )(t4, hspec, f2, f2inv)


def _hyena_long(parts, fargs, bias, tables):
    a_f, a_z, a_inv, f2, f2inv = tables
    bsz, length, cw = parts[0].shape
    assert bsz == 2, "the two batch rows ride as real and imaginary parts of one transform"
    n1 = 2 * length // DFT_N2
    ow = B_ORDER * cw
    tf, l1 = _filter_stage_a(*fargs, a_f, length)
    hspec = _dftc_filter(tf.reshape(2, n1, DFT_N2, ow), f2, 1.0 / l1, 2)
    ones = jnp.ones((1, cw), F32)
    z = parts[0].reshape(bsz * n1 // 2, DFT_N2, cw)
    for o in range(B_ORDER):
        t = _dfta_fwd(z, a_z, ones)
        u = _dftc_mid(t.reshape(2, n1, DFT_N2, cw), hspec, o, f2, f2inv, 4)
        z = _dfta_inv(u.reshape(2 * n1, DFT_N2, cw), a_inv, parts[o + 1].reshape(z.shape), z,
                      bias[o:o + 1])
    return z.reshape(bsz, length, cw)


def _ctx_conv_body(v_ref, x1_ref, x2_ref, filt_ref, il1_ref, bias_ref, ff_ref, fz_ref, fi_ref, o_ref, *, n, cw):
    hf = _hdot(ff_ref[...], filt_ref[...] * il1_ref[...])
    z = jnp.concatenate([v_ref[0], v_ref[1]], axis=0)
    for o, x_ref in enumerate((x1_ref, x2_ref)):
        zz = _hdot(fz_ref[...], z)
        zr, zi = zz[:n], zz[n:]
        hr, hi = hf[:n, o * cw:(o + 1) * cw], hf[n:, o * cw:(o + 1) * cw]
        y = _hdot(fi_ref[...], jnp.concatenate([zr * hr - zi * hi, zr * hi + zi * hr], axis=0))
        xn = jnp.concatenate([x_ref[0], x_ref[1]], axis=0)
        z = xn * (y + bias_ref[o:o + 1, :] * z)
    half = n // 2
    o_ref[0] = z[:half]
    o_ref[1] = z[half:]


def _hyena_ctx(parts, filt, l1, bias):
    bsz, length, cw = parts[0].shape
    assert bsz == 2
    n = 2 * length
    k = jnp.arange(n, dtype=jnp.int32)
    ang = ((k[:, None] * k[None, :]) % n).astype(F32) * (-2.0 * math.pi / n)
    cr, ci = jnp.cos(ang), jnp.sin(ang)
    ff = jnp.concatenate([cr, ci], axis=0)
    crh, cih = cr[:, :length], ci[:, :length]
    fz = jnp.concatenate([jnp.concatenate([crh, -cih], axis=1), jnp.concatenate([cih, crh], axis=1)], axis=0)
    fi = fz.T / n
    body = functools.partial(_ctx_conv_body, n=n, cw=cw)
    full = lambda a: pl.BlockSpec(a.shape, lambda i: (0,) * a.ndim)
    args = (parts[0], parts[1], parts[2], filt, 1.0 / l1, bias, ff, fz, fi)
    return pl.pallas_call(
        body,
        out_shape=jax.ShapeDtypeStruct((bsz, length, cw), F32),
        grid=(1,),
        in_specs=[full(a) for a in args],
        out_specs=pl.BlockSpec((bsz, length, cw), lambda i: (0, 0, 0)),
        compiler_params=_cparams(("arbitrary",)),
        name="hyena_ctx",
    )(*args)


def _rope_tables(length):
    n_rows = length // GRID_W
    row = jnp.repeat(jnp.arange(n_rows), GRID_W).astype(F32)
    col = jnp.tile(jnp.arange(GRID_W), n_rows).astype(F32)
    nf = C_HEAD_DIM // 4
    inv = ROPE_BASE ** (-jnp.arange(nf, dtype=F32) * 2.0 / (C_HEAD_DIM // 2))
    ar, ac = row[:, None] * inv, col[:, None] * inv
    cos_h = jnp.concatenate([jnp.cos(ar), jnp.cos(ar), jnp.cos(ac), jnp.cos(ac)], axis=-1)
    sin_h = jnp.concatenate([-jnp.sin(ar), jnp.sin(ar), -jnp.sin(ac), jnp.sin(ac)], axis=-1)
    reps = LANES // C_HEAD_DIM
    return jnp.tile(cos_h, (1, reps)), jnp.tile(sin_h, (1, reps))


def _softmax_heads(q_ref, keys, vals, sink_ref, masks, o_ref):
    rows = q_ref.shape[1]
    top = lax.broadcasted_iota(jnp.int32, (2 * rows, 1), 0) < rows
    chains = [(g, half) for g in range(C_KV_HEADS) for half in range(2)]
    off = lambda g, half: (2 * g + half) * LANES
    lane_blocks = lambda xs: [x[:, j * LANES:(j + 1) * LANES] for x in xs for j in range(x.shape[1] // LANES)]
    qs = [jnp.concatenate([q_ref[0, :, (2 * g) * LANES:(2 * g + 1) * LANES],
                           q_ref[0, :, (2 * g + 1) * LANES:(2 * g + 2) * LANES]], axis=0)
          for g in range(C_KV_HEADS)]
    scores, shifts, sinks = {}, {}, {}
    for g, half in chains:
        ss = []
        for k_ref, msk in zip(keys, masks):
            s = lax.dot_general(qs[g], k_ref[0, :, off(g, half):off(g, half) + LANES], (((1,), (1,)), ((), ())),
                                preferred_element_type=F32)
            ss.append(s if msk is None else jnp.where(jnp.concatenate([msk, msk], axis=0), s, NEG_BIG))
        sk = jnp.where(top, sink_ref[C_GROUP * g + half], sink_ref[C_GROUP * g + 2 + half])
        mx = functools.reduce(jnp.maximum, lane_blocks(ss))
        scores[g, half], sinks[g, half] = ss, sk
        shifts[g, half] = jnp.maximum(jnp.max(mx, axis=-1, keepdims=True), sk)
    probs, rdens = {}, {}
    for c in chains:
        ps = [jnp.exp(s - shifts[c]) for s in scores[c]]
        den = jnp.sum(functools.reduce(jnp.add, lane_blocks(ps)), axis=-1, keepdims=True) + jnp.exp(sinks[c] - shifts[c])
        probs[c], rdens[c] = [p.astype(BF16) for p in ps], 1.0 / den
    outs = {}
    for g, half in chains:
        o = None
        for p, v_ref in zip(probs[g, half], vals):
            t = jnp.dot(p, v_ref[0, :, off(g, half):off(g, half) + LANES], preferred_element_type=F32)
            o = t if o is None else o + t
        outs[g, half] = o * rdens[g, half]
    for g in range(C_KV_HEADS):
        acc = outs[g, 0] + outs[g, 1]
        o_ref[0, :, (2 * g) * LANES:(2 * g + 1) * LANES] = acc[:rows]
        o_ref[0, :, (2 * g + 1) * LANES:(2 * g + 2) * LANES] = acc[rows:]


def _attn_body(sink_ref, q_ref, kp_ref, kc_ref, kn_ref, kx_ref, vp_ref, vc_ref, vn_ref, vx_ref, o_ref):
    n = pl.program_id(1)
    blk = q_ref.shape[1]
    ri = lax.broadcasted_iota(jnp.int32, (blk, blk), 0)
    ci = lax.broadcasted_iota(jnp.int32, (blk, blk), 1)
    mask_p = jnp.logical_and(ci >= ri, n > 0)
    mask_n = jnp.logical_and(ci <= ri, n < pl.num_programs(1) - 1)
    _softmax_heads(q_ref, (kp_ref, kc_ref, kn_ref, kx_ref), (vp_ref, vc_ref, vn_ref, vx_ref), sink_ref,
                   (mask_p, None, mask_n, None), o_ref)


def _attention(q3, k3, v3, kx3, vx3, sink):
    bsz, length, wq = q3.shape
    wk = k3.shape[2]
    lc = kx3.shape[1]
    nb = length // C_BLOCK
    cur = lambda w: pl.BlockSpec((1, C_BLOCK, w), lambda b, i: (b, i, 0))
    prv = lambda w: pl.BlockSpec((1, C_BLOCK, w), lambda b, i: (b, jnp.maximum(i - 1, 0), 0))
    nxt = lambda w: pl.BlockSpec((1, C_BLOCK, w), lambda b, i: (b, jnp.minimum(i + 1, nb - 1), 0))
    ctx = pl.BlockSpec((1, lc, wk), lambda b, i: (b, 0, 0))
    return pl.pallas_call(
        _attn_body,
        out_shape=jax.ShapeDtypeStruct((bsz, length, wq), F32),
        grid=(bsz, nb),
        in_specs=[pl.BlockSpec(memory_space=pltpu.SMEM), cur(wq), prv(wk), cur(wk), nxt(wk), ctx,
                  prv(wk), cur(wk), nxt(wk), ctx],
        out_specs=cur(wq),
        compiler_params=_cparams(("parallel", "parallel")),
        name="window_attn",
    )(sink, q3, k3, k3, k3, kx3, v3, v3, v3, vx3)


def _ctx_attn_body(sink_ref, q_ref, k_ref, v_ref, o_ref):
    _softmax_heads(q_ref, (k_ref,), (v_ref,), sink_ref, (None,), o_ref)


def _ctx_attention(q3, k3, v3, sink):
    bsz, lc, wq = q3.shape
    wk = k3.shape[2]
    spec = lambda w: pl.BlockSpec((1, lc, w), lambda b: (b, 0, 0))
    return pl.pallas_call(
        _ctx_attn_body,
        out_shape=jax.ShapeDtypeStruct((bsz, lc, wq), F32),
        grid=(bsz,),
        in_specs=[pl.BlockSpec(memory_space=pltpu.SMEM), spec(wq), spec(wk), spec(wk)],
        out_specs=spec(wq),
        compiler_params=_cparams(("parallel",)),
        name="ctx_attn",
    )(sink, q3, k3, v3)


def _merge_body(h_ref, mod_ref, of_ref, ob_ref, ga_ref, yb_ref, yc_ref, wg_ref, wbr_ref, wo_ref, nw_ref,
                g_ref, b_ref, o_ref, *, alpha):
    h = h_ref[...]
    d = h.shape[1]
    u = (h * (1.0 + mod_ref[0, 4:5, :]) + mod_ref[0, 3:4, :]).astype(BF16)
    gates = [jnp.dot(u, wg_ref[:, n * d:(n + 1) * d], preferred_element_type=F32) for n in range(3)]
    branches = [jnp.dot(y_ref[...].astype(BF16), wbr_ref[n], preferred_element_type=F32)
                for n, y_ref in ((1, yb_ref), (2, yc_ref))]
    o = of_ref[...] + ob_ref[...]
    heads = []
    for hh in range(o.shape[1] // A_DK):
        oh = o[:, hh * A_DK:(hh + 1) * A_DK]
        ms = jnp.mean(oh * oh, axis=-1, keepdims=True)
        heads.append(oh * lax.rsqrt(ms + RMS_EPS) * nw_ref[:, hh * A_DK:(hh + 1) * A_DK])
    ya = jnp.concatenate(heads, axis=1) * _silu(ga_ref[...])
    branches.insert(0, jnp.dot(ya.astype(BF16), wbr_ref[0], preferred_element_type=F32))
    m = jax.nn.sigmoid(gates[0]) * branches[0]
    for n in range(1, 3):
        m = m + jax.nn.sigmoid(gates[n]) * branches[n]
    y = jnp.dot(m.astype(BF16), wo_ref[...], preferred_element_type=F32)
    r = alpha * h + mod_ref[0, 5:6, :] * y
    o_ref[...] = _layer_norm(r, g_ref[...], b_ref[...])


def _merge(h2, mod, of2, ob2, pa2, yb2, yc2, wg, wbr, wo, idx, norm_w, g, b, rows_per_mod, alpha, tm):
    r, d = h2.shape
    bw = of2.shape[1]
    tpm = rows_per_mod // tm
    body = functools.partial(_merge_body, alpha=alpha)
    row = lambda w: pl.BlockSpec((tm, w), lambda i: (i, 0))
    return pl.pallas_call(
        body,
        out_shape=jax.ShapeDtypeStruct((r, d), F32),
        grid=(r // tm,),
        in_specs=[row(d), pl.BlockSpec((1, N_MOD, d), lambda i: (i // tpm, 0, 0)),
                  row(bw), row(bw), pl.BlockSpec((tm, bw), lambda i: (i, 2)), row(bw), row(bw),
                  _layer_spec(wg, idx), _layer_spec(wbr, idx), _layer_spec(wo, idx),
                  _const_spec((1, bw)), _const_spec((1, d)), _const_spec((1, d))],
        out_specs=row(d),
        compiler_params=_cparams(("parallel",)),
        name="merge",
    )(h2, mod, of2, ob2, pa2, yb2, yc2, wg, wbr, wo, norm_w.reshape(1, bw), g.reshape(1, d), b.reshape(1, d))


def kernel(x, c, ctx, c_ctx, ada_w, ada_b, ln_g, ln_b, ffn_w_in, ffn_w_out, mix_w_in, hgrn_lb, hgrn_norm_w,
           hyena_conv_w, hyena_conv_b, hyena_w1, hyena_b1, hyena_f1, hyena_w2, hyena_b2, hyena_f2, hyena_w3,
           hyena_bias, attn_sink, branch_w, out_w):
    bsz, seq, d = x.shape
    lc = ctx.shape[1]
    depth = ada_w.shape[0]
    alpha = (2.0 * depth) ** 0.25
    bw = hgrn_lb.shape[2]
    wk = C_KV_HEADS * C_HEAD_DIM
    widths = (5 * bw, (B_ORDER + 1) * bw, bw, wk)
    off_g = widths[0] + widths[1] + widths[2] + 2 * wk
    tm = 1024
    tmp = 512
    tmc = min(256, lc)

    s = jax.nn.softmax(hgrn_lb.astype(F32), axis=0)
    lower_bounds = jnp.cumsum(s, axis=0) - s[0:1]

    c8 = jnp.zeros((SUBLANES, d), F32).at[:bsz].set(c).at[bsz].set(c_ctx)
    mods = _ada_mod(c8, ada_w, ada_b).reshape(depth, SUBLANES, N_MOD, d)

    cos_t, sin_t = _rope_tables(seq)
    feats_c, tcol_c = _hyena_feats(lc)
    rates = _decay_rates(bw)
    tables = _dft_tables(seq)

    w_in_bf = ffn_w_in.astype(BF16)
    w_out_bf = ffn_w_out.astype(BF16)
    proj_bf = mix_w_in[:, :, :off_g].astype(BF16)
    gate_bf = mix_w_in[:, :, off_g:].astype(BF16)
    br_bf = branch_w.astype(BF16)
    out_bf = out_w.astype(BF16)

    h = x.reshape(bsz * seq, d)
    hc = ctx.reshape(bsz * lc, d)
    for l in range(depth):
        last = l == depth - 1
        mod = mods[l, :bsz]
        modc = mods[l, bsz:bsz + 1]
        ffn = lambda t, mm, m0, j, g, rpm, tt: _ffn(t, mm, m0, w_in_bf, w_out_bf, (l, j), ln_g[l, g],
                                                     ln_b[l, g], rpm, alpha, tt)
        h = ffn(h, mod, 0, 0, 0, seq, tm)
        hc = ffn(hc, modc, 0, 0, 0, bsz * lc, tmc)

        conv = (hyena_conv_w[l], hyena_conv_b[l])
        pa, b0, b1, b2, pq, pk, pv = _inproj(h, mod, proj_bf, (l,), *conv, cos_t, sin_t, seq, seq, True, widths, tmp)
        ca, c0, c1, c2, cq, ck, cv = _inproj(hc, modc, proj_bf, (l,), *conv, cos_t, sin_t, bsz * lc, lc, False, widths,
                                             tmc)

        s0 = jnp.zeros((bsz, A_HEADS, 2, A_DK, A_DK), F32)
        ocf, ocb, s_ctx = _hgrn(ca.reshape(bsz, lc, -1), lower_bounds[l], s0, lc)
        of, ob, _ = _hgrn(pa.reshape(bsz, seq, -1), lower_bounds[l], s_ctx, 512)

        parts = [p.reshape(bsz, seq, bw) for p in (b0, b1, b2)]
        fargs = (hyena_w1[l], hyena_b1[l], hyena_f1[l], hyena_w2[l], hyena_b2[l], hyena_f2[l], hyena_w3[l], rates)
        yb = _hyena_long(parts, fargs, hyena_bias[l], tables)

        kx, vx = ck.reshape(bsz, lc, 4 * wk), cv.reshape(bsz, lc, 4 * wk)
        yc = _attention(pq.reshape(bsz, seq, bw), pk.reshape(bsz, seq, 4 * wk), pv.reshape(bsz, seq, 4 * wk),
                        kx, vx, attn_sink[l])

        merge = lambda t, mm, a1, a2, a3, a4, a5, rpm, tt: _merge(
            t, mm, a1, a2, a3, a4, a5, gate_bf, br_bf, out_bf, (l,), hgrn_norm_w[l], ln_g[l, 1], ln_b[l, 1],
            rpm, alpha, tt)
        h = merge(h, mod, of.reshape(-1, bw), ob.reshape(-1, bw), pa, yb.reshape(-1, bw), yc.reshape(-1, bw),
                  seq, tmp)
        h = ffn(h, mod, 6, 1, 2, seq, tm)
        if not last:
            cparts = [p.reshape(bsz, lc, bw) for p in (c0, c1, c2)]
            cfilt, cl1 = _hyena_filter(feats_c, tcol_c, *fargs, lc)
            ycb = _hyena_ctx(cparts, cfilt, cl1, hyena_bias[l])
            ycc = _ctx_attention(cq.reshape(bsz, lc, bw), kx, vx, attn_sink[l])
            hc = merge(hc, modc, ocf.reshape(-1, bw), ocb.reshape(-1, bw), ca, ycb.reshape(-1, bw),
                       ycc.reshape(-1, bw), bsz * lc, tmc)
            hc = ffn(hc, modc, 6, 1, 2, bsz * lc, tmc)
    return h.reshape(bsz, seq, d)
```

```python
import functools
import math

import jax
import jax.numpy as jnp
from jax import lax
from jax.experimental import pallas as pl
from jax.experimental.pallas import tpu as pltpu

F32 = jnp.float32
BF16 = jnp.bfloat16
HI = lax.Precision.HIGHEST

LANES = 128
SUBLANES = 8
VMEM_LIMIT = 56 * 1024 * 1024

N_MOD = 9
A_HEADS = 4
A_DK = 128
B_ORDER = 2
B_EMB = 33
B_BANDS = 16
B_FFN = 64
B_FAST_DECAY = 0.3
B_SLOW_DECAY = 1.5
B_TARGET = 1e-2
C_HEAD_DIM = 64
C_HEADS = 8
C_KV_HEADS = 2
C_GROUP = 4
C_BLOCK = 128
GRID_W = 64
ROPE_BASE = 10000.0
LN_EPS = 1e-5
RMS_EPS = 1e-6
DFT_N2 = 256
NEG_BIG = -1e30


def _cparams(sem, vmem=VMEM_LIMIT):
    return pltpu.CompilerParams(dimension_semantics=sem, vmem_limit_bytes=vmem)


def _const_spec(shape):
    nd = len(shape)
    return pl.BlockSpec(shape, lambda *_: (0,) * nd, pipeline_mode=pl.Buffered(1))


def _layer_spec(stacked, idx):
    tail = stacked.shape[len(idx):]
    return pl.BlockSpec((None,) * len(idx) + tail, lambda *_: tuple(idx) + (0,) * len(tail),
                        pipeline_mode=pl.Buffered(1))


def _bdot(a, b):
    return jnp.dot(a.astype(BF16), b.astype(BF16), preferred_element_type=F32)


def _bdot_nt(a, b):
    return lax.dot_general(a.astype(BF16), b.astype(BF16), (((1,), (1,)), ((), ())),
                           preferred_element_type=F32)


def _hdot(a, b):
    return jnp.dot(a, b, precision=HI, preferred_element_type=F32)


def _layer_norm(x, g, b):
    mu = jnp.mean(x, axis=-1, keepdims=True)
    xc = x - mu
    var = jnp.mean(xc * xc, axis=-1, keepdims=True)
    return xc * lax.rsqrt(var + LN_EPS) * g + b


def _silu(x):
    return x * jax.nn.sigmoid(x)


def _ada_body(c_ref, w_ref, b_ref, o_ref):
    o_ref[0] = _hdot(_silu(c_ref[...]), w_ref[0]) + b_ref[0]


def _ada_mod(c8, ada_w, ada_b):
    depth, d, nw = ada_w.shape
    tn = nw // 8
    return pl.pallas_call(
        _ada_body,
        out_shape=jax.ShapeDtypeStruct((depth, SUBLANES, nw), F32),
        grid=(depth, nw // tn),
        in_specs=[pl.BlockSpec((SUBLANES, d), lambda l, j: (0, 0)),
                  pl.BlockSpec((1, d, tn), lambda l, j: (l, 0, j)),
                  pl.BlockSpec((1, 1, tn), lambda l, j: (l, 0, j))],
        out_specs=pl.BlockSpec((1, SUBLANES, tn), lambda l, j: (l, 0, j)),
        compiler_params=_cparams(("parallel", "parallel")),
        name="ada_mod",
    )(c8, ada_w, ada_b.reshape(depth, 1, nw))


def _ffn_body(h_ref, mod_ref, win_ref, wout_ref, g_ref, b_ref, o_ref, *, m0, d_ff, fc, alpha):
    h = h_ref[...]
    u = (h * (1.0 + mod_ref[0, m0 + 1:m0 + 2, :]) + mod_ref[0, m0:m0 + 1, :]).astype(BF16)
    acc = jnp.zeros_like(h)
    for j in range(d_ff // fc):
        a = jnp.dot(u, win_ref[:, j * fc:(j + 1) * fc], preferred_element_type=F32)
        b = jnp.dot(u, win_ref[:, d_ff + j * fc:d_ff + (j + 1) * fc], preferred_element_type=F32)
        act = (_silu(a) * b).astype(BF16)
        acc = acc + jnp.dot(act, wout_ref[j * fc:(j + 1) * fc, :], preferred_element_type=F32)
    r = alpha * h + 0.5 * mod_ref[0, m0 + 2:m0 + 3, :] * acc
    o_ref[...] = _layer_norm(r, g_ref[...], b_ref[...])


def _ffn(h2, mod, m0, w_in, w_out, idx, g, b, rows_per_mod, alpha, tm):
    r, d = h2.shape
    d_ff = w_out.shape[-2]
    tpm = rows_per_mod // tm
    fc = 2 * LANES
    assert d_ff % fc == 0
    body = functools.partial(_ffn_body, m0=m0, d_ff=d_ff, fc=fc, alpha=alpha)
    return pl.pallas_call(
        body,
        out_shape=jax.ShapeDtypeStruct((r, d), F32),
        grid=(r // tm,),
        in_specs=[pl.BlockSpec((tm, d), lambda i: (i, 0)),
                  pl.BlockSpec((1, N_MOD, d), lambda i: (i // tpm, 0, 0)),
                  _layer_spec(w_in, idx), _layer_spec(w_out, idx),
                  _const_spec((1, d)), _const_spec((1, d))],
        out_specs=pl.BlockSpec((tm, d), lambda i: (i, 0)),
        compiler_params=_cparams(("parallel",)),
        name="ffn",
    )(h2, mod, w_in, w_out, g.reshape(1, d), b.reshape(1, d))


def _swap16(x):
    lane = lax.broadcasted_iota(jnp.int32, x.shape, 1)
    return jnp.where((lane % 32) < 16, pltpu.roll(x, LANES - 16, 1), pltpu.roll(x, 16, 1))


def _head_pair_variants(x):
    lane = lax.broadcasted_iota(jnp.int32, x.shape, 1)
    low = lane < C_HEAD_DIM
    xs = pltpu.roll(x, C_HEAD_DIM, 1)
    parts = (jnp.where(low, x, 0.0), jnp.where(low, 0.0, xs), jnp.where(low, xs, 0.0), jnp.where(low, 0.0, x))
    return jnp.concatenate(parts, axis=1).astype(BF16)


def _inproj_body(h_ref, hp_ref, hn_ref, mod_ref, w_ref, cw_ref, cb_ref, cos_ref, sin_ref,
                 pa_ref, b0_ref, b1_ref, b2_ref, pq_ref, pk_ref, pv_ref, *, wa, wb, wq, wk, rope, tps):
    assert wk == LANES
    i = pl.program_id(0)
    tm = h_ref.shape[0]
    scale1 = 1.0 + mod_ref[0, 4:5, :]
    shift = mod_ref[0, 3:4, :]
    u = (h_ref[...] * scale1 + shift).astype(BF16)
    wbm = w_ref[:, wa:wa + wb]
    pb = jnp.dot(u, wbm, preferred_element_type=F32)
    up = (hp_ref[...] * scale1 + shift).astype(BF16)
    un = (hn_ref[...] * scale1 + shift).astype(BF16)
    prev_row = jnp.dot(up, wbm, preferred_element_type=F32)[SUBLANES - 1:SUBLANES, :]
    next_row = jnp.dot(un, wbm, preferred_element_type=F32)[0:1, :]
    prev_row = jnp.where(i % tps > 0, prev_row, 0.0)
    next_row = jnp.where(i % tps < tps - 1, next_row, 0.0)
    oq = wa + wb
    q = jnp.dot(u, w_ref[:, oq:oq + wq], preferred_element_type=F32)
    k = jnp.dot(u, w_ref[:, oq + wq:oq + wq + wk], preferred_element_type=F32)
    v = jnp.dot(u, w_ref[:, oq + wq + wk:oq + wq + 2 * wk], preferred_element_type=F32)
    pa_ref[...] = jnp.dot(u, w_ref[:, 0:wa], preferred_element_type=F32)
    row = lax.broadcasted_iota(jnp.int32, (tm, 1), 0)
    xm = jnp.where(row == 0, prev_row, pltpu.roll(pb, 1, 0))
    xp = jnp.where(row == tm - 1, next_row, pltpu.roll(pb, tm - 1, 0))
    y = xm * cw_ref[0:1, :] + pb * cw_ref[1:2, :] + xp * cw_ref[2:3, :] + cb_ref[...]
    cw = wb // 3
    b0_ref[...] = y[:, 0:cw]
    b1_ref[...] = y[:, cw:2 * cw]
    b2_ref[...] = y[:, 2 * cw:3 * cw]
    scale = C_HEAD_DIM ** -0.5
    if rope:
        cs = cos_ref[...]
        sn = sin_ref[...]
        for j in range(wq // LANES):
            xq = q[:, j * LANES:(j + 1) * LANES]
            pq_ref[:, j * LANES:(j + 1) * LANES] = ((xq * cs + _swap16(xq) * sn) * scale).astype(BF16)
        k = k * cs + _swap16(k) * sn
    else:
        pq_ref[...] = (q * scale).astype(BF16)
    pk_ref[...] = _head_pair_variants(k)
    pv_ref[...] = _head_pair_variants(v)


def _inproj(h2, mod, w, idx, conv_w, conv_b, cos_t, sin_t, rows_per_mod, rows_per_seq, rope, widths, tm):
    r, d = h2.shape
    wa, wb, wq, wk = widths
    tpm = rows_per_mod // tm
    tps = rows_per_seq // tm
    r8 = tm // SUBLANES
    n8 = r // SUBLANES
    body = functools.partial(_inproj_body, wa=wa, wb=wb, wq=wq, wk=wk, rope=rope, tps=tps)
    ow = (wa, wb // 3, wb // 3, wb // 3, wq, 4 * wk, 4 * wk)
    od = (F32, F32, F32, F32, BF16, BF16, BF16)
    return pl.pallas_call(
        body,
        out_shape=[jax.ShapeDtypeStruct((r, n), t) for n, t in zip(ow, od)],
        grid=(r // tm,),
        in_specs=[pl.BlockSpec((tm, d), lambda i: (i, 0)),
                  pl.BlockSpec((SUBLANES, d), lambda i: (jnp.maximum(i * r8 - 1, 0), 0)),
                  pl.BlockSpec((SUBLANES, d), lambda i: (jnp.minimum((i + 1) * r8, n8 - 1), 0)),
                  pl.BlockSpec((1, N_MOD, d), lambda i: (i // tpm, 0, 0)),
                  _layer_spec(w, idx), _const_spec((3, wb)), _const_spec((1, wb)),
                  pl.BlockSpec((tm, LANES), lambda i: (i % tps, 0)),
                  pl.BlockSpec((tm, LANES), lambda i: (i % tps, 0))],
        out_specs=[pl.BlockSpec((tm, n), lambda i: (i, 0)) for n in ow],
        compiler_params=_cparams(("parallel",)),
        name="inproj",
    )(h2, h2, h2, mod, w, conv_w, conv_b.reshape(1, wb), cos_t, sin_t)


HGRN_CHUNK = 128
HGRN_LEVEL_GROUPS = ((64, 32), (16, 8), (4, 2, 1))


def _pivot_rows(b_ref, c, r0):
    t, w = b_ref.shape
    g = 2 * c
    if g >= SUBLANES:
        rows = [jnp.broadcast_to(b_ref[s + r0:s + r0 + 1, :], (g, w)) for s in range(0, t, g)]
        return rows[0] if len(rows) == 1 else jnp.concatenate(rows, axis=0)
    b = b_ref[...]
    row = lax.broadcasted_iota(jnp.int32, (t, 1), 0) % g
    out = b
    for m in range(g):
        if m != r0:
            out = jnp.where(row == m, pltpu.roll(b, (m - r0) % t, 0), out)
    return out


def _split3_bf16(x):
    c = 65537.0
    t = c * x
    hi = t - (t - x)
    rest = x - hi
    t = c * rest
    mid = t - (t - rest)
    lo = rest - mid
    return hi.astype(BF16), mid.astype(BF16), lo.astype(BF16)


def _hgrn_gates(z, lb, tri, b_ref, k_ref):
    e = jnp.exp(-jnp.abs(z))
    r = 1.0 / (1.0 + e)
    pos = z >= 0.0
    sig_p = jnp.where(pos, r, e * r)
    sig_n = jnp.where(pos, e * r, r)
    logf = jnp.log2(lb + (1.0 - lb) * sig_p)
    k_ref[...] = ((1.0 - lb) * sig_n).astype(BF16)
    bb = jnp.dot(tri, jnp.concatenate(_split3_bf16(logf), axis=1), preferred_element_type=F32)
    w = z.shape[1]
    b_ref[...] = bb[:, 0:w] + bb[:, w:2 * w] + bb[:, 2 * w:3 * w]


def _hgrn_diag(q, k_ref, qh_ref, att_ref):
    t = q.shape[0]
    qh_ref[...] = q.astype(BF16)
    ri = lax.broadcasted_iota(jnp.int32, (t, t), 0)
    ci = lax.broadcasted_iota(jnp.int32, (t, t), 1)
    att_ref[...] = jnp.where(ri == ci, jnp.sum(q * k_ref[...].astype(F32), axis=-1, keepdims=True), 0.0)


def _hgrn_levels(lvl, sgn_ref, b_ref, k_ref, qh_ref, att_ref, cs, reverse):
    b, qh, kh = b_ref[...], qh_ref[...], k_ref[...]
    prods = []
    for c in cs:
        level = c.bit_length() - 1
        piv = _pivot_rows(b_ref, c, c if reverse else c - 1)
        dec = jnp.exp2((b - piv) * sgn_ref[level]).astype(BF16)
        prods.append((level, _bdot_nt(qh * dec, kh * dec)))
    att = att_ref[...]
    for level, a in prods:
        att = jnp.where(lvl == level, a, att)
    att_ref[...] = att


def _hgrn_finish(q, v, st, b_ref, k_ref, att_ref, reverse):
    t = q.shape[0]
    b = b_ref[...]
    vb = v.astype(BF16)
    out = _bdot_nt(q * jnp.exp2(b), st) + jnp.dot(att_ref[...].astype(BF16), vb, preferred_element_type=F32)
    blast = b[0:1, :] if reverse else b[t - 1:t, :]
    ke = k_ref[...] * jnp.exp2(blast - b).astype(BF16)
    return out, st * jnp.exp2(blast) + jnp.dot(vb.T, ke, preferred_element_type=F32)


def _hgrn_body(qf_ref, vf_ref, zf_ref, qb_ref, vb_ref, zb_ref, lb_ref, s0_ref, tri_ref, lvl_ref, sgn_ref,
               of_ref, ob_ref, sfin_ref, st_ref, b_ref, k_ref, qh_ref, att_ref, *, nc, nh):
    i = pl.program_id(1)

    @pl.when(i == 0)
    def _():
        st_ref[...] = s0_ref[0]

    def step(j, carry):
        fwd = pl.ds(pl.multiple_of(j * HGRN_CHUNK, HGRN_CHUNK), HGRN_CHUNK)
        bwd = pl.ds(pl.multiple_of((nc - 1 - j) * HGRN_CHUNK, HGRN_CHUNK), HGRN_CHUNK)
        streams = ((qf_ref, vf_ref, zf_ref, of_ref, fwd), (qb_ref, vb_ref, zb_ref, ob_ref, bwd))
        chains = [(h, d) for h in range(nh) for d in range(2)]
        cols = lambda h: slice(h * A_DK, (h + 1) * A_DK)
        for h, d in chains:
            z_r, rows = streams[d][2], streams[d][4]
            _hgrn_gates(z_r[0, rows, cols(h)], lb_ref[d:d + 1, cols(h)], tri_ref[d], b_ref.at[h, d], k_ref.at[h, d])
        for h, d in chains:
            q_r, rows = streams[d][0], streams[d][4]
            _hgrn_diag(q_r[0, rows, cols(h)], k_ref.at[h, d], qh_ref.at[h, d], att_ref.at[h, d])
        for cs in HGRN_LEVEL_GROUPS:
            for h, d in chains:
                _hgrn_levels(lvl_ref[d], sgn_ref.at[d], b_ref.at[h, d], k_ref.at[h, d], qh_ref.at[h, d],
                             att_ref.at[h, d], cs, reverse=(d == 1))
        for h, d in chains:
            q_r, v_r, _, o_r, rows = streams[d]
            out, st_new = _hgrn_finish(q_r[0, rows, cols(h)], v_r[0, rows, cols(h)], st_ref[h, d], b_ref.at[h, d],
                                       k_ref.at[h, d], att_ref.at[h, d], reverse=(d == 1))
            o_r[0, rows, cols(h)] = out
            st_ref[h, d] = st_new
        return carry

    lax.fori_loop(0, nc, step, 0)

    @pl.when(i == pl.num_programs(1) - 1)
    def _():
        sfin_ref[0] = st_ref[...]


def _hgrn(pa3, lb, s0, t):
    bsz, length, _ = pa3.shape
    w = lb.shape[1]
    nh = w // A_DK
    n = length // t
    ii = jnp.arange(HGRN_CHUNK, dtype=jnp.int32)
    tri = jnp.stack([ii[:, None] >= ii[None, :], ii[:, None] <= ii[None, :]]).astype(BF16)
    top = 31 - lax.clz(ii[:, None] ^ ii[None, :])
    lvl = jnp.stack([jnp.where(ii[:, None] > ii[None, :], top, -1), jnp.where(ii[:, None] < ii[None, :], top, -1)])
    nlev = HGRN_CHUNK.bit_length() - 1
    later = ((ii[None, :, None] >> jnp.arange(nlev, dtype=jnp.int32)[:, None, None]) & 1) == 1
    sgn_f = jnp.broadcast_to(jnp.where(later, 1.0, -1.0).astype(F32), (nlev, HGRN_CHUNK, A_DK))
    sgn = jnp.stack([sgn_f, -sgn_f])

    def col(off, rev):
        if rev:
            return pl.BlockSpec((1, t, w), lambda b, i: (b, n - 1 - i, off))
        return pl.BlockSpec((1, t, w), lambda b, i: (b, i, off))

    st_spec = pl.BlockSpec((1, nh, 2, A_DK, A_DK), lambda b, i: (b, 0, 0, 0, 0))
    sq = pl.BlockSpec((2, HGRN_CHUNK, HGRN_CHUNK), lambda b, i: (0, 0, 0))
    return pl.pallas_call(
        functools.partial(_hgrn_body, nc=t // HGRN_CHUNK, nh=nh),
        out_shape=[jax.ShapeDtypeStruct((bsz, length, w), F32),
                   jax.ShapeDtypeStruct((bsz, length, w), F32),
                   jax.ShapeDtypeStruct((bsz, nh, 2, A_DK, A_DK), F32)],
        grid=(bsz, n),
        in_specs=[col(0, False), col(1, False), col(3, False),
                  col(0, True), col(1, True), col(4, True),
                  pl.BlockSpec((2, w), lambda b, i: (0, 0)),
                  st_spec, sq, sq,
                  pl.BlockSpec((2, nlev, HGRN_CHUNK, A_DK), lambda b, i: (0, 0, 0, 0))],
        out_specs=[col(0, False), col(0, True), st_spec],
        scratch_shapes=[pltpu.VMEM((nh, 2, A_DK, A_DK), F32),
                        pltpu.VMEM((nh, 2, HGRN_CHUNK, A_DK), F32),
                        pltpu.VMEM((nh, 2, HGRN_CHUNK, A_DK), BF16),
                        pltpu.VMEM((nh, 2, HGRN_CHUNK, A_DK), BF16),
                        pltpu.VMEM((nh, 2, HGRN_CHUNK, HGRN_CHUNK), F32)],
        compiler_params=_cparams(("parallel", "arbitrary")),
        name="hgrn",
    )(pa3, pa3, pa3, pa3, pa3, pa3, lb, s0, tri, lvl, sgn)


def _hyena_feats(length, order=None):
    n = jnp.arange(2 * length) if order is None else order
    pos = jnp.where(n < length, n, 2 * length - n).astype(F32)
    t = pos / (length - 1)
    w = 2.0 * math.pi * pos / length
    bands = jnp.linspace(1e-4, B_BANDS - 1, B_BANDS, dtype=F32)
    feats = jnp.concatenate([t[None, :], jnp.cos(bands[:, None] * w[None, :]), -jnp.sin(bands[:, None] * w[None, :])],
                            axis=0)
    return jnp.pad(feats, ((0, B_FFN - B_EMB), (0, 0))), t[:, None]


def _decay_rates(width):
    max_decay = math.log(B_TARGET) / B_FAST_DECAY
    min_decay = math.log(B_TARGET) / B_SLOW_DECAY
    return jnp.abs(jnp.linspace(min_decay, max_decay, width, dtype=F32)).reshape(1, width)


def _filter_body(f_ref, t_ref, w1_ref, b1_ref, f1_ref, w2_ref, b2_ref, f2_ref, w3_ref, rate_ref, o_ref, l1_ref,
                 *, tr, length):
    i = pl.program_id(0)
    h = jnp.sin(f1_ref[...] * (_hdot(w1_ref[...], f_ref[...]) + b1_ref[...]))
    h = jnp.sin(f2_ref[...] * (_hdot(w2_ref[...], h) + b2_ref[...]))
    h = jnp.dot(h.T.astype(BF16), w3_ref[...], preferred_element_type=F32)
    dec = jnp.exp(-(t_ref[...] * rate_ref[...]))
    h = h * jnp.concatenate([dec] * B_ORDER, axis=1)
    row = i * tr + lax.broadcasted_iota(jnp.int32, (tr, 1), 0)
    h = jnp.where(row == length, 0.0, h)
    o_ref[...] = h

    @pl.when(i == 0)
    def _():
        l1_ref[...] = jnp.zeros_like(l1_ref)

    l1_ref[...] += jnp.sum(jnp.abs(h), axis=0, keepdims=True)


def _hyena_filter(feats_t, tcol, w1, b1, f1, w2, b2, f2, w3, rates, length):
    width = rates.shape[1]
    ow = B_ORDER * width
    tr = min(512, length)
    nblk = 2 * length // tr
    w1t = jnp.pad(w1, ((0, B_FFN - B_EMB), (0, 0))).T
    body = functools.partial(_filter_body, tr=tr, length=length)
    small = lambda shape: pl.BlockSpec(shape, lambda i: (0, 0))
    col = lambda v: v.reshape(-1, 1)
    return pl.pallas_call(
        body,
        out_shape=[jax.ShapeDtypeStruct((2 * length, ow), F32), jax.ShapeDtypeStruct((1, ow), F32)],
        grid=(nblk,),
        in_specs=[pl.BlockSpec((B_FFN, tr), lambda i: (0, i)),
                  pl.BlockSpec((tr, 1), lambda i: (i, 0)),
                  small((B_FFN, B_FFN)), small((B_FFN, 1)), small((B_FFN, 1)),
                  small((B_FFN, B_FFN)), small((B_FFN, 1)), small((B_FFN, 1)),
                  pl.BlockSpec((B_FFN, ow), lambda i: (0, (2 * i) // nblk)),
                  small((1, width))],
        out_specs=[pl.BlockSpec((tr, ow), lambda i: (i, 0)), pl.BlockSpec((1, ow), lambda i: (0, 0))],
        compiler_params=_cparams(("arbitrary",)),
        name="hyena_filter",
    )(feats_t, tcol, w1t, col(b1), col(f1), w2.T, col(b2), col(f2), w3.astype(BF16), rates)


def _dft_tables(length):
    n = 2 * length
    n2 = DFT_N2
    n1 = n // n2
    n1h = n1 // 2
    k1 = jnp.arange(n1, dtype=jnp.int32)
    j2 = jnp.arange(n2, dtype=jnp.int32)
    ang1 = ((k1[:, None] * k1[None, :]) % n1).astype(F32) * (2.0 * math.pi / n1)
    ang2 = (j2[:, None] * k1[None, :]).astype(F32) * (2.0 * math.pi / n)
    c1, s1, c2, s2 = jnp.cos(ang1), jnp.sin(ang1), jnp.cos(ang2), jnp.sin(ang2)
    er = c1[None] * c2[:, :, None] - s1[None] * s2[:, :, None]
    ei = -(c1[None] * s2[:, :, None] + s1[None] * c2[:, :, None])
    er_t = c1[None] * c2[:, None, :] - s1[None] * s2[:, None, :]
    ei_t = -(c1[None] * s2[:, None, :] + s1[None] * c2[:, None, :])
    a_f = jnp.concatenate([er, ei], axis=1)
    erh, eih = er[:, :, :n1h], ei[:, :, :n1h]
    a_z = jnp.concatenate([jnp.concatenate([erh, -eih], axis=2),
                           jnp.concatenate([eih, erh], axis=2)], axis=1)
    erth, eith = er_t[:, :n1h, :], ei_t[:, :n1h, :]
    a_inv = jnp.concatenate([jnp.concatenate([erth, eith], axis=2),
                             jnp.concatenate([-eith, erth], axis=2)], axis=1) / n
    kk = jnp.arange(n2, dtype=jnp.int32)
    ang2 = ((kk[:, None] * kk[None, :]) % n2).astype(F32) * (-2.0 * math.pi / n2)
    fr, fi = jnp.cos(ang2), jnp.sin(ang2)
    f2 = jnp.concatenate([jnp.concatenate([fr, -fi], axis=1), jnp.concatenate([fi, fr], axis=1)], axis=0)
    return a_f.astype(BF16), a_z.astype(BF16), a_inv.astype(BF16), f2.astype(BF16), f2.T.astype(BF16)


DFT_BW = 512


def _to_slabs(x_ref, slab_ref, scale=None):
    rows, nb, w = x_ref.shape
    for lt in range(w // LANES):
        x = x_ref[:, :, lt * LANES:(lt + 1) * LANES]
        if scale is not None:
            x = x * scale[:, lt * LANES:(lt + 1) * LANES]
        slab_ref[lt] = x.reshape(rows * nb, LANES)


def _slab_rows(slab_ref, j, rows, nb):
    return jnp.concatenate([slab_ref[lt, pl.ds(j, rows, stride=nb), :] for lt in range(slab_ref.shape[0])], axis=1)


def _dfta_fwd_body(x_ref, a_ref, s_ref, o_ref, xs_ref, ys_ref):
    rows, nb, w = x_ref.shape
    m = o_ref.shape[0]
    _to_slabs(x_ref, xs_ref, s_ref[...])
    for j in range(nb):
        y = jnp.dot(a_ref[j], _slab_rows(xs_ref, j, rows, nb).astype(BF16), preferred_element_type=F32)
        for lt in range(w // LANES):
            ys_ref[lt, pl.ds(j, m, stride=nb), :] = y[:, lt * LANES:(lt + 1) * LANES]
    for lt in range(w // LANES):
        o_ref[:, :, lt * LANES:(lt + 1) * LANES] = ys_ref[lt].reshape(m, nb, LANES)


def _dfta_fwd(x3, a, scale):
    rows, n2, cw = x3.shape
    m = a.shape[1]
    nb, bwid = SUBLANES, DFT_BW
    return pl.pallas_call(
        _dfta_fwd_body,
        out_shape=jax.ShapeDtypeStruct((m, n2, cw), F32),
        grid=(n2 // nb, cw // bwid),
        in_specs=[pl.BlockSpec((rows, nb, bwid), lambda i, c: (0, i, c)),
                  pl.BlockSpec((nb, m, rows), lambda i, c: (i, 0, 0)),
                  pl.BlockSpec((1, bwid), lambda i, c: (0, c))],
        out_specs=pl.BlockSpec((m, nb, bwid), lambda i, c: (0, i, c)),
        scratch_shapes=[pltpu.VMEM((bwid // LANES, rows * nb, LANES), F32),
                        pltpu.VMEM((bwid // LANES, m * nb, LANES), F32)],
        compiler_params=_cparams(("parallel", "parallel")),
        name="dft_stage_a",
    )(x3, a, scale)


def _dfta_inv_body(u_ref, a_ref, xn_ref, z_ref, bias_ref, o_ref, us_ref, ys_ref):
    k, nb, w = u_ref.shape
    m = o_ref.shape[0]
    _to_slabs(u_ref, us_ref)
    for j in range(nb):
        y = jnp.dot(a_ref[j], _slab_rows(us_ref, j, k, nb).astype(BF16), preferred_element_type=F32)
        for lt in range(w // LANES):
            ys_ref[lt, pl.ds(j, m, stride=nb), :] = y[:, lt * LANES:(lt + 1) * LANES]
    for lt in range(w // LANES):
        sl = slice(lt * LANES, (lt + 1) * LANES)
        y = ys_ref[lt].reshape(m, nb, LANES)
        o_ref[:, :, sl] = xn_ref[:, :, sl] * (y + bias_ref[:, sl] * z_ref[:, :, sl])


def _dfta_inv(u3, a_inv, xn3, z3, bias):
    n2, m, k = a_inv.shape
    cw = u3.shape[2]
    nb, bwid = SUBLANES, DFT_BW
    return pl.pallas_call(
        _dfta_inv_body,
        out_shape=jax.ShapeDtypeStruct((m, n2, cw), F32),
        grid=(n2 // nb, cw // bwid),
        in_specs=[pl.BlockSpec((k, nb, bwid), lambda i, c: (0, i, c)),
                  pl.BlockSpec((nb, m, k), lambda i, c: (i, 0, 0)),
                  pl.BlockSpec((m, nb, bwid), lambda i, c: (0, i, c)),
                  pl.BlockSpec((m, nb, bwid), lambda i, c: (0, i, c)),
                  pl.BlockSpec((1, bwid), lambda i, c: (0, c))],
        out_specs=pl.BlockSpec((m, nb, bwid), lambda i, c: (0, i, c)),
        scratch_shapes=[pltpu.VMEM((bwid // LANES, k * nb, LANES), F32),
                        pltpu.VMEM((bwid // LANES, m * nb, LANES), F32)],
        compiler_params=_cparams(("parallel", "parallel")),
        name="dft_stage_a_inv",
    )(u3, a_inv, xn3, z3, bias)


def _dfta_inv_fwd_body(u_ref, ai_ref, af_ref, xn_ref, z_ref, bias_ref, o_ref, t_ref, us_ref, ys_ref, ts_ref):
    k, nb, w = u_ref.shape
    m = o_ref.shape[0]
    _to_slabs(u_ref, us_ref)
    for j in range(nb):
        y = jnp.dot(ai_ref[j], _slab_rows(us_ref, j, k, nb).astype(BF16), preferred_element_type=F32)
        for lt in range(w // LANES):
            ys_ref[lt, pl.ds(j, m, stride=nb), :] = y[:, lt * LANES:(lt + 1) * LANES]
    for lt in range(w // LANES):
        sl = slice(lt * LANES, (lt + 1) * LANES)
        zn = xn_ref[:, :, sl] * (ys_ref[lt].reshape(m, nb, LANES) + bias_ref[:, sl] * z_ref[:, :, sl])
        o_ref[:, :, sl] = zn
        ys_ref[lt] = zn.reshape(m * nb, LANES)
    for j in range(nb):
        t = jnp.dot(af_ref[j], _slab_rows(ys_ref, j, m, nb).astype(BF16), preferred_element_type=F32)
        for lt in range(w // LANES):
            ts_ref[lt, pl.ds(j, k, stride=nb), :] = t[:, lt * LANES:(lt + 1) * LANES]
    for lt in range(w // LANES):
        t_ref[:, :, lt * LANES:(lt + 1) * LANES] = ts_ref[lt].reshape(k, nb, LANES)


def _dfta_inv_fwd(u3, a_inv, a_fwd, xn3, z3, bias):
    n2, m, k = a_inv.shape
    cw = u3.shape[2]
    nb, bwid = SUBLANES, DFT_BW
    blk = lambda r: pl.BlockSpec((r, nb, bwid), lambda i, c: (0, i, c))
    slab = lambda r: pltpu.VMEM((bwid // LANES, r * nb, LANES), F32)
    return pl.pallas_call(
        _dfta_inv_fwd_body,
        out_shape=[jax.ShapeDtypeStruct((m, n2, cw), F32), jax.ShapeDtypeStruct((k, n2, cw), F32)],
        grid=(n2 // nb, cw // bwid),
        in_specs=[blk(k),
                  pl.BlockSpec((nb, m, k), lambda i, c: (i, 0, 0)),
                  pl.BlockSpec((nb, k, m), lambda i, c: (i, 0, 0)),
                  blk(m), blk(m),
                  pl.BlockSpec((1, bwid), lambda i, c: (0, c))],
        out_specs=[blk(m), blk(k)],
        scratch_shapes=[slab(k), slab(m), slab(k)],
        compiler_params=_cparams(("parallel", "parallel")),
        name="dft_stage_a_inv_fwd",
    )(u3, a_inv, a_fwd, xn3, z3, bias)


def _filter_stage_a_body(f_ref, t_ref, w1_ref, b1_ref, f1_ref, w2_ref, b2_ref, f2_ref, w3_ref, rate_ref, a_ref,
                         o_ref, l1_ref, xs_ref, ys_ref):
    i = pl.program_id(0)
    m, nb, ow = o_ref.shape
    rn = xs_ref.shape[1]
    rows = rn // nb
    h = jnp.sin(f1_ref[...] * (_hdot(w1_ref[...], f_ref[...]) + b1_ref[...]))
    h = jnp.sin(f2_ref[...] * (_hdot(w2_ref[...], h) + b2_ref[...]))
    ht = h.T.astype(BF16)
    half = rn // 2
    x = jnp.concatenate([jnp.dot(ht[:half], w3_ref[:, :ow], preferred_element_type=F32),
                         jnp.dot(ht[half:], w3_ref[:, ow:], preferred_element_type=F32)], axis=0)
    dec = jnp.exp(-(t_ref[...] * rate_ref[...]))
    x = x * jnp.concatenate([dec] * B_ORDER, axis=1)
    r = lax.broadcasted_iota(jnp.int32, (rn, 1), 0)
    x = jnp.where(jnp.logical_and(i == 0, r == half), 0.0, x)

    @pl.when(i == 0)
    def _():
        l1_ref[...] = jnp.zeros_like(l1_ref)

    l1_ref[...] += jnp.sum(jnp.abs(x), axis=0, keepdims=True)
    for lt in range(ow // LANES):
        xs_ref[lt] = x[:, lt * LANES:(lt + 1) * LANES]
    for j in range(nb):
        y = jnp.dot(a_ref[j], _slab_rows(xs_ref, j, rows, nb).astype(BF16), preferred_element_type=F32)
        for lt in range(ow // LANES):
            ys_ref[lt, pl.ds(j, m, stride=nb), :] = y[:, lt * LANES:(lt + 1) * LANES]
    for lt in range(ow // LANES):
        o_ref[:, :, lt * LANES:(lt + 1) * LANES] = ys_ref[lt].reshape(m, nb, LANES)


def _filter_stage_a(w1, b1, f1, w2, b2, f2, w3, rates, a_f, length):
    n = 2 * length
    n1 = n // DFT_N2
    nb = SUBLANES
    nblk = DFT_N2 // nb
    ow = B_ORDER * rates.shape[1]
    order = jnp.arange(n).reshape(n1, nblk, nb).transpose(1, 0, 2).reshape(-1)
    feats_t, tcol = _hyena_feats(length, order)
    w1t = jnp.pad(w1, ((0, B_FFN - B_EMB), (0, 0))).T
    small = lambda shape: pl.BlockSpec(shape, lambda i: (0, 0))
    col = lambda v: v.reshape(-1, 1)
    return pl.pallas_call(
        _filter_stage_a_body,
        out_shape=[jax.ShapeDtypeStruct((2 * n1, DFT_N2, ow), F32), jax.ShapeDtypeStruct((1, ow), F32)],
        grid=(nblk,),
        in_specs=[pl.BlockSpec((B_FFN, n1 * nb), lambda i: (0, i)),
                  pl.BlockSpec((n1 * nb, 1), lambda i: (i, 0)),
                  small((B_FFN, B_FFN)), small((B_FFN, 1)), small((B_FFN, 1)),
                  small((B_FFN, B_FFN)), small((B_FFN, 1)), small((B_FFN, 1)),
                  small((B_FFN, 2 * ow)), small((1, ow // B_ORDER)),
                  pl.BlockSpec((nb, 2 * n1, n1), lambda i: (i, 0, 0))],
        out_specs=[pl.BlockSpec((2 * n1, nb, ow), lambda i: (0, i, 0)), pl.BlockSpec((1, ow), lambda i: (0, 0))],
        scratch_shapes=[pltpu.VMEM((ow // LANES, n1 * nb, LANES), F32),
                        pltpu.VMEM((ow // LANES, 2 * n1 * nb, LANES), F32)],
        compiler_params=_cparams(("arbitrary",)),
        name="filter_stage_a",
    )(feats_t, tcol, w1t, col(b1), col(f1), w2.T, col(b2), col(f2), w3.astype(BF16), rates, a_f)


def _dftc_filter_body(t_ref, f_ref, s_ref, o_ref, *, kb):
    for j in range(kb):
        t = t_ref[:, j]
        t2 = t.reshape(2 * DFT_N2, t.shape[-1])
        y = jnp.dot(f_ref[...], t2.astype(BF16), preferred_element_type=F32) * s_ref[...]
        o_ref[j] = y.reshape(2, DFT_N2, t.shape[-1]).astype(BF16)


def _dftc_filter(t4, f2, scale, kb):
    _, n1, n2, ow = t4.shape
    body = functools.partial(_dftc_filter_body, kb=kb)
    return pl.pallas_call(
        body,
        out_shape=jax.ShapeDtypeStruct((n1, 2, n2, ow), BF16),
        grid=(n1 // kb,),
        in_specs=[pl.BlockSpec((2, kb, n2, ow), lambda i: (0, i, 0, 0)),
                  pl.BlockSpec((2 * n2, 2 * n2), lambda i: (0, 0)),
                  pl.BlockSpec((1, ow), lambda i: (0, 0))],
        out_specs=pl.BlockSpec((kb, 2, n2, ow), lambda i: (i, 0, 0, 0)),
        compiler_params=_cparams(("parallel",)),
        name="dft_stage_c_filter",
    )(t4, f2, scale)


def _dftc_mid_body(t_ref, h_ref, f_ref, fi_ref, o_ref, *, kb):
    cw = t_ref.shape[-1]
    ys = [jnp.dot(f_ref[...], t_ref[:, j].reshape(2 * DFT_N2, cw).astype(BF16), preferred_element_type=F32)
          for j in range(kb)]
    ps = []
    for j, y in enumerate(ys):
        yr, yi = y[:DFT_N2], y[DFT_N2:]
        hr, hi = h_ref[j, 0].astype(F32), h_ref[j, 1].astype(F32)
        ps.append(jnp.concatenate([yr * hr - yi * hi, yr * hi + yi * hr], axis=0).astype(BF16))
    for j, p in enumerate(ps):
        u = jnp.dot(fi_ref[...], p, preferred_element_type=F32)
        o_ref[0, j] = u[:DFT_N2]
        o_ref[1, j] = u[DFT_N2:]


def _dftc_mid(t4, hspec, order, f2, f2inv, kb):
    _, n1, n2, cw = t4.shape
    body = functools.partial(_dftc_mid_body, kb=kb)
    return pl.pallas_call(
        body,
        out_shape=jax.ShapeDtypeStruct((2, n1, n2, cw), F32),
        grid=(n1 // kb,),
        in_specs=[pl.BlockSpec((2, kb, n2, cw), lambda i: (0, i, 0, 0)),
                  pl.BlockSpec((kb, 2, n2, cw), lambda i: (i, 0, 0, order)),
                  pl.BlockSpec((2 * n2, 2 * n2), lambda i: (0, 0)),
                  pl.BlockSpec((2 * n2, 2 * n2), lambda i: (0, 0))],
        out_specs=pl.BlockSpec((2, kb, n2, cw), lambda i: (0, i, 0, 0)),
        compiler_params=_cparams(("parallel",)),
        name="dft_stage_c_mid",
    )(t4, hspec, f2, f2inv)


def _hyena_long(parts, fargs, bias, tables):
    a_f, a_z, a_inv, f2, f2inv = tables
    bsz, length, cw = parts[0].shape
    assert bsz == 2, "the two batch rows ride as real and imaginary parts of one transform"
    n1 = 2 * length // DFT_N2
    ow = B_ORDER * cw
    tf, l1 = _filter_stage_a(*fargs, a_f, length)
    hspec = _dftc_filter(tf.reshape(2, n1, DFT_N2, ow), f2, 1.0 / l1, 2)
    ones = jnp.ones((1, cw), F32)
    z = parts[0].reshape(bsz * n1 // 2, DFT_N2, cw)
    t = _dfta_fwd(z, a_z, ones)
    for o in range(B_ORDER):
        u = _dftc_mid(t.reshape(2, n1, DFT_N2, cw), hspec, o, f2, f2inv, 4).reshape(2 * n1, DFT_N2, cw)
        xn = parts[o + 1].reshape(z.shape)
        if o + 1 < B_ORDER:
            z, t = _dfta_inv_fwd(u, a_inv, a_z, xn, z, bias[o:o + 1])
        else:
            z = _dfta_inv(u, a_inv, xn, z, bias[o:o + 1])
    return z.reshape(bsz, length, cw)


def _ctx_conv_body(v_ref, x1_ref, x2_ref, filt_ref, il1_ref, bias_ref, ff_ref, fz_ref, fi_ref, o_ref, *, n, cw):
    hf = _hdot(ff_ref[...], filt_ref[...] * il1_ref[...])
    z = jnp.concatenate([v_ref[0], v_ref[1]], axis=0)
    for o, x_ref in enumerate((x1_ref, x2_ref)):
        zz = _hdot(fz_ref[...], z)
        zr, zi = zz[:n], zz[n:]
        hr, hi = hf[:n, o * cw:(o + 1) * cw], hf[n:, o * cw:(o + 1) * cw]
        y = _hdot(fi_ref[...], jnp.concatenate([zr * hr - zi * hi, zr * hi + zi * hr], axis=0))
        xn = jnp.concatenate([x_ref[0], x_ref[1]], axis=0)
        z = xn * (y + bias_ref[o:o + 1, :] * z)
    half = n // 2
    o_ref[0] = z[:half]
    o_ref[1] = z[half:]


def _hyena_ctx(parts, filt, l1, bias):
    bsz, length, cw = parts[0].shape
    assert bsz == 2
    n = 2 * length
    k = jnp.arange(n, dtype=jnp.int32)
    ang = ((k[:, None] * k[None, :]) % n).astype(F32) * (-2.0 * math.pi / n)
    cr, ci = jnp.cos(ang), jnp.sin(ang)
    ff = jnp.concatenate([cr, ci], axis=0)
    crh, cih = cr[:, :length], ci[:, :length]
    fz = jnp.concatenate([jnp.concatenate([crh, -cih], axis=1), jnp.concatenate([cih, crh], axis=1)], axis=0)
    fi = fz.T / n
    body = functools.partial(_ctx_conv_body, n=n, cw=cw)
    full = lambda a: pl.BlockSpec(a.shape, lambda i: (0,) * a.ndim)
    args = (parts[0], parts[1], parts[2], filt, 1.0 / l1, bias, ff, fz, fi)
    return pl.pallas_call(
        body,
        out_shape=jax.ShapeDtypeStruct((bsz, length, cw), F32),
        grid=(1,),
        in_specs=[full(a) for a in args],
        out_specs=pl.BlockSpec((bsz, length, cw), lambda i: (0, 0, 0)),
        compiler_params=_cparams(("arbitrary",)),
        name="hyena_ctx",
    )(*args)


def _rope_tables(length):
    n_rows = length // GRID_W
    row = jnp.repeat(jnp.arange(n_rows), GRID_W).astype(F32)
    col = jnp.tile(jnp.arange(GRID_W), n_rows).astype(F32)
    nf = C_HEAD_DIM // 4
    inv = ROPE_BASE ** (-jnp.arange(nf, dtype=F32) * 2.0 / (C_HEAD_DIM // 2))
    ar, ac = row[:, None] * inv, col[:, None] * inv
    cos_h = jnp.concatenate([jnp.cos(ar), jnp.cos(ar), jnp.cos(ac), jnp.cos(ac)], axis=-1)
    sin_h = jnp.concatenate([-jnp.sin(ar), jnp.sin(ar), -jnp.sin(ac), jnp.sin(ac)], axis=-1)
    reps = LANES // C_HEAD_DIM
    return jnp.tile(cos_h, (1, reps)), jnp.tile(sin_h, (1, reps))


def _softmax_heads(q_ref, keys, vals, sink_ref, masks, o_ref):
    rows = q_ref.shape[1]
    top = lax.broadcasted_iota(jnp.int32, (2 * rows, 1), 0) < rows
    chains = [(g, half) for g in range(C_KV_HEADS) for half in range(2)]
    off = lambda g, half: (2 * g + half) * LANES
    lane_blocks = lambda xs: [x[:, j * LANES:(j + 1) * LANES] for x in xs for j in range(x.shape[1] // LANES)]
    qs = [jnp.concatenate([q_ref[0, :, (2 * g) * LANES:(2 * g + 1) * LANES],
                           q_ref[0, :, (2 * g + 1) * LANES:(2 * g + 2) * LANES]], axis=0)
          for g in range(C_KV_HEADS)]
    scores, shifts, sinks = {}, {}, {}
    for g, half in chains:
        ss = []
        for k_ref, msk in zip(keys, masks):
            s = lax.dot_general(qs[g], k_ref[0, :, off(g, half):off(g, half) + LANES], (((1,), (1,)), ((), ())),
                                preferred_element_type=F32)
            ss.append(s if msk is None else jnp.where(jnp.concatenate([msk, msk], axis=0), s, NEG_BIG))
        sk = jnp.where(top, sink_ref[C_GROUP * g + half], sink_ref[C_GROUP * g + 2 + half])
        mx = functools.reduce(jnp.maximum, lane_blocks(ss))
        scores[g, half], sinks[g, half] = ss, sk
        shifts[g, half] = jnp.maximum(jnp.max(mx, axis=-1, keepdims=True), sk)
    probs, rdens = {}, {}
    for c in chains:
        ps = [jnp.exp(s - shifts[c]) for s in scores[c]]
        den = jnp.sum(functools.reduce(jnp.add, lane_blocks(ps)), axis=-1, keepdims=True) + jnp.exp(sinks[c] - shifts[c])
        probs[c], rdens[c] = [p.astype(BF16) for p in ps], 1.0 / den
    outs = {}
    for g, half in chains:
        o = None
        for p, v_ref in zip(probs[g, half], vals):
            t = jnp.dot(p, v_ref[0, :, off(g, half):off(g, half) + LANES], preferred_element_type=F32)
            o = t if o is None else o + t
        outs[g, half] = o * rdens[g, half]
    for g in range(C_KV_HEADS):
        acc = outs[g, 0] + outs[g, 1]
        o_ref[0, :, (2 * g) * LANES:(2 * g + 1) * LANES] = acc[:rows]
        o_ref[0, :, (2 * g + 1) * LANES:(2 * g + 2) * LANES] = acc[rows:]


def _attn_body(sink_ref, q_ref, kp_ref, kc_ref, kn_ref, kx_ref, vp_ref, vc_ref, vn_ref, vx_ref, o_ref):
    n = pl.program_id(1)
    blk = q_ref.shape[1]
    ri = lax.broadcasted_iota(jnp.int32, (blk, blk), 0)
    ci = lax.broadcasted_iota(jnp.int32, (blk, blk), 1)
    mask_p = jnp.logical_and(ci >= ri, n > 0)
    mask_n = jnp.logical_and(ci <= ri, n < pl.num_programs(1) - 1)
    _softmax_heads(q_ref, (kp_ref, kc_ref, kn_ref, kx_ref), (vp_ref, vc_ref, vn_ref, vx_ref), sink_ref,
                   (mask_p, None, mask_n, None), o_ref)


def _attention(q3, k3, v3, kx3, vx3, sink):
    bsz, length, wq = q3.shape
    wk = k3.shape[2]
    lc = kx3.shape[1]
    nb = length // C_BLOCK
    cur = lambda w: pl.BlockSpec((1, C_BLOCK, w), lambda b, i: (b, i, 0))
    prv = lambda w: pl.BlockSpec((1, C_BLOCK, w), lambda b, i: (b, jnp.maximum(i - 1, 0), 0))
    nxt = lambda w: pl.BlockSpec((1, C_BLOCK, w), lambda b, i: (b, jnp.minimum(i + 1, nb - 1), 0))
    ctx = pl.BlockSpec((1, lc, wk), lambda b, i: (b, 0, 0))
    return pl.pallas_call(
        _attn_body,
        out_shape=jax.ShapeDtypeStruct((bsz, length, wq), F32),
        grid=(bsz, nb),
        in_specs=[pl.BlockSpec(memory_space=pltpu.SMEM), cur(wq), prv(wk), cur(wk), nxt(wk), ctx,
                  prv(wk), cur(wk), nxt(wk), ctx],
        out_specs=cur(wq),
        compiler_params=_cparams(("parallel", "parallel")),
        name="window_attn",
    )(sink, q3, k3, k3, k3, kx3, v3, v3, v3, vx3)


def _ctx_attn_body(sink_ref, q_ref, k_ref, v_ref, o_ref):
    _softmax_heads(q_ref, (k_ref,), (v_ref,), sink_ref, (None,), o_ref)


def _ctx_attention(q3, k3, v3, sink):
    bsz, lc, wq = q3.shape
    wk = k3.shape[2]
    spec = lambda w: pl.BlockSpec((1, lc, w), lambda b: (b, 0, 0))
    return pl.pallas_call(
        _ctx_attn_body,
        out_shape=jax.ShapeDtypeStruct((bsz, lc, wq), F32),
        grid=(bsz,),
        in_specs=[pl.BlockSpec(memory_space=pltpu.SMEM), spec(wq), spec(wk), spec(wk)],
        out_specs=spec(wq),
        compiler_params=_cparams(("parallel",)),
        name="ctx_attn",
    )(sink, q3, k3, v3)


def _merge_body(h_ref, mod_ref, of_ref, ob_ref, ga_ref, yb_ref, yc_ref, wg_ref, wbr_ref, wo_ref, nw_ref,
                g_ref, b_ref, o_ref, *, alpha):
    h = h_ref[...]
    d = h.shape[1]
    u = (h * (1.0 + mod_ref[0, 4:5, :]) + mod_ref[0, 3:4, :]).astype(BF16)
    gates = [jnp.dot(u, wg_ref[:, n * d:(n + 1) * d], preferred_element_type=F32) for n in range(3)]
    branches = [jnp.dot(y_ref[...].astype(BF16), wbr_ref[n], preferred_element_type=F32)
                for n, y_ref in ((1, yb_ref), (2, yc_ref))]
    o = of_ref[...] + ob_ref[...]
    heads = []
    for hh in range(o.shape[1] // A_DK):
        oh = o[:, hh * A_DK:(hh + 1) * A_DK]
        ms = jnp.mean(oh * oh, axis=-1, keepdims=True)
        heads.append(oh * lax.rsqrt(ms + RMS_EPS) * nw_ref[:, hh * A_DK:(hh + 1) * A_DK])
    ya = jnp.concatenate(heads, axis=1) * _silu(ga_ref[...])
    branches.insert(0, jnp.dot(ya.astype(BF16), wbr_ref[0], preferred_element_type=F32))
    m = jax.nn.sigmoid(gates[0]) * branches[0]
    for n in range(1, 3):
        m = m + jax.nn.sigmoid(gates[n]) * branches[n]
    y = jnp.dot(m.astype(BF16), wo_ref[...], preferred_element_type=F32)
    r = alpha * h + mod_ref[0, 5:6, :] * y
    o_ref[...] = _layer_norm(r, g_ref[...], b_ref[...])


def _merge(h2, mod, of2, ob2, pa2, yb2, yc2, wg, wbr, wo, idx, norm_w, g, b, rows_per_mod, alpha, tm):
    r, d = h2.shape
    bw = of2.shape[1]
    tpm = rows_per_mod // tm
    body = functools.partial(_merge_body, alpha=alpha)
    row = lambda w: pl.BlockSpec((tm, w), lambda i: (i, 0))
    return pl.pallas_call(
        body,
        out_shape=jax.ShapeDtypeStruct((r, d), F32),
        grid=(r // tm,),
        in_specs=[row(d), pl.BlockSpec((1, N_MOD, d), lambda i: (i // tpm, 0, 0)),
                  row(bw), row(bw), pl.BlockSpec((tm, bw), lambda i: (i, 2)), row(bw), row(bw),
                  _layer_spec(wg, idx), _layer_spec(wbr, idx), _layer_spec(wo, idx),
                  _const_spec((1, bw)), _const_spec((1, d)), _const_spec((1, d))],
        out_specs=row(d),
        compiler_params=_cparams(("parallel",)),
        name="merge",
    )(h2, mod, of2, ob2, pa2, yb2, yc2, wg, wbr, wo, norm_w.reshape(1, bw), g.reshape(1, d), b.reshape(1, d))


def kernel(x, c, ctx, c_ctx, ada_w, ada_b, ln_g, ln_b, ffn_w_in, ffn_w_out, mix_w_in, hgrn_lb, hgrn_norm_w,
           hyena_conv_w, hyena_conv_b, hyena_w1, hyena_b1, hyena_f1, hyena_w2, hyena_b2, hyena_f2, hyena_w3,
           hyena_bias, attn_sink, branch_w, out_w):
    bsz, seq, d = x.shape
    lc = ctx.shape[1]
    depth = ada_w.shape[0]
    alpha = (2.0 * depth) ** 0.25
    bw = hgrn_lb.shape[2]
    wk = C_KV_HEADS * C_HEAD_DIM
    widths = (5 * bw, (B_ORDER + 1) * bw, bw, wk)
    off_g = widths[0] + widths[1] + widths[2] + 2 * wk
    tm = 1024
    tmp = 512
    tmc = min(256, lc)

    s = jax.nn.softmax(hgrn_lb.astype(F32), axis=0)
    lower_bounds = jnp.cumsum(s, axis=0) - s[0:1]

    c8 = jnp.zeros((SUBLANES, d), F32).at[:bsz].set(c).at[bsz].set(c_ctx)
    mods = _ada_mod(c8, ada_w, ada_b).reshape(depth, SUBLANES, N_MOD, d)

    cos_t, sin_t = _rope_tables(seq)
    feats_c, tcol_c = _hyena_feats(lc)
    rates = _decay_rates(bw)
    tables = _dft_tables(seq)

    w_in_bf = ffn_w_in.astype(BF16)
    w_out_bf = ffn_w_out.astype(BF16)
    proj_bf = mix_w_in[:, :, :off_g].astype(BF16)
    gate_bf = mix_w_in[:, :, off_g:].astype(BF16)
    br_bf = branch_w.astype(BF16)
    out_bf = out_w.astype(BF16)

    h = x.reshape(bsz * seq, d)
    hc = ctx.reshape(bsz * lc, d)
    for l in range(depth):
        last = l == depth - 1
        mod = mods[l, :bsz]
        modc = mods[l, bsz:bsz + 1]
        ffn = lambda t, mm, m0, j, g, rpm, tt: _ffn(t, mm, m0, w_in_bf, w_out_bf, (l, j), ln_g[l, g],
                                                     ln_b[l, g], rpm, alpha, tt)
        h = ffn(h, mod, 0, 0, 0, seq, tm)
        hc = ffn(hc, modc, 0, 0, 0, bsz * lc, tmc)

        conv = (hyena_conv_w[l], hyena_conv_b[l])
        pa, b0, b1, b2, pq, pk, pv = _inproj(h, mod, proj_bf, (l,), *conv, cos_t, sin_t, seq, seq, True, widths, tmp)
        ca, c0, c1, c2, cq, ck, cv = _inproj(hc, modc, proj_bf, (l,), *conv, cos_t, sin_t, bsz * lc, lc, False, widths,
                                             tmc)

        s0 = jnp.zeros((bsz, A_HEADS, 2, A_DK, A_DK), F32)
        ocf, ocb, s_ctx = _hgrn(ca.reshape(bsz, lc, -1), lower_bounds[l], s0, lc)
        of, ob, _ = _hgrn(pa.reshape(bsz, seq, -1), lower_bounds[l], s_ctx, min(1024, seq))

        parts = [p.reshape(bsz, seq, bw) for p in (b0, b1, b2)]
        fargs = (hyena_w1[l], hyena_b1[l], hyena_f1[l], hyena_w2[l], hyena_b2[l], hyena_f2[l], hyena_w3[l], rates)
        yb = _hyena_long(parts, fargs, hyena_bias[l], tables)

        kx, vx = ck.reshape(bsz, lc, 4 * wk), cv.reshape(bsz, lc, 4 * wk)
        yc = _attention(pq.reshape(bsz, seq, bw), pk.reshape(bsz, seq, 4 * wk), pv.reshape(bsz, seq, 4 * wk),
                        kx, vx, attn_sink[l])

        merge = lambda t, mm, a1, a2, a3, a4, a5, rpm, tt: _merge(
            t, mm, a1, a2, a3, a4, a5, gate_bf, br_bf, out_bf, (l,), hgrn_norm_w[l], ln_g[l, 1], ln_b[l, 1],
            rpm, alpha, tt)
        h = merge(h, mod, of.reshape(-1, bw), ob.reshape(-1, bw), pa, yb.reshape(-1, bw), yc.reshape(-1, bw),
                  seq, tmp)
        h = ffn(h, mod, 6, 1, 2, seq, tm)
        if not last:
            cparts = [p.reshape(bsz, lc, bw) for p in (c0, c1, c2)]
            cfilt, cl1 = _hyena_filter(feats_c, tcol_c, *fargs, lc)
            ycb = _hyena_ctx(cparts, cfilt, cl1, hyena_bias[l])
            ycc = _ctx_attention(cq.reshape(bsz, lc, bw), kx, vx, attn_sink[l])
            hc = merge(hc, modc, ocf.reshape(-1, bw), ocb.reshape(-1, bw), ca, ycb.reshape(-1, bw),
                       ycc.reshape(-1, bw), bsz * lc, tmc)
            hc = ffn(hc, modc, 6, 1, 2, bsz * lc, tmc)
    return h.reshape(bsz, seq, d)
```

```python
import functools
import math

import jax
import jax.numpy as jnp
from jax import lax
from jax.experimental import pallas as pl
from jax.experimental.pallas import tpu as pltpu

F32 = jnp.float32
BF16 = jnp.bfloat16
HI = lax.Precision.HIGHEST

LANES = 128
SUBLANES = 8
VMEM_LIMIT = 56 * 1024 * 1024

N_MOD = 9
A_HEADS = 4
A_DK = 128
B_ORDER = 2
B_EMB = 33
B_BANDS = 16
B_FFN = 64
B_FAST_DECAY = 0.3
B_SLOW_DECAY = 1.5
B_TARGET = 1e-2
C_HEAD_DIM = 64
C_HEADS = 8
C_KV_HEADS = 2
C_GROUP = 4
C_BLOCK = 128
GRID_W = 64
ROPE_BASE = 10000.0
LN_EPS = 1e-5
RMS_EPS = 1e-6
DFT_N2 = 256
NEG_BIG = -1e30


def _cparams(sem, vmem=VMEM_LIMIT):
    return pltpu.CompilerParams(dimension_semantics=sem, vmem_limit_bytes=vmem)


def _const_spec(shape):
    nd = len(shape)
    return pl.BlockSpec(shape, lambda *_: (0,) * nd, pipeline_mode=pl.Buffered(1))


def _layer_spec(stacked, idx):
    tail = stacked.shape[len(idx):]
    return pl.BlockSpec((None,) * len(idx) + tail, lambda *_: tuple(idx) + (0,) * len(tail),
                        pipeline_mode=pl.Buffered(1))


def _bdot(a, b):
    return jnp.dot(a.astype(BF16), b.astype(BF16), preferred_element_type=F32)


def _bdot_nt(a, b):
    return lax.dot_general(a.astype(BF16), b.astype(BF16), (((1,), (1,)), ((), ())),
                           preferred_element_type=F32)


def _hdot(a, b):
    return jnp.dot(a, b, precision=HI, preferred_element_type=F32)


def _layer_norm(x, g, b):
    mu = jnp.mean(x, axis=-1, keepdims=True)
    xc = x - mu
    var = jnp.mean(xc * xc, axis=-1, keepdims=True)
    return xc * lax.rsqrt(var + LN_EPS) * g + b


def _silu(x):
    return x * jax.nn.sigmoid(x)


def _ada_body(c_ref, w_ref, b_ref, o_ref):
    o_ref[0] = _hdot(_silu(c_ref[...]), w_ref[0]) + b_ref[0]


def _ada_mod(c8, ada_w, ada_b):
    depth, d, nw = ada_w.shape
    tn = nw // 8
    return pl.pallas_call(
        _ada_body,
        out_shape=jax.ShapeDtypeStruct((depth, SUBLANES, nw), F32),
        grid=(depth, nw // tn),
        in_specs=[pl.BlockSpec((SUBLANES, d), lambda l, j: (0, 0)),
                  pl.BlockSpec((1, d, tn), lambda l, j: (l, 0, j)),
                  pl.BlockSpec((1, 1, tn), lambda l, j: (l, 0, j))],
        out_specs=pl.BlockSpec((1, SUBLANES, tn), lambda l, j: (l, 0, j)),
        compiler_params=_cparams(("parallel", "parallel")),
        name="ada_mod",
    )(c8, ada_w, ada_b.reshape(depth, 1, nw))


def _ffn_body(h_ref, mod_ref, win_ref, wout_ref, g_ref, b_ref, o_ref, *, m0, d_ff, fc, alpha):
    h = h_ref[...]
    u = (h * (1.0 + mod_ref[0, m0 + 1:m0 + 2, :]) + mod_ref[0, m0:m0 + 1, :]).astype(BF16)
    acc = jnp.zeros_like(h)
    for j in range(d_ff // fc):
        a = jnp.dot(u, win_ref[:, j * fc:(j + 1) * fc], preferred_element_type=F32)
        b = jnp.dot(u, win_ref[:, d_ff + j * fc:d_ff + (j + 1) * fc], preferred_element_type=F32)
        act = (_silu(a) * b).astype(BF16)
        acc = acc + jnp.dot(act, wout_ref[j * fc:(j + 1) * fc, :], preferred_element_type=F32)
    r = alpha * h + 0.5 * mod_ref[0, m0 + 2:m0 + 3, :] * acc
    o_ref[...] = _layer_norm(r, g_ref[...], b_ref[...])


def _ffn(h2, mod, m0, w_in, w_out, idx, g, b, rows_per_mod, alpha, tm):
    r, d = h2.shape
    d_ff = w_out.shape[-2]
    tpm = rows_per_mod // tm
    fc = 2 * LANES
    assert d_ff % fc == 0
    body = functools.partial(_ffn_body, m0=m0, d_ff=d_ff, fc=fc, alpha=alpha)
    return pl.pallas_call(
        body,
        out_shape=jax.ShapeDtypeStruct((r, d), F32),
        grid=(r // tm,),
        in_specs=[pl.BlockSpec((tm, d), lambda i: (i, 0)),
                  pl.BlockSpec((1, N_MOD, d), lambda i: (i // tpm, 0, 0)),
                  _layer_spec(w_in, idx), _layer_spec(w_out, idx),
                  _const_spec((1, d)), _const_spec((1, d))],
        out_specs=pl.BlockSpec((tm, d), lambda i: (i, 0)),
        compiler_params=_cparams(("parallel",)),
        name="ffn",
    )(h2, mod, w_in, w_out, g.reshape(1, d), b.reshape(1, d))


def _swap16(x):
    lane = lax.broadcasted_iota(jnp.int32, x.shape, 1)
    return jnp.where((lane % 32) < 16, pltpu.roll(x, LANES - 16, 1), pltpu.roll(x, 16, 1))


def _head_pair_variants(x):
    lane = lax.broadcasted_iota(jnp.int32, x.shape, 1)
    low = lane < C_HEAD_DIM
    xs = pltpu.roll(x, C_HEAD_DIM, 1)
    parts = (jnp.where(low, x, 0.0), jnp.where(low, 0.0, xs), jnp.where(low, xs, 0.0), jnp.where(low, 0.0, x))
    return jnp.concatenate(parts, axis=1).astype(BF16)


def _inproj_body(h_ref, hp_ref, hn_ref, mod_ref, w_ref, cw_ref, cb_ref, cos_ref, sin_ref,
                 pa_ref, b0_ref, b1_ref, b2_ref, pq_ref, pk_ref, pv_ref, *, wa, wb, wq, wk, rope, tps):
    assert wk == LANES
    i = pl.program_id(0)
    tm = h_ref.shape[0]
    scale1 = 1.0 + mod_ref[0, 4:5, :]
    shift = mod_ref[0, 3:4, :]
    ucat = (jnp.concatenate([h_ref[...], hp_ref[...], hn_ref[...]], axis=0) * scale1 + shift).astype(BF16)
    u = ucat[:tm]
    pb_all = jnp.dot(ucat, w_ref[:, wa:wa + wb], preferred_element_type=F32)
    pb = pb_all[:tm]
    prev_row = jnp.where(i % tps > 0, pb_all[tm + SUBLANES - 1:tm + SUBLANES, :], 0.0)
    next_row = jnp.where(i % tps < tps - 1, pb_all[tm + SUBLANES:tm + SUBLANES + 1, :], 0.0)
    oq = wa + wb
    q = jnp.dot(u, w_ref[:, oq:oq + wq], preferred_element_type=F32)
    kv = jnp.dot(u, w_ref[:, oq + wq:oq + wq + 2 * wk], preferred_element_type=F32)
    k, v = kv[:, :wk], kv[:, wk:]
    pa_ref[...] = jnp.dot(u, w_ref[:, 0:wa], preferred_element_type=F32)
    row = lax.broadcasted_iota(jnp.int32, (tm, 1), 0)
    xm = jnp.where(row == 0, prev_row, pltpu.roll(pb, 1, 0))
    xp = jnp.where(row == tm - 1, next_row, pltpu.roll(pb, tm - 1, 0))
    y = xm * cw_ref[0:1, :] + pb * cw_ref[1:2, :] + xp * cw_ref[2:3, :] + cb_ref[...]
    cw = wb // 3
    b0_ref[...] = y[:, 0:cw]
    b1_ref[...] = y[:, cw:2 * cw]
    b2_ref[...] = y[:, 2 * cw:3 * cw]
    scale = C_HEAD_DIM ** -0.5
    if rope:
        cs = cos_ref[...]
        sn = sin_ref[...]
        for j in range(wq // LANES):
            xq = q[:, j * LANES:(j + 1) * LANES]
            pq_ref[:, j * LANES:(j + 1) * LANES] = ((xq * cs + _swap16(xq) * sn) * scale).astype(BF16)
        k = k * cs + _swap16(k) * sn
    else:
        pq_ref[...] = (q * scale).astype(BF16)
    pk_ref[...] = _head_pair_variants(k)
    pv_ref[...] = _head_pair_variants(v)


def _inproj(h2, mod, w, idx, conv_w, conv_b, cos_t, sin_t, rows_per_mod, rows_per_seq, rope, widths, tm):
    r, d = h2.shape
    wa, wb, wq, wk = widths
    tpm = rows_per_mod // tm
    tps = rows_per_seq // tm
    r8 = tm // SUBLANES
    n8 = r // SUBLANES
    body = functools.partial(_inproj_body, wa=wa, wb=wb, wq=wq, wk=wk, rope=rope, tps=tps)
    ow = (wa, wb // 3, wb // 3, wb // 3, wq, 4 * wk, 4 * wk)
    od = (F32, F32, F32, F32, BF16, BF16, BF16)
    return pl.pallas_call(
        body,
        out_shape=[jax.ShapeDtypeStruct((r, n), t) for n, t in zip(ow, od)],
        grid=(r // tm,),
        in_specs=[pl.BlockSpec((tm, d), lambda i: (i, 0)),
                  pl.BlockSpec((SUBLANES, d), lambda i: (jnp.maximum(i * r8 - 1, 0), 0)),
                  pl.BlockSpec((SUBLANES, d), lambda i: (jnp.minimum((i + 1) * r8, n8 - 1), 0)),
                  pl.BlockSpec((1, N_MOD, d), lambda i: (i // tpm, 0, 0)),
                  _layer_spec(w, idx), _const_spec((3, wb)), _const_spec((1, wb)),
                  pl.BlockSpec((tm, LANES), lambda i: (i % tps, 0)),
                  pl.BlockSpec((tm, LANES), lambda i: (i % tps, 0))],
        out_specs=[pl.BlockSpec((tm, n), lambda i: (i, 0)) for n in ow],
        compiler_params=_cparams(("parallel",)),
        name="inproj",
    )(h2, h2, h2, mod, w, conv_w, conv_b.reshape(1, wb), cos_t, sin_t)


HGRN_CHUNK = 128
HGRN_LEVEL_GROUPS = ((64, 32), (16, 8), (4, 2, 1))


def _pivot_rows(b_ref, c, r0):
    t, w = b_ref.shape
    g = 2 * c
    if g >= SUBLANES:
        rows = [jnp.broadcast_to(b_ref[s + r0:s + r0 + 1, :], (g, w)) for s in range(0, t, g)]
        return rows[0] if len(rows) == 1 else jnp.concatenate(rows, axis=0)
    b = b_ref[...]
    row = lax.broadcasted_iota(jnp.int32, (t, 1), 0) % g
    out = b
    for m in range(g):
        if m != r0:
            out = jnp.where(row == m, pltpu.roll(b, (m - r0) % t, 0), out)
    return out


def _split3_bf16(x):
    c = 65537.0
    t = c * x
    hi = t - (t - x)
    rest = x - hi
    t = c * rest
    mid = t - (t - rest)
    lo = rest - mid
    return hi.astype(BF16), mid.astype(BF16), lo.astype(BF16)


def _hgrn_gates(z, lb, tri, b_ref, k_ref):
    e = jnp.exp(-jnp.abs(z))
    r = 1.0 / (1.0 + e)
    pos = z >= 0.0
    sig_p = jnp.where(pos, r, e * r)
    sig_n = jnp.where(pos, e * r, r)
    logf = jnp.log2(lb + (1.0 - lb) * sig_p)
    k_ref[...] = ((1.0 - lb) * sig_n).astype(BF16)
    bb = jnp.dot(tri, jnp.concatenate(_split3_bf16(logf), axis=1), preferred_element_type=F32)
    w = z.shape[1]
    b_ref[...] = bb[:, 0:w] + bb[:, w:2 * w] + bb[:, 2 * w:3 * w]


def _hgrn_diag(q, k_ref, qh_ref, att_ref):
    t = q.shape[0]
    qh_ref[...] = q.astype(BF16)
    ri = lax.broadcasted_iota(jnp.int32, (t, t), 0)
    ci = lax.broadcasted_iota(jnp.int32, (t, t), 1)
    att_ref[...] = jnp.where(ri == ci, jnp.sum(q * k_ref[...].astype(F32), axis=-1, keepdims=True), 0.0)


def _hgrn_levels(lvl, sgn_ref, b_ref, k_ref, qh_ref, att_ref, cs, reverse):
    b, qh, kh = b_ref[...], qh_ref[...], k_ref[...]
    prods = []
    for c in cs:
        level = c.bit_length() - 1
        piv = _pivot_rows(b_ref, c, c if reverse else c - 1)
        dec = jnp.exp2((b - piv) * sgn_ref[level]).astype(BF16)
        prods.append((level, _bdot_nt(qh * dec, kh * dec)))
    att = att_ref[...]
    for level, a in prods:
        att = jnp.where(lvl == level, a, att)
    att_ref[...] = att


def _hgrn_finish(q, v, st, b_ref, k_ref, att_ref, reverse):
    t = q.shape[0]
    b = b_ref[...]
    vb = v.astype(BF16)
    out = _bdot_nt(q * jnp.exp2(b), st) + jnp.dot(att_ref[...].astype(BF16), vb, preferred_element_type=F32)
    blast = b[0:1, :] if reverse else b[t - 1:t, :]
    ke = k_ref[...] * jnp.exp2(blast - b).astype(BF16)
    return out, st * jnp.exp2(blast) + jnp.dot(vb.T, ke, preferred_element_type=F32)


def _hgrn_body(qf_ref, vf_ref, zf_ref, qb_ref, vb_ref, zb_ref, lb_ref, s0_ref, tri_ref, lvl_ref, sgn_ref,
               of_ref, ob_ref, sfin_ref, st_ref, b_ref, k_ref, qh_ref, att_ref, *, nc, nh):
    i = pl.program_id(1)

    @pl.when(i == 0)
    def _():
        st_ref[...] = s0_ref[0]

    def step(j, carry):
        fwd = pl.ds(pl.multiple_of(j * HGRN_CHUNK, HGRN_CHUNK), HGRN_CHUNK)
        bwd = pl.ds(pl.multiple_of((nc - 1 - j) * HGRN_CHUNK, HGRN_CHUNK), HGRN_CHUNK)
        streams = ((qf_ref, vf_ref, zf_ref, of_ref, fwd), (qb_ref, vb_ref, zb_ref, ob_ref, bwd))
        chains = [(h, d) for h in range(nh) for d in range(2)]
        cols = lambda h: slice(h * A_DK, (h + 1) * A_DK)
        for h, d in chains:
            z_r, rows = streams[d][2], streams[d][4]
            _hgrn_gates(z_r[0, rows, cols(h)], lb_ref[d:d + 1, cols(h)], tri_ref[d], b_ref.at[h, d], k_ref.at[h, d])
        for h, d in chains:
            q_r, rows = streams[d][0], streams[d][4]
            _hgrn_diag(q_r[0, rows, cols(h)], k_ref.at[h, d], qh_ref.at[h, d], att_ref.at[h, d])
        for cs in HGRN_LEVEL_GROUPS:
            for h, d in chains:
                _hgrn_levels(lvl_ref[d], sgn_ref.at[d], b_ref.at[h, d], k_ref.at[h, d], qh_ref.at[h, d],
                             att_ref.at[h, d], cs, reverse=(d == 1))
        for h, d in chains:
            q_r, v_r, _, o_r, rows = streams[d]
            out, st_new = _hgrn_finish(q_r[0, rows, cols(h)], v_r[0, rows, cols(h)], st_ref[h, d], b_ref.at[h, d],
                                       k_ref.at[h, d], att_ref.at[h, d], reverse=(d == 1))
            o_r[0, rows, cols(h)] = out
            st_ref[h, d] = st_new
        return carry

    lax.fori_loop(0, nc, step, 0)

    @pl.when(i == pl.num_programs(1) - 1)
    def _():
        sfin_ref[0] = st_ref[...]


def _hgrn(pa3, lb, s0, t):
    bsz, length, _ = pa3.shape
    w = lb.shape[1]
    nh = w // A_DK
    n = length // t
    ii = jnp.arange(HGRN_CHUNK, dtype=jnp.int32)
    tri = jnp.stack([ii[:, None] >= ii[None, :], ii[:, None] <= ii[None, :]]).astype(BF16)
    top = 31 - lax.clz(ii[:, None] ^ ii[None, :])
    lvl = jnp.stack([jnp.where(ii[:, None] > ii[None, :], top, -1), jnp.where(ii[:, None] < ii[None, :], top, -1)])
    nlev = HGRN_CHUNK.bit_length() - 1
    later = ((ii[None, :, None] >> jnp.arange(nlev, dtype=jnp.int32)[:, None, None]) & 1) == 1
    sgn_f = jnp.broadcast_to(jnp.where(later, 1.0, -1.0).astype(F32), (nlev, HGRN_CHUNK, A_DK))
    sgn = jnp.stack([sgn_f, -sgn_f])

    def col(off, rev):
        if rev:
            return pl.BlockSpec((1, t, w), lambda b, i: (b, n - 1 - i, off))
        return pl.BlockSpec((1, t, w), lambda b, i: (b, i, off))

    st_spec = pl.BlockSpec((1, nh, 2, A_DK, A_DK), lambda b, i: (b, 0, 0, 0, 0))
    sq = pl.BlockSpec((2, HGRN_CHUNK, HGRN_CHUNK), lambda b, i: (0, 0, 0))
    return pl.pallas_call(
        functools.partial(_hgrn_body, nc=t // HGRN_CHUNK, nh=nh),
        out_shape=[jax.ShapeDtypeStruct((bsz, length, w), F32),
                   jax.ShapeDtypeStruct((bsz, length, w), F32),
                   jax.ShapeDtypeStruct((bsz, nh, 2, A_DK, A_DK), F32)],
        grid=(bsz, n),
        in_specs=[col(0, False), col(1, False), col(3, False),
                  col(0, True), col(1, True), col(4, True),
                  pl.BlockSpec((2, w), lambda b, i: (0, 0)),
                  st_spec, sq, sq,
                  pl.BlockSpec((2, nlev, HGRN_CHUNK, A_DK), lambda b, i: (0, 0, 0, 0))],
        out_specs=[col(0, False), col(0, True), st_spec],
        scratch_shapes=[pltpu.VMEM((nh, 2, A_DK, A_DK), F32),
                        pltpu.VMEM((nh, 2, HGRN_CHUNK, A_DK), F32),
                        pltpu.VMEM((nh, 2, HGRN_CHUNK, A_DK), BF16),
                        pltpu.VMEM((nh, 2, HGRN_CHUNK, A_DK), BF16),
                        pltpu.VMEM((nh, 2, HGRN_CHUNK, HGRN_CHUNK), F32)],
        compiler_params=_cparams(("parallel", "arbitrary")),
        name="hgrn",
    )(pa3, pa3, pa3, pa3, pa3, pa3, lb, s0, tri, lvl, sgn)


def _hyena_feats(length, order=None):
    n = jnp.arange(2 * length) if order is None else order
    pos = jnp.where(n < length, n, 2 * length - n).astype(F32)
    t = pos / (length - 1)
    w = 2.0 * math.pi * pos / length
    bands = jnp.linspace(1e-4, B_BANDS - 1, B_BANDS, dtype=F32)
    feats = jnp.concatenate([t[None, :], jnp.cos(bands[:, None] * w[None, :]), -jnp.sin(bands[:, None] * w[None, :])],
                            axis=0)
    return jnp.pad(feats, ((0, B_FFN - B_EMB), (0, 0))), t[:, None]


def _decay_rates(width):
    max_decay = math.log(B_TARGET) / B_FAST_DECAY
    min_decay = math.log(B_TARGET) / B_SLOW_DECAY
    return jnp.abs(jnp.linspace(min_decay, max_decay, width, dtype=F32)).reshape(1, width)


def _filter_body(f_ref, t_ref, w1_ref, b1_ref, f1_ref, w2_ref, b2_ref, f2_ref, w3_ref, rate_ref, o_ref, l1_ref,
                 *, tr, length):
    i = pl.program_id(0)
    h = jnp.sin(f1_ref[...] * (_hdot(w1_ref[...], f_ref[...]) + b1_ref[...]))
    h = jnp.sin(f2_ref[...] * (_hdot(w2_ref[...], h) + b2_ref[...]))
    h = jnp.dot(h.T.astype(BF16), w3_ref[...], preferred_element_type=F32)
    dec = jnp.exp(-(t_ref[...] * rate_ref[...]))
    h = h * jnp.concatenate([dec] * B_ORDER, axis=1)
    row = i * tr + lax.broadcasted_iota(jnp.int32, (tr, 1), 0)
    h = jnp.where(row == length, 0.0, h)
    o_ref[...] = h

    @pl.when(i == 0)
    def _():
        l1_ref[...] = jnp.zeros_like(l1_ref)

    l1_ref[...] += jnp.sum(jnp.abs(h), axis=0, keepdims=True)


def _hyena_filter(feats_t, tcol, w1, b1, f1, w2, b2, f2, w3, rates, length):
    width = rates.shape[1]
    ow = B_ORDER * width
    tr = min(512, length)
    nblk = 2 * length // tr
    w1t = jnp.pad(w1, ((0, B_FFN - B_EMB), (0, 0))).T
    body = functools.partial(_filter_body, tr=tr, length=length)
    small = lambda shape: pl.BlockSpec(shape, lambda i: (0, 0))
    col = lambda v: v.reshape(-1, 1)
    return pl.pallas_call(
        body,
        out_shape=[jax.ShapeDtypeStruct((2 * length, ow), F32), jax.ShapeDtypeStruct((1, ow), F32)],
        grid=(nblk,),
        in_specs=[pl.BlockSpec((B_FFN, tr), lambda i: (0, i)),
                  pl.BlockSpec((tr, 1), lambda i: (i, 0)),
                  small((B_FFN, B_FFN)), small((B_FFN, 1)), small((B_FFN, 1)),
                  small((B_FFN, B_FFN)), small((B_FFN, 1)), small((B_FFN, 1)),
                  pl.BlockSpec((B_FFN, ow), lambda i: (0, (2 * i) // nblk)),
                  small((1, width))],
        out_specs=[pl.BlockSpec((tr, ow), lambda i: (i, 0)), pl.BlockSpec((1, ow), lambda i: (0, 0))],
        compiler_params=_cparams(("arbitrary",)),
        name="hyena_filter",
    )(feats_t, tcol, w1t, col(b1), col(f1), w2.T, col(b2), col(f2), w3.astype(BF16), rates)


def _dft_tables(length):
    n = 2 * length
    n2 = DFT_N2
    n1 = n // n2
    n1h = n1 // 2
    k1 = jnp.arange(n1, dtype=jnp.int32)
    j2 = jnp.arange(n2, dtype=jnp.int32)
    ang1 = ((k1[:, None] * k1[None, :]) % n1).astype(F32) * (2.0 * math.pi / n1)
    ang2 = (j2[:, None] * k1[None, :]).astype(F32) * (2.0 * math.pi / n)
    c1, s1, c2, s2 = jnp.cos(ang1), jnp.sin(ang1), jnp.cos(ang2), jnp.sin(ang2)
    er = c1[None] * c2[:, :, None] - s1[None] * s2[:, :, None]
    ei = -(c1[None] * s2[:, :, None] + s1[None] * c2[:, :, None])
    er_t = c1[None] * c2[:, None, :] - s1[None] * s2[:, None, :]
    ei_t = -(c1[None] * s2[:, None, :] + s1[None] * c2[:, None, :])
    a_f = jnp.concatenate([er, ei], axis=1)
    erh, eih = er[:, :, :n1h], ei[:, :, :n1h]
    a_z = jnp.concatenate([jnp.concatenate([erh, -eih], axis=2),
                           jnp.concatenate([eih, erh], axis=2)], axis=1)
    erth, eith = er_t[:, :n1h, :], ei_t[:, :n1h, :]
    a_inv = jnp.concatenate([jnp.concatenate([erth, eith], axis=2),
                             jnp.concatenate([-eith, erth], axis=2)], axis=1) / n
    kk = jnp.arange(n2, dtype=jnp.int32)
    ang2 = ((kk[:, None] * kk[None, :]) % n2).astype(F32) * (-2.0 * math.pi / n2)
    fr, fi = jnp.cos(ang2), jnp.sin(ang2)
    f2 = jnp.concatenate([jnp.concatenate([fr, -fi], axis=1), jnp.concatenate([fi, fr], axis=1)], axis=0)
    return a_f.astype(BF16), a_z.astype(BF16), a_inv.astype(BF16), f2.astype(BF16), f2.T.astype(BF16)


DFT_BW = 512


def _to_slabs(x_ref, slab_ref, scale=None):
    rows, nb, w = x_ref.shape
    for lt in range(w // LANES):
        x = x_ref[:, :, lt * LANES:(lt + 1) * LANES]
        if scale is not None:
            x = x * scale[:, lt * LANES:(lt + 1) * LANES]
        slab_ref[lt] = x.reshape(rows * nb, LANES)


def _slab_rows(slab_ref, j, rows, nb):
    return jnp.concatenate([slab_ref[lt, pl.ds(j, rows, stride=nb), :] for lt in range(slab_ref.shape[0])], axis=1)


def _dfta_fwd_body(x_ref, a_ref, s_ref, o_ref, xs_ref, ys_ref):
    rows, nb, w = x_ref.shape
    m = o_ref.shape[0]
    _to_slabs(x_ref, xs_ref, s_ref[...])
    for j in range(nb):
        y = jnp.dot(a_ref[j], _slab_rows(xs_ref, j, rows, nb).astype(BF16), preferred_element_type=F32)
        for lt in range(w // LANES):
            ys_ref[lt, pl.ds(j, m, stride=nb), :] = y[:, lt * LANES:(lt + 1) * LANES]
    for lt in range(w // LANES):
        o_ref[:, :, lt * LANES:(lt + 1) * LANES] = ys_ref[lt].reshape(m, nb, LANES)


def _dfta_fwd(x3, a, scale):
    rows, n2, cw = x3.shape
    m = a.shape[1]
    nb, bwid = SUBLANES, DFT_BW
    return pl.pallas_call(
        _dfta_fwd_body,
        out_shape=jax.ShapeDtypeStruct((m, n2, cw), F32),
        grid=(n2 // nb, cw // bwid),
        in_specs=[pl.BlockSpec((rows, nb, bwid), lambda i, c: (0, i, c)),
                  pl.BlockSpec((nb, m, rows), lambda i, c: (i, 0, 0)),
                  pl.BlockSpec((1, bwid), lambda i, c: (0, c))],
        out_specs=pl.BlockSpec((m, nb, bwid), lambda i, c: (0, i, c)),
        scratch_shapes=[pltpu.VMEM((bwid // LANES, rows * nb, LANES), F32),
                        pltpu.VMEM((bwid // LANES, m * nb, LANES), F32)],
        compiler_params=_cparams(("parallel", "parallel")),
        name="dft_stage_a",
    )(x3, a, scale)


def _dfta_inv_body(u_ref, a_ref, xn_ref, z_ref, bias_ref, o_ref, us_ref, ys_ref):
    k, nb, w = u_ref.shape
    m = o_ref.shape[0]
    _to_slabs(u_ref, us_ref)
    for j in range(nb):
        y = jnp.dot(a_ref[j], _slab_rows(us_ref, j, k, nb).astype(BF16), preferred_element_type=F32)
        for lt in range(w // LANES):
            ys_ref[lt, pl.ds(j, m, stride=nb), :] = y[:, lt * LANES:(lt + 1) * LANES]
    for lt in range(w // LANES):
        sl = slice(lt * LANES, (lt + 1) * LANES)
        y = ys_ref[lt].reshape(m, nb, LANES)
        o_ref[:, :, sl] = xn_ref[:, :, sl] * (y + bias_ref[:, sl] * z_ref[:, :, sl])


def _dfta_inv(u3, a_inv, xn3, z3, bias):
    n2, m, k = a_inv.shape
    cw = u3.shape[2]
    nb, bwid = SUBLANES, DFT_BW
    return pl.pallas_call(
        _dfta_inv_body,
        out_shape=jax.ShapeDtypeStruct((m, n2, cw), F32),
        grid=(n2 // nb, cw // bwid),
        in_specs=[pl.BlockSpec((k, nb, bwid), lambda i, c: (0, i, c)),
                  pl.BlockSpec((nb, m, k), lambda i, c: (i, 0, 0)),
                  pl.BlockSpec((m, nb, bwid), lambda i, c: (0, i, c)),
                  pl.BlockSpec((m, nb, bwid), lambda i, c: (0, i, c)),
                  pl.BlockSpec((1, bwid), lambda i, c: (0, c))],
        out_specs=pl.BlockSpec((m, nb, bwid), lambda i, c: (0, i, c)),
        scratch_shapes=[pltpu.VMEM((bwid // LANES, k * nb, LANES), F32),
                        pltpu.VMEM((bwid // LANES, m * nb, LANES), F32)],
        compiler_params=_cparams(("parallel", "parallel")),
        name="dft_stage_a_inv",
    )(u3, a_inv, xn3, z3, bias)


def _dfta_inv_fwd_body(u_ref, ai_ref, af_ref, xn_ref, z_ref, bias_ref, o_ref, t_ref, us_ref, ys_ref, ts_ref):
    k, nb, w = u_ref.shape
    m = o_ref.shape[0]
    _to_slabs(u_ref, us_ref)
    for j in range(nb):
        y = jnp.dot(ai_ref[j], _slab_rows(us_ref, j, k, nb).astype(BF16), preferred_element_type=F32)
        for lt in range(w // LANES):
            ys_ref[lt, pl.ds(j, m, stride=nb), :] = y[:, lt * LANES:(lt + 1) * LANES]
    for lt in range(w // LANES):
        sl = slice(lt * LANES, (lt + 1) * LANES)
        zn = xn_ref[:, :, sl] * (ys_ref[lt].reshape(m, nb, LANES) + bias_ref[:, sl] * z_ref[:, :, sl])
        o_ref[:, :, sl] = zn
        ys_ref[lt] = zn.reshape(m * nb, LANES)
    for j in range(nb):
        t = jnp.dot(af_ref[j], _slab_rows(ys_ref, j, m, nb).astype(BF16), preferred_element_type=F32)
        for lt in range(w // LANES):
            ts_ref[lt, pl.ds(j, k, stride=nb), :] = t[:, lt * LANES:(lt + 1) * LANES]
    for lt in range(w // LANES):
        t_ref[:, :, lt * LANES:(lt + 1) * LANES] = ts_ref[lt].reshape(k, nb, LANES)


def _dfta_inv_fwd(u3, a_inv, a_fwd, xn3, z3, bias):
    n2, m, k = a_inv.shape
    cw = u3.shape[2]
    nb, bwid = SUBLANES, DFT_BW
    blk = lambda r: pl.BlockSpec((r, nb, bwid), lambda i, c: (0, i, c))
    slab = lambda r: pltpu.VMEM((bwid // LANES, r * nb, LANES), F32)
    return pl.pallas_call(
        _dfta_inv_fwd_body,
        out_shape=[jax.ShapeDtypeStruct((m, n2, cw), F32), jax.ShapeDtypeStruct((k, n2, cw), F32)],
        grid=(n2 // nb, cw // bwid),
        in_specs=[blk(k),
                  pl.BlockSpec((nb, m, k), lambda i, c: (i, 0, 0)),
                  pl.BlockSpec((nb, k, m), lambda i, c: (i, 0, 0)),
                  blk(m), blk(m),
                  pl.BlockSpec((1, bwid), lambda i, c: (0, c))],
        out_specs=[blk(m), blk(k)],
        scratch_shapes=[slab(k), slab(m), slab(k)],
        compiler_params=_cparams(("parallel", "parallel")),
        name="dft_stage_a_inv_fwd",
    )(u3, a_inv, a_fwd, xn3, z3, bias)


def _filter_stage_a_body(f_ref, t_ref, w1_ref, b1_ref, f1_ref, w2_ref, b2_ref, f2_ref, w3_ref, rate_ref, a_ref,
                         o_ref, l1_ref, xs_ref, ys_ref):
    i = pl.program_id(0)
    m, nb, ow = o_ref.shape
    rn = xs_ref.shape[1]
    rows = rn // nb
    h = jnp.sin(f1_ref[...] * (_hdot(w1_ref[...], f_ref[...]) + b1_ref[...]))
    h = jnp.sin(f2_ref[...] * (_hdot(w2_ref[...], h) + b2_ref[...]))
    ht = h.T.astype(BF16)
    half = rn // 2
    x = jnp.concatenate([jnp.dot(ht[:half], w3_ref[:, :ow], preferred_element_type=F32),
                         jnp.dot(ht[half:], w3_ref[:, ow:], preferred_element_type=F32)], axis=0)
    dec = jnp.exp(-(t_ref[...] * rate_ref[...]))
    x = x * jnp.concatenate([dec] * B_ORDER, axis=1)
    r = lax.broadcasted_iota(jnp.int32, (rn, 1), 0)
    x = jnp.where(jnp.logical_and(i == 0, r == half), 0.0, x)

    @pl.when(i == 0)
    def _():
        l1_ref[...] = jnp.zeros_like(l1_ref)

    l1_ref[...] += jnp.sum(jnp.abs(x), axis=0, keepdims=True)
    for lt in range(ow // LANES):
        xs_ref[lt] = x[:, lt * LANES:(lt + 1) * LANES]
    for j in range(nb):
        y = jnp.dot(a_ref[j], _slab_rows(xs_ref, j, rows, nb).astype(BF16), preferred_element_type=F32)
        for lt in range(ow // LANES):
            ys_ref[lt, pl.ds(j, m, stride=nb), :] = y[:, lt * LANES:(lt + 1) * LANES]
    for lt in range(ow // LANES):
        o_ref[:, :, lt * LANES:(lt + 1) * LANES] = ys_ref[lt].reshape(m, nb, LANES)


def _filter_stage_a(w1, b1, f1, w2, b2, f2, w3, rates, a_f, length):
    n = 2 * length
    n1 = n // DFT_N2
    nb = SUBLANES
    nblk = DFT_N2 // nb
    ow = B_ORDER * rates.shape[1]
    order = jnp.arange(n).reshape(n1, nblk, nb).transpose(1, 0, 2).reshape(-1)
    feats_t, tcol = _hyena_feats(length, order)
    w1t = jnp.pad(w1, ((0, B_FFN - B_EMB), (0, 0))).T
    small = lambda shape: pl.BlockSpec(shape, lambda i: (0, 0))
    col = lambda v: v.reshape(-1, 1)
    return pl.pallas_call(
        _filter_stage_a_body,
        out_shape=[jax.ShapeDtypeStruct((2 * n1, DFT_N2, ow), F32), jax.ShapeDtypeStruct((1, ow), F32)],
        grid=(nblk,),
        in_specs=[pl.BlockSpec((B_FFN, n1 * nb), lambda i: (0, i)),
                  pl.BlockSpec((n1 * nb, 1), lambda i: (i, 0)),
                  small((B_FFN, B_FFN)), small((B_FFN, 1)), small((B_FFN, 1)),
                  small((B_FFN, B_FFN)), small((B_FFN, 1)), small((B_FFN, 1)),
                  small((B_FFN, 2 * ow)), small((1, ow // B_ORDER)),
                  pl.BlockSpec((nb, 2 * n1, n1), lambda i: (i, 0, 0))],
        out_specs=[pl.BlockSpec((2 * n1, nb, ow), lambda i: (0, i, 0)), pl.BlockSpec((1, ow), lambda i: (0, 0))],
        scratch_shapes=[pltpu.VMEM((ow // LANES, n1 * nb, LANES), F32),
                        pltpu.VMEM((ow // LANES, 2 * n1 * nb, LANES), F32)],
        compiler_params=_cparams(("arbitrary",)),
        name="filter_stage_a",
    )(feats_t, tcol, w1t, col(b1), col(f1), w2.T, col(b2), col(f2), w3.astype(BF16), rates, a_f)


def _dftc_filter_body(t_ref, f_ref, s_ref, o_ref, *, kb):
    for j in range(kb):
        t = t_ref[:, j]
        t2 = t.reshape(2 * DFT_N2, t.shape[-1])
        y = jnp.dot(f_ref[...], t2.astype(BF16), preferred_element_type=F32) * s_ref[...]
        o_ref[j] = y.reshape(2, DFT_N2, t.shape[-1]).astype(BF16)


def _dftc_filter(t4, f2, scale, kb):
    _, n1, n2, ow = t4.shape
    body = functools.partial(_dftc_filter_body, kb=kb)
    return pl.pallas_call(
        body,
        out_shape=jax.ShapeDtypeStruct((n1, 2, n2, ow), BF16),
        grid=(n1 // kb,),
        in_specs=[pl.BlockSpec((2, kb, n2, ow), lambda i: (0, i, 0, 0)),
                  pl.BlockSpec((2 * n2, 2 * n2), lambda i: (0, 0)),
                  pl.BlockSpec((1, ow), lambda i: (0, 0))],
        out_specs=pl.BlockSpec((kb, 2, n2, ow), lambda i: (i, 0, 0, 0)),
        compiler_params=_cparams(("parallel",)),
        name="dft_stage_c_filter",
    )(t4, f2, scale)


def _dftc_mid_body(t_ref, h_ref, f_ref, fi_ref, o_ref, *, kb):
    cw = t_ref.shape[-1]
    ys = [jnp.dot(f_ref[...], t_ref[:, j].reshape(2 * DFT_N2, cw).astype(BF16), preferred_element_type=F32)
          for j in range(kb)]
    ps = []
    for j, y in enumerate(ys):
        yr, yi = y[:DFT_N2], y[DFT_N2:]
        hr, hi = h_ref[j, 0].astype(F32), h_ref[j, 1].astype(F32)
        ps.append(jnp.concatenate([yr * hr - yi * hi, yr * hi + yi * hr], axis=0).astype(BF16))
    for j, p in enumerate(ps):
        u = jnp.dot(fi_ref[...], p, preferred_element_type=F32)
        o_ref[0, j] = u[:DFT_N2]
        o_ref[1, j] = u[DFT_N2:]


def _dftc_mid(t4, hspec, order, f2, f2inv, kb):
    _, n1, n2, cw = t4.shape
    body = functools.partial(_dftc_mid_body, kb=kb)
    return pl.pallas_call(
        body,
        out_shape=jax.ShapeDtypeStruct((2, n1, n2, cw), F32),
        grid=(n1 // kb,),
        in_specs=[pl.BlockSpec((2, kb, n2, cw), lambda i: (0, i, 0, 0)),
                  pl.BlockSpec((kb, 2, n2, cw), lambda i: (i, 0, 0, order)),
                  pl.BlockSpec((2 * n2, 2 * n2), lambda i: (0, 0)),
                  pl.BlockSpec((2 * n2, 2 * n2), lambda i: (0, 0))],
        out_specs=pl.BlockSpec((2, kb, n2, cw), lambda i: (0, i, 0, 0)),
        compiler_params=_cparams(("parallel",)),
        name="dft_stage_c_mid",
    )(t4, hspec, f2, f2inv)


def _hyena_long(parts, fargs, bias, tables):
    a_f, a_z, a_inv, f2, f2inv = tables
    bsz, length, cw = parts[0].shape
    assert bsz == 2, "the two batch rows ride as real and imaginary parts of one transform"
    n1 = 2 * length // DFT_N2
    ow = B_ORDER * cw
    tf, l1 = _filter_stage_a(*fargs, a_f, length)
    hspec = _dftc_filter(tf.reshape(2, n1, DFT_N2, ow), f2, 1.0 / l1, 2)
    ones = jnp.ones((1, cw), F32)
    z = parts[0].reshape(bsz * n1 // 2, DFT_N2, cw)
    t = _dfta_fwd(z, a_z, ones)
    for o in range(B_ORDER):
        u = _dftc_mid(t.reshape(2, n1, DFT_N2, cw), hspec, o, f2, f2inv, 4).reshape(2 * n1, DFT_N2, cw)
        xn = parts[o + 1].reshape(z.shape)
        if o + 1 < B_ORDER:
            z, t = _dfta_inv_fwd(u, a_inv, a_z, xn, z, bias[o:o + 1])
        else:
            z = _dfta_inv(u, a_inv, xn, z, bias[o:o + 1])
    return z.reshape(bsz, length, cw)


def _ctx_conv_body(v_ref, x1_ref, x2_ref, filt_ref, il1_ref, bias_ref, ff_ref, fz_ref, fi_ref, o_ref, *, n, cw):
    hf = _hdot(ff_ref[...], filt_ref[...] * il1_ref[...])
    z = jnp.concatenate([v_ref[0], v_ref[1]], axis=0)
    for o, x_ref in enumerate((x1_ref, x2_ref)):
        zz = _hdot(fz_ref[...], z)
        zr, zi = zz[:n], zz[n:]
        hr, hi = hf[:n, o * cw:(o + 1) * cw], hf[n:, o * cw:(o + 1) * cw]
        y = _hdot(fi_ref[...], jnp.concatenate([zr * hr - zi * hi, zr * hi + zi * hr], axis=0))
        xn = jnp.concatenate([x_ref[0], x_ref[1]], axis=0)
        z = xn * (y + bias_ref[o:o + 1, :] * z)
    half = n // 2
    o_ref[0] = z[:half]
    o_ref[1] = z[half:]


def _hyena_ctx(parts, filt, l1, bias):
    bsz, length, cw = parts[0].shape
    assert bsz == 2
    n = 2 * length
    k = jnp.arange(n, dtype=jnp.int32)
    ang = ((k[:, None] * k[None, :]) % n).astype(F32) * (-2.0 * math.pi / n)
    cr, ci = jnp.cos(ang), jnp.sin(ang)
    ff = jnp.concatenate([cr, ci], axis=0)
    crh, cih = cr[:, :length], ci[:, :length]
    fz = jnp.concatenate([jnp.concatenate([crh, -cih], axis=1), jnp.concatenate([cih, crh], axis=1)], axis=0)
    fi = fz.T / n
    body = functools.partial(_ctx_conv_body, n=n, cw=cw)
    full = lambda a: pl.BlockSpec(a.shape, lambda i: (0,) * a.ndim)
    args = (parts[0], parts[1], parts[2], filt, 1.0 / l1, bias, ff, fz, fi)
    return pl.pallas_call(
        body,
        out_shape=jax.ShapeDtypeStruct((bsz, length, cw), F32),
        grid=(1,),
        in_specs=[full(a) for a in args],
        out_specs=pl.BlockSpec((bsz, length, cw), lambda i: (0, 0, 0)),
        compiler_params=_cparams(("arbitrary",)),
        name="hyena_ctx",
    )(*args)


def _rope_tables(length):
    n_rows = length // GRID_W
    row = jnp.repeat(jnp.arange(n_rows), GRID_W).astype(F32)
    col = jnp.tile(jnp.arange(GRID_W), n_rows).astype(F32)
    nf = C_HEAD_DIM // 4
    inv = ROPE_BASE ** (-jnp.arange(nf, dtype=F32) * 2.0 / (C_HEAD_DIM // 2))
    ar, ac = row[:, None] * inv, col[:, None] * inv
    cos_h = jnp.concatenate([jnp.cos(ar), jnp.cos(ar), jnp.cos(ac), jnp.cos(ac)], axis=-1)
    sin_h = jnp.concatenate([-jnp.sin(ar), jnp.sin(ar), -jnp.sin(ac), jnp.sin(ac)], axis=-1)
    reps = LANES // C_HEAD_DIM
    return jnp.tile(cos_h, (1, reps)), jnp.tile(sin_h, (1, reps))


def _softmax_heads(q_ref, keys, vals, sink_ref, masks, o_ref):
    rows = q_ref.shape[1]
    top = lax.broadcasted_iota(jnp.int32, (2 * rows, 1), 0) < rows
    chains = [(g, half) for g in range(C_KV_HEADS) for half in range(2)]
    off = lambda g, half: (2 * g + half) * LANES
    lane_blocks = lambda xs: [x[:, j * LANES:(j + 1) * LANES] for x in xs for j in range(x.shape[1] // LANES)]
    qs = [jnp.concatenate([q_ref[0, :, (2 * g) * LANES:(2 * g + 1) * LANES],
                           q_ref[0, :, (2 * g + 1) * LANES:(2 * g + 2) * LANES]], axis=0)
          for g in range(C_KV_HEADS)]
    scores, shifts, sinks = {}, {}, {}
    for g, half in chains:
        ss = []
        for k_ref, msk in zip(keys, masks):
            s = lax.dot_general(qs[g], k_ref[0, :, off(g, half):off(g, half) + LANES], (((1,), (1,)), ((), ())),
                                preferred_element_type=F32)
            ss.append(s if msk is None else jnp.where(jnp.concatenate([msk, msk], axis=0), s, NEG_BIG))
        sk = jnp.where(top, sink_ref[C_GROUP * g + half], sink_ref[C_GROUP * g + 2 + half])
        mx = functools.reduce(jnp.maximum, lane_blocks(ss))
        scores[g, half], sinks[g, half] = ss, sk
        shifts[g, half] = jnp.maximum(jnp.max(mx, axis=-1, keepdims=True), sk)
    probs, rdens = {}, {}
    for c in chains:
        ps = [jnp.exp(s - shifts[c]) for s in scores[c]]
        den = jnp.sum(functools.reduce(jnp.add, lane_blocks(ps)), axis=-1, keepdims=True) + jnp.exp(sinks[c] - shifts[c])
        probs[c], rdens[c] = [p.astype(BF16) for p in ps], 1.0 / den
    outs = {}
    for g, half in chains:
        o = None
        for p, v_ref in zip(probs[g, half], vals):
            t = jnp.dot(p, v_ref[0, :, off(g, half):off(g, half) + LANES], preferred_element_type=F32)
            o = t if o is None else o + t
        outs[g, half] = o * rdens[g, half]
    for g in range(C_KV_HEADS):
        acc = outs[g, 0] + outs[g, 1]
        o_ref[0, :, (2 * g) * LANES:(2 * g + 1) * LANES] = acc[:rows]
        o_ref[0, :, (2 * g + 1) * LANES:(2 * g + 2) * LANES] = acc[rows:]


def _attn_body(sink_ref, q_ref, kp_ref, kc_ref, kn_ref, kx_ref, vp_ref, vc_ref, vn_ref, vx_ref, o_ref):
    n = pl.program_id(1)
    blk = q_ref.shape[1]
    ri = lax.broadcasted_iota(jnp.int32, (blk, blk), 0)
    ci = lax.broadcasted_iota(jnp.int32, (blk, blk), 1)
    mask_p = jnp.logical_and(ci >= ri, n > 0)
    mask_n = jnp.logical_and(ci <= ri, n < pl.num_programs(1) - 1)
    _softmax_heads(q_ref, (kp_ref, kc_ref, kn_ref, kx_ref), (vp_ref, vc_ref, vn_ref, vx_ref), sink_ref,
                   (mask_p, None, mask_n, None), o_ref)


def _attention(q3, k3, v3, kx3, vx3, sink):
    bsz, length, wq = q3.shape
    wk = k3.shape[2]
    lc = kx3.shape[1]
    nb = length // C_BLOCK
    cur = lambda w: pl.BlockSpec((1, C_BLOCK, w), lambda b, i: (b, i, 0))
    prv = lambda w: pl.BlockSpec((1, C_BLOCK, w), lambda b, i: (b, jnp.maximum(i - 1, 0), 0))
    nxt = lambda w: pl.BlockSpec((1, C_BLOCK, w), lambda b, i: (b, jnp.minimum(i + 1, nb - 1), 0))
    ctx = pl.BlockSpec((1, lc, wk), lambda b, i: (b, 0, 0))
    return pl.pallas_call(
        _attn_body,
        out_shape=jax.ShapeDtypeStruct((bsz, length, wq), F32),
        grid=(bsz, nb),
        in_specs=[pl.BlockSpec(memory_space=pltpu.SMEM), cur(wq), prv(wk), cur(wk), nxt(wk), ctx,
                  prv(wk), cur(wk), nxt(wk), ctx],
        out_specs=cur(wq),
        compiler_params=_cparams(("parallel", "parallel")),
        name="window_attn",
    )(sink, q3, k3, k3, k3, kx3, v3, v3, v3, vx3)


def _ctx_attn_body(sink_ref, q_ref, k_ref, v_ref, o_ref):
    _softmax_heads(q_ref, (k_ref,), (v_ref,), sink_ref, (None,), o_ref)


def _ctx_attention(q3, k3, v3, sink):
    bsz, lc, wq = q3.shape
    wk = k3.shape[2]
    spec = lambda w: pl.BlockSpec((1, lc, w), lambda b: (b, 0, 0))
    return pl.pallas_call(
        _ctx_attn_body,
        out_shape=jax.ShapeDtypeStruct((bsz, lc, wq), F32),
        grid=(bsz,),
        in_specs=[pl.BlockSpec(memory_space=pltpu.SMEM), spec(wq), spec(wk), spec(wk)],
        out_specs=spec(wq),
        compiler_params=_cparams(("parallel",)),
        name="ctx_attn",
    )(sink, q3, k3, v3)


def _merge_body(h_ref, mod_ref, of_ref, ob_ref, ga_ref, yb_ref, yc_ref, wg_ref, wbr_ref, wo_ref, nw_ref,
                g_ref, b_ref, o_ref, *, alpha):
    h = h_ref[...]
    d = h.shape[1]
    u = (h * (1.0 + mod_ref[0, 4:5, :]) + mod_ref[0, 3:4, :]).astype(BF16)
    gates = [jnp.dot(u, wg_ref[:, n * d:(n + 1) * d], preferred_element_type=F32) for n in range(3)]
    branches = [jnp.dot(y_ref[...].astype(BF16), wbr_ref[n], preferred_element_type=F32)
                for n, y_ref in ((1, yb_ref), (2, yc_ref))]
    o = of_ref[...] + ob_ref[...]
    heads = []
    for hh in range(o.shape[1] // A_DK):
        oh = o[:, hh * A_DK:(hh + 1) * A_DK]
        ms = jnp.mean(oh * oh, axis=-1, keepdims=True)
        heads.append(oh * lax.rsqrt(ms + RMS_EPS) * nw_ref[:, hh * A_DK:(hh + 1) * A_DK])
    ya = jnp.concatenate(heads, axis=1) * _silu(ga_ref[...])
    branches.insert(0, jnp.dot(ya.astype(BF16), wbr_ref[0], preferred_element_type=F32))
    m = jax.nn.sigmoid(gates[0]) * branches[0]
    for n in range(1, 3):
        m = m + jax.nn.sigmoid(gates[n]) * branches[n]
    y = jnp.dot(m.astype(BF16), wo_ref[...], preferred_element_type=F32)
    r = alpha * h + mod_ref[0, 5:6, :] * y
    o_ref[...] = _layer_norm(r, g_ref[...], b_ref[...])


def _merge(h2, mod, of2, ob2, pa2, yb2, yc2, wg, wbr, wo, idx, norm_w, g, b, rows_per_mod, alpha, tm):
    r, d = h2.shape
    bw = of2.shape[1]
    tpm = rows_per_mod // tm
    body = functools.partial(_merge_body, alpha=alpha)
    row = lambda w: pl.BlockSpec((tm, w), lambda i: (i, 0))
    return pl.pallas_call(
        body,
        out_shape=jax.ShapeDtypeStruct((r, d), F32),
        grid=(r // tm,),
        in_specs=[row(d), pl.BlockSpec((1, N_MOD, d), lambda i: (i // tpm, 0, 0)),
                  row(bw), row(bw), pl.BlockSpec((tm, bw), lambda i: (i, 2)), row(bw), row(bw),
                  _layer_spec(wg, idx), _layer_spec(wbr, idx), _layer_spec(wo, idx),
                  _const_spec((1, bw)), _const_spec((1, d)), _const_spec((1, d))],
        out_specs=row(d),
        compiler_params=_cparams(("parallel",)),
        name="merge",
    )(h2, mod, of2, ob2, pa2, yb2, yc2, wg, wbr, wo, norm_w.reshape(1, bw), g.reshape(1, d), b.reshape(1, d))


def kernel(x, c, ctx, c_ctx, ada_w, ada_b, ln_g, ln_b, ffn_w_in, ffn_w_out, mix_w_in, hgrn_lb, hgrn_norm_w,
           hyena_conv_w, hyena_conv_b, hyena_w1, hyena_b1, hyena_f1, hyena_w2, hyena_b2, hyena_f2, hyena_w3,
           hyena_bias, attn_sink, branch_w, out_w):
    bsz, seq, d = x.shape
    lc = ctx.shape[1]
    depth = ada_w.shape[0]
    alpha = (2.0 * depth) ** 0.25
    bw = hgrn_lb.shape[2]
    wk = C_KV_HEADS * C_HEAD_DIM
    widths = (5 * bw, (B_ORDER + 1) * bw, bw, wk)
    off_g = widths[0] + widths[1] + widths[2] + 2 * wk
    tm = 1024
    tmp = 512
    tmc = min(256, lc)

    s = jax.nn.softmax(hgrn_lb.astype(F32), axis=0)
    lower_bounds = jnp.cumsum(s, axis=0) - s[0:1]

    c8 = jnp.zeros((SUBLANES, d), F32).at[:bsz].set(c).at[bsz].set(c_ctx)
    mods = _ada_mod(c8, ada_w, ada_b).reshape(depth, SUBLANES, N_MOD, d)

    cos_t, sin_t = _rope_tables(seq)
    feats_c, tcol_c = _hyena_feats(lc)
    rates = _decay_rates(bw)
    tables = _dft_tables(seq)

    w_in_bf = ffn_w_in.astype(BF16)
    w_out_bf = ffn_w_out.astype(BF16)
    proj_bf = mix_w_in[:, :, :off_g].astype(BF16)
    gate_bf = mix_w_in[:, :, off_g:].astype(BF16)
    br_bf = branch_w.astype(BF16)
    out_bf = out_w.astype(BF16)

    h = x.reshape(bsz * seq, d)
    hc = ctx.reshape(bsz * lc, d)
    for l in range(depth):
        last = l == depth - 1
        mod = mods[l, :bsz]
        modc = mods[l, bsz:bsz + 1]
        ffn = lambda t, mm, m0, j, g, rpm, tt: _ffn(t, mm, m0, w_in_bf, w_out_bf, (l, j), ln_g[l, g],
                                                     ln_b[l, g], rpm, alpha, tt)
        h = ffn(h, mod, 0, 0, 0, seq, tm)
        hc = ffn(hc, modc, 0, 0, 0, bsz * lc, tmc)

        conv = (hyena_conv_w[l], hyena_conv_b[l])
        pa, b0, b1, b2, pq, pk, pv = _inproj(h, mod, proj_bf, (l,), *conv, cos_t, sin_t, seq, seq, True, widths, tmp)
        ca, c0, c1, c2, cq, ck, cv = _inproj(hc, modc, proj_bf, (l,), *conv, cos_t, sin_t, bsz * lc, lc, False, widths,
                                             tmc)

        s0 = jnp.zeros((bsz, A_HEADS, 2, A_DK, A_DK), F32)
        ocf, ocb, s_ctx = _hgrn(ca.reshape(bsz, lc, -1), lower_bounds[l], s0, lc)
        of, ob, _ = _hgrn(pa.reshape(bsz, seq, -1), lower_bounds[l], s_ctx, min(1024, seq))

        parts = [p.reshape(bsz, seq, bw) for p in (b0, b1, b2)]
        fargs = (hyena_w1[l], hyena_b1[l], hyena_f1[l], hyena_w2[l], hyena_b2[l], hyena_f2[l], hyena_w3[l], rates)
        yb = _hyena_long(parts, fargs, hyena_bias[l], tables)

        kx, vx = ck.reshape(bsz, lc, 4 * wk), cv.reshape(bsz, lc, 4 * wk)
        yc = _attention(pq.reshape(bsz, seq, bw), pk.reshape(bsz, seq, 4 * wk), pv.reshape(bsz, seq, 4 * wk),
                        kx, vx, attn_sink[l])

        merge = lambda t, mm, a1, a2, a3, a4, a5, rpm, tt: _merge(
            t, mm, a1, a2, a3, a4, a5, gate_bf, br_bf, out_bf, (l,), hgrn_norm_w[l], ln_g[l, 1], ln_b[l, 1],
            rpm, alpha, tt)
        h = merge(h, mod, of.reshape(-1, bw), ob.reshape(-1, bw), pa, yb.reshape(-1, bw), yc.reshape(-1, bw),
                  seq, tmp)
        h = ffn(h, mod, 6, 1, 2, seq, tm)
        if not last:
            cparts = [p.reshape(bsz, lc, bw) for p in (c0, c1, c2)]
            cfilt, cl1 = _hyena_filter(feats_c, tcol_c, *fargs, lc)
            ycb = _hyena_ctx(cparts, cfilt, cl1, hyena_bias[l])
            ycc = _ctx_attention(cq.reshape(bsz, lc, bw), kx, vx, attn_sink[l])
            hc = merge(hc, modc, ocf.reshape(-1, bw), ocb.reshape(-1, bw), ca, ycb.reshape(-1, bw),
                       ycc.reshape(-1, bw), bsz * lc, tmc)
            hc = ffn(hc, modc, 6, 1, 2, bsz * lc, tmc)
    return h.reshape(bsz, seq, d)
```
